```python
import jax, jax.numpy as jnp
from jax import lax
import numpy as np

D_MODEL = 4096
BATCH = 2
SEQ = 4096
DEPTH = 1
DEC_BATCH = 32
DEC_SEQ = 4
PAST_LEN = 8192
PAGE_SIZE = 128

HEAD_DIM = 128
POOL_WIDTH = D_MODEL // 4
POOL_WINDOWS = (2, 4, 8, 16)
POOL_GROUP = POOL_WIDTH // len(POOL_WINDOWS)
POOL_MAX = max(POOL_WINDOWS)
NSA_WIDTH = D_MODEL - POOL_WIDTH
N_HEADS = NSA_WIDTH // HEAD_DIM
KV_HEADS = 4
GROUP = N_HEADS // KV_HEADS
KV_WIDTH = KV_HEADS * HEAD_DIM
N_BRANCH = 3
D_IN_PROJ = POOL_WIDTH + NSA_WIDTH + 6 * KV_WIDTH + N_BRANCH * N_HEADS
CMP_LEN = 32
CMP_STRIDE = 16
CMP_RATIO = CMP_LEN // CMP_STRIDE
SLC_LEN = 64
N_SEL = 16
WINDOW = 512
Q_BLOCK = 128
ROPE_THETA = 10000.0
D_FF = -(-8 * D_MODEL // (3 * 256)) * 256
LN_EPS = 1e-5
ALPHA = (2 * DEPTH) ** 0.25
BETA = (8 * DEPTH) ** -0.25

kernel_name = "hymba_pool_nsa_deepnorm_step"


def _rope(x, pos):
    half = HEAD_DIM // 2
    inv = ROPE_THETA ** (-jnp.arange(half, dtype=jnp.float32) / half)
    ang = pos.astype(jnp.float32)[:, None] * inv[None, :]
    cos = jnp.cos(ang)[:, None, :]
    sin = jnp.sin(ang)[:, None, :]
    xf = x.astype(jnp.float32)
    x1, x2 = xf[..., :half], xf[..., half:]
    return jnp.concatenate([x1 * cos - x2 * sin, x2 * cos + x1 * sin], -1).astype(x.dtype)


def _layernorm(x, g, b):
    xf = x.astype(jnp.float32)
    mu = jnp.mean(xf, -1, keepdims=True)
    var = jnp.mean(jnp.square(xf - mu), -1, keepdims=True)
    return ((xf - mu) * lax.rsqrt(var + LN_EPS) * g + b).astype(x.dtype)


def _masked_softmax(s, mask, axis):
    s = jnp.where(mask, s, -jnp.inf)
    m = jnp.max(s, axis=axis, keepdims=True)
    m = jnp.where(jnp.isfinite(m), m, 0.0)
    e = jnp.where(mask, jnp.exp(s - m), 0.0)
    return e / jnp.maximum(jnp.sum(e, axis=axis, keepdims=True), 1e-30)


def _project(x, w_in_l, pos):
    B, T, _ = x.shape
    z = jnp.einsum('btd,de->bte', x, w_in_l)
    u = z[..., :POOL_WIDTH]
    o = POOL_WIDTH
    q = _rope(z[..., o:o + NSA_WIDTH].reshape(B, T, N_HEADS, HEAD_DIM), pos)
    q = q.reshape(B, T, KV_HEADS, GROUP, HEAD_DIM)
    o += NSA_WIDTH
    kv = z[..., o:o + 6 * KV_WIDTH].reshape(B, T, 6, KV_HEADS, HEAD_DIM)
    o += 6 * KV_WIDTH
    gates = jax.nn.sigmoid(z[..., o:o + N_BRANCH * N_HEADS].astype(jnp.float32))
    gates = gates.reshape(B, T, KV_HEADS, GROUP, N_BRANCH)
    kc, vc = kv[:, :, 0], kv[:, :, 1]
    ks, vs = _rope(kv[:, :, 2], pos), kv[:, :, 3]
    kw, vw = _rope(kv[:, :, 4], pos), kv[:, :, 5]
    return u, q, kc, vc, ks, vs, kw, vw, gates


def _pool_mix(u_all, pos_out, w_pool_l, scale_l):
    B, Ta, _ = u_all.shape
    n = pos_out.shape[0]
    uf = u_all.astype(jnp.float32)
    cs = jnp.concatenate([jnp.zeros((B, 1, POOL_WIDTH), jnp.float32), jnp.cumsum(uf, axis=1)], axis=1)
    hi = cs[:, Ta - n + 1:]
    cur = uf[:, Ta - n:]
    rows = jnp.arange(Ta - n, Ta, dtype=jnp.int32)
    outs = []
    for g, w in enumerate(POOL_WINDOWS):
        c0, c1 = g * POOL_GROUP, (g + 1) * POOL_GROUP
        lo = cs[:, jnp.maximum(rows + 1 - w, 0), c0:c1]
        cnt = jnp.minimum(pos_out + 1, w).astype(jnp.float32)[None, :, None]
        d = (hi[..., c0:c1] - lo) / cnt - cur[..., c0:c1]
        outs.append(jnp.einsum('btc,ce->bte', d.astype(u_all.dtype), w_pool_l[g]))
    return jnp.concatenate(outs, -1) * scale_l


def _compress(rows, w1, pe, w2):
    B, T = rows.shape[:2]
    n_chunk = T // CMP_STRIDE
    chunks = rows[:, :n_chunk * CMP_STRIDE].reshape(B, n_chunk, CMP_STRIDE, KV_HEADS, HEAD_DIM)
    part = jnp.einsum('bnpgd,apde->abnge', chunks,
                      w1.reshape(CMP_RATIO, CMP_STRIDE, HEAD_DIM, HEAD_DIM))
    n_cmp = n_chunk - CMP_RATIO + 1
    hid = jnp.einsum('pd,pde->e', pe, w1)
    for a in range(CMP_RATIO):
        hid = hid + part[a, :, a:a + n_cmp]
    return jnp.einsum('bnge,ef->bngf', jax.nn.gelu(hid), w2)


def _to_blocks(rows):
    B, T = rows.shape[:2]
    n_slc = -(-T // SLC_LEN)
    rows = jnp.pad(rows, ((0, 0), (0, n_slc * SLC_LEN - T), (0, 0), (0, 0)))
    return rows.reshape(B, n_slc, SLC_LEN, KV_HEADS, HEAD_DIM).transpose(0, 3, 1, 2, 4)


def _nsa_keys(kc_rows, vc_rows, ks_rows, vs_rows, w1k, pek, w2k, w1v, pev, w2v):
    kcb = _compress(kc_rows, w1k, pek, w2k)
    n_cmp = kcb.shape[1]
    cmp_end = jnp.arange(n_cmp, dtype=jnp.int32) * CMP_STRIDE + CMP_LEN - 1
    kcb = _rope(kcb, cmp_end)
    vcb = _compress(vc_rows, w1v, pev, w2v)
    return kcb, vcb, cmp_end, _to_blocks(ks_rows), _to_blocks(vs_rows)


def _nsa_attend(q, q_pos, kcb, vcb, cmp_end, k_sb, v_sb, kw, vw, win_pos, gates):
    B, Tq = q.shape[:2]
    f32 = jnp.float32
    scale = HEAD_DIM ** -0.5
    s_c = jnp.einsum('btgrd,bngd->bgrtn', q, kcb).astype(f32) * scale
    p_c = _masked_softmax(s_c, cmp_end[None, :] <= q_pos[:, None], -1)
    o_c = jnp.einsum('bgrtn,bngd->btgrd', p_c.astype(vcb.dtype), vcb)
    n_slc = k_sb.shape[2]
    blk = jnp.arange(n_slc, dtype=jnp.int32)
    blk_start = blk * SLC_LEN
    cmp_start = cmp_end - (CMP_LEN - 1)
    overlap = ((cmp_start[:, None] < blk_start[None, :] + SLC_LEN)
               & (cmp_end[:, None] >= blk_start[None, :])).astype(f32)
    imp = jnp.einsum('bgrtn,ns->bgts', p_c, overlap)
    cur = q_pos // SLC_LEN
    forced = (blk[None, :] == 0) | (blk[None, :] == cur[:, None]) | (blk[None, :] == cur[:, None] - 1)
    imp = jnp.where(forced, jnp.inf, jnp.where(blk[None, :] <= cur[:, None], imp, -jnp.inf))
    top_v, top_i = lax.top_k(imp, min(N_SEL, n_slc))
    gather = jax.vmap(jax.vmap(lambda kb, ii: kb[ii]))
    k_sel = gather(k_sb, top_i)
    v_sel = gather(v_sb, top_i)
    s_s = jnp.einsum('btgrd,bgtksd->bgrtks', q, k_sel).astype(f32) * scale
    key_pos = top_i[..., None] * SLC_LEN + jnp.arange(SLC_LEN, dtype=jnp.int32)
    mask_s = (top_v > -jnp.inf)[..., None] & (key_pos <= q_pos[:, None, None])
    sh = s_s.shape
    p_s = _masked_softmax(s_s.reshape(sh[:4] + (-1,)),
                          mask_s[:, :, None].reshape(B, KV_HEADS, 1, Tq, -1), -1).reshape(sh)
    o_s = jnp.einsum('bgrtks,bgtksd->btgrd', p_s.astype(v_sel.dtype), v_sel)
    s_w = jnp.einsum('btgrd,bsgd->bgrts', q, kw).astype(f32) * scale
    rel = q_pos[:, None] - win_pos[None, :]
    mask_w = (rel >= 0) & (rel < WINDOW) & (win_pos[None, :] >= 0)
    p_w = _masked_softmax(s_w, mask_w, -1)
    o_w = jnp.einsum('bgrts,bsgd->btgrd', p_w.astype(vw.dtype), vw)
    o = (gates[..., 0:1] * o_c.astype(f32) + gates[..., 1:2] * o_s.astype(f32)
         + gates[..., 2:3] * o_w.astype(f32))
    return o.astype(q.dtype)


def _nsa_prompt(q, gates, kcb, vcb, cmp_end, k_sb, v_sb, kw, vw):
    B, T = q.shape[:2]
    kw_pad = jnp.pad(kw, ((0, 0), (WINDOW, 0), (0, 0), (0, 0)))
    vw_pad = jnp.pad(vw, ((0, 0), (WINDOW, 0), (0, 0), (0, 0)))

    def one_block(c):
        start = c * Q_BLOCK
        qb = lax.dynamic_slice_in_dim(q, start, Q_BLOCK, axis=1)
        gb = lax.dynamic_slice_in_dim(gates, start, Q_BLOCK, axis=1)
        kwb = lax.dynamic_slice_in_dim(kw_pad, start, WINDOW + Q_BLOCK, axis=1)
        vwb = lax.dynamic_slice_in_dim(vw_pad, start, WINDOW + Q_BLOCK, axis=1)
        q_pos = start + jnp.arange(Q_BLOCK, dtype=jnp.int32)
        win_pos = start - WINDOW + jnp.arange(WINDOW + Q_BLOCK, dtype=jnp.int32)
        return _nsa_attend(qb, q_pos, kcb, vcb, cmp_end, k_sb, v_sb, kwb, vwb, win_pos, gb)

    o = lax.map(one_block, jnp.arange(T // Q_BLOCK, dtype=jnp.int32))
    return jnp.moveaxis(o, 0, 1).reshape(B, T, NSA_WIDTH)


def _finish(x, pool_out, nsa_out, w_o_l, ln1_g_l, ln1_b_l, w_gate_l, w_up_l, w_down_l, ln2_g_l, ln2_b_l):
    mixed = jnp.concatenate([pool_out.astype(x.dtype), nsa_out.astype(x.dtype)], -1)
    h = _layernorm(ALPHA * x + jnp.einsum('btd,de->bte', mixed, w_o_l), ln1_g_l, ln1_b_l)
    ff = jax.nn.silu(jnp.einsum('btd,df->btf', h, w_gate_l)) * jnp.einsum('btd,df->btf', h, w_up_l)
    f = jnp.einsum('btf,fd->btd', ff, w_down_l)
    return _layernorm(ALPHA * h + f, ln2_g_l, ln2_b_l)


def setup_inputs(seed: int = 0) -> dict:
    key = jax.random.key(seed)
    k = jax.random.split(key, 32)
    f32 = jnp.float32
    n_pages = PAST_LEN // PAGE_SIZE
    n_phys = (5 * DEC_BATCH * n_pages + 3) // 4
    win_buf = min(WINDOW, PAST_LEN)

    def nrm(kk, shape, scale=1.0):
        return scale * jax.random.normal(kk, shape, f32)

    page_table = jax.random.permutation(k[0], n_phys)[:DEC_BATCH * n_pages]
    page_table = page_table.reshape(DEC_BATCH, n_pages).astype(jnp.int32)
    paged = (DEPTH, n_phys, PAGE_SIZE, KV_HEADS, HEAD_DIM)
    win = (DEPTH, DEC_BATCH, win_buf, KV_HEADS, HEAD_DIM)
    cmp1 = (CMP_LEN * HEAD_DIM) ** -0.5
    return {
        'x_prompt': nrm(k[1], (BATCH, SEQ, D_MODEL)),
        'x_sample': nrm(k[2], (DEC_BATCH, DEC_SEQ, D_MODEL)),
        'cache_k_cmp': nrm(k[3], paged),
        'cache_v_cmp': nrm(k[4], paged),
        'cache_k_slc': nrm(k[5], paged),
        'cache_v_slc': nrm(k[6], paged),
        'state_k_win': nrm(k[7], win),
        'state_v_win': nrm(k[8], win),
        'state_pool': nrm(k[9], (DEPTH, DEC_BATCH, POOL_MAX - 1, POOL_WIDTH)),
        'page_table': page_table,
        'w_in': nrm(k[10], (DEPTH, D_MODEL, D_IN_PROJ), D_MODEL ** -0.5),
        'w_cmp1_k': nrm(k[11], (DEPTH, CMP_LEN, HEAD_DIM, HEAD_DIM), cmp1),
        'pe_cmp_k': nrm(k[12], (DEPTH, CMP_LEN, HEAD_DIM), 0.5),
        'w_cmp2_k': nrm(k[13], (DEPTH, HEAD_DIM, HEAD_DIM), HEAD_DIM ** -0.5),
        'w_cmp1_v': nrm(k[14], (DEPTH, CMP_LEN, HEAD_DIM, HEAD_DIM), cmp1),
        'pe_cmp_v': nrm(k[15], (DEPTH, CMP_LEN, HEAD_DIM), 0.5),
        'w_cmp2_v': nrm(k[16], (DEPTH, HEAD_DIM, HEAD_DIM), HEAD_DIM ** -0.5),
        'w_pool': nrm(k[17], (DEPTH, len(POOL_WINDOWS), POOL_GROUP, POOL_GROUP), POOL_GROUP ** -0.5),
        'pool_scale': 1.0 + nrm(k[18], (DEPTH, POOL_WIDTH), 0.02),
        'w_o': nrm(k[19], (DEPTH, D_MODEL, D_MODEL), BETA * D_MODEL ** -0.5),
        'ln1_g': 1.0 + nrm(k[20], (DEPTH, D_MODEL), 0.02),
        'ln1_b': nrm(k[21], (DEPTH, D_MODEL), 0.02),
        'w_gate': nrm(k[22], (DEPTH, D_MODEL, D_FF), D_MODEL ** -0.5),
        'w_up': nrm(k[23], (DEPTH, D_MODEL, D_FF), D_MODEL ** -0.5),
        'w_down': nrm(k[24], (DEPTH, D_FF, D_MODEL), BETA * D_FF ** -0.5),
        'ln2_g': 1.0 + nrm(k[25], (DEPTH, D_MODEL), 0.02),
        'ln2_b': nrm(k[26], (DEPTH, D_MODEL), 0.02),
    }


def reference(x_prompt, x_sample, cache_k_cmp, cache_v_cmp, cache_k_slc, cache_v_slc,
              state_k_win, state_v_win, state_pool, page_table, w_in,
              w_cmp1_k, pe_cmp_k, w_cmp2_k, w_cmp1_v, pe_cmp_v, w_cmp2_v,
              w_pool, pool_scale, w_o, ln1_g, ln1_b, w_gate, w_up, w_down, ln2_g, ln2_b):
    B, T, _ = x_prompt.shape
    DB, S, _ = x_sample.shape
    past = page_table.shape[1] * PAGE_SIZE
    wb = state_k_win.shape[2]
    wbp = min(WINDOW, T)
    pos_p = jnp.arange(T, dtype=jnp.int32)
    pos_s = past + jnp.arange(S, dtype=jnp.int32)
    win_pos_s = past - wb + jnp.arange(wb + S, dtype=jnp.int32)
    hp, hs = x_prompt, x_sample
    prompt_states, sample_states = [], []
    for l in range(DEPTH):
        cmp_w = (w_cmp1_k[l], pe_cmp_k[l], w_cmp2_k[l], w_cmp1_v[l], pe_cmp_v[l], w_cmp2_v[l])
        tail = (w_o[l], ln1_g[l], ln1_b[l], w_gate[l], w_up[l], w_down[l], ln2_g[l], ln2_b[l])
        u, q, kc, vc, ks, vs, kw, vw, g = _project(hp, w_in[l], pos_p)
        pool_out = _pool_mix(u, pos_p, w_pool[l], pool_scale[l])
        kcb, vcb, cmp_end, k_sb, v_sb = _nsa_keys(kc, vc, ks, vs, *cmp_w)
        nsa_out = _nsa_prompt(q, g, kcb, vcb, cmp_end, k_sb, v_sb, kw, vw)
        prompt_states.append((kc, vc, ks, vs, kw[:, T - wbp:], vw[:, T - wbp:], u[:, T - (POOL_MAX - 1):]))
        hp = _finish(hp, pool_out, nsa_out, *tail)
        u, q, kc, vc, ks, vs, kw, vw, g = _project(hs, w_in[l], pos_s)

        def paged_rows(cache):
            return cache[l, page_table].reshape(DB, past, KV_HEADS, HEAD_DIM)

        kc_all = jnp.concatenate([paged_rows(cache_k_cmp), kc], 1)
        vc_all = jnp.concatenate([paged_rows(cache_v_cmp), vc], 1)
        ks_all = jnp.concatenate([paged_rows(cache_k_slc), ks], 1)
        vs_all = jnp.concatenate([paged_rows(cache_v_slc), vs], 1)
        kw_all = jnp.concatenate([state_k_win[l], kw], 1)
        vw_all = jnp.concatenate([state_v_win[l], vw], 1)
        u_all = jnp.concatenate([state_pool[l], u], 1)
        pool_out = _pool_mix(u_all, pos_s, w_pool[l], pool_scale[l])
        kcb, vcb, cmp_end, k_sb, v_sb = _nsa_keys(kc_all, vc_all, ks_all, vs_all, *cmp_w)
        nsa_out = _nsa_attend(q, pos_s, kcb, vcb, cmp_end, k_sb, v_sb, kw_all, vw_all,
                              win_pos_s, g).reshape(DB, S, NSA_WIDTH)
        sample_states.append((kc, vc, ks, vs, kw_all[:, S:], vw_all[:, S:], u_all[:, S:]))
        hs = _finish(hs, pool_out, nsa_out, *tail)
    (kc_p, vc_p, ks_p, vs_p, kw_p, vw_p, pool_p) = [jnp.stack(a, 0) for a in zip(*prompt_states)]
    (kc_s, vc_s, ks_s, vs_s, kw_s, vw_s, pool_s) = [jnp.stack(a, 0) for a in zip(*sample_states)]
    return (hp, hs, kc_p, vc_p, ks_p, vs_p, kw_p, vw_p, pool_p,
            kc_s, vc_s, ks_s, vs_s, kw_s, vw_s, pool_s)
```

```python
import functools
import math

import jax
import jax.numpy as jnp
from jax import lax
from jax.experimental import pallas as pl
from jax.experimental.pallas import tpu as pltpu

HEAD_DIM = 128
POOL_WINDOWS = (2, 4, 8, 16)
POOL_MAX = max(POOL_WINDOWS)
N_BRANCH = 3
CMP_LEN = 32
CMP_STRIDE = 16
CMP_RATIO = CMP_LEN // CMP_STRIDE
SLC_LEN = 64
N_SEL = 16
WINDOW = 512
Q_TILE = 128
ROPE_THETA = 10000.0
LN_EPS = 1e-5
LANES = 128
SUBLANES = 8
VMEM_LIMIT = 56 * 1024 * 1024
NEG = -1e30

f32 = jnp.float32
bf16 = jnp.bfloat16


def _cparams(sem):
    return pltpu.CompilerParams(dimension_semantics=sem, vmem_limit_bytes=VMEM_LIMIT)


def _pick_tile(n, target, align):
    best = None
    for t in range(align, min(n, target) + 1, align):
        if n % t == 0:
            best = t
    if best is None:
        raise ValueError(f"no tile for {n} (target {target}, align {align})")
    return best


def _dot(a, b):
    return jnp.dot(a, b, preferred_element_type=f32)


def _dot_nt(a, b):
    return lax.dot_general(a, b, (((1,), (1,)), ((), ())), preferred_element_type=f32)


def _rope_tile(a, cosf, sinf):
    return a * cosf + pltpu.roll(a, HEAD_DIM // 2, axis=1) * sinf


def _div_pow2(x, d):
    assert d & (d - 1) == 0
    return lax.shift_right_arithmetic(x, d.bit_length() - 1)


def _sigmoid(x):
    return 1.0 / (1.0 + jnp.exp(-x))


def _proj_kernel(x_ref, w_ref, cs_ref, o_ref, *, mode, scale):
    acc = _dot(x_ref[...], w_ref[...])
    if mode == "rope":
        cosf = cs_ref[:, :HEAD_DIM]
        sinf = cs_ref[:, HEAD_DIM:]
        for h in range(acc.shape[1] // HEAD_DIM):
            sl = slice(h * HEAD_DIM, (h + 1) * HEAD_DIM)
            o_ref[:, sl] = (_rope_tile(acc[:, sl], cosf, sinf) * scale).astype(o_ref.dtype)
    elif mode == "sigmoid":
        o_ref[...] = _sigmoid(acc).astype(o_ref.dtype)
    else:
        o_ref[...] = acc.astype(o_ref.dtype)


def _proj(x, w, cs, *, mode, scale=1.0, out_dtype=f32, tm, tn):
    M, K = x.shape
    N = w.shape[1]
    return pl.pallas_call(
        functools.partial(_proj_kernel, mode=mode, scale=scale),
        grid=(M // tm, N // tn),
        in_specs=[pl.BlockSpec((tm, K), lambda i, j: (i, 0)),
                  pl.BlockSpec((K, tn), lambda i, j: (0, j)),
                  pl.BlockSpec((tm, 2 * HEAD_DIM), lambda i, j: (i, 0))],
        out_specs=pl.BlockSpec((tm, tn), lambda i, j: (i, j)),
        out_shape=jax.ShapeDtypeStruct((M, N), out_dtype),
        compiler_params=_cparams(("parallel", "arbitrary")),
        name=f"proj_{mode}",
    )(x, w, cs)


def _gateup_kernel(x_ref, wg_ref, wu_ref, o_ref):
    x = x_ref[...]
    g = _dot(x, wg_ref[...])
    u = _dot(x, wu_ref[...])
    o_ref[...] = (g * _sigmoid(g) * u).astype(o_ref.dtype)


def _gateup(x, wg, wu, *, tm, tf):
    M, K = x.shape
    F = wg.shape[1]
    return pl.pallas_call(
        _gateup_kernel,
        grid=(M // tm, F // tf),
        in_specs=[pl.BlockSpec((tm, K), lambda i, j: (i, 0)),
                  pl.BlockSpec((K, tf), lambda i, j: (0, j)),
                  pl.BlockSpec((K, tf), lambda i, j: (0, j))],
        out_specs=pl.BlockSpec((tm, tf), lambda i, j: (i, j)),
        out_shape=jax.ShapeDtypeStruct((M, F), bf16),
        compiler_params=_cparams(("parallel", "arbitrary")),
        name="ffn_gate_up",
    )(x, wg, wu)


def _resid_ln_kernel(a_ref, w_ref, r_ref, g_ref, b_ref, o_ref, acc_ref, *, alpha):
    k = pl.program_id(1)

    @pl.when(k == 0)
    def _():
        acc_ref[...] = jnp.zeros_like(acc_ref)

    acc_ref[...] += _dot(a_ref[...], w_ref[...])

    @pl.when(k == pl.num_programs(1) - 1)
    def _():
        y = alpha * r_ref[...] + acc_ref[...]
        mu = jnp.mean(y, axis=-1, keepdims=True)
        d = y - mu
        var = jnp.mean(d * d, axis=-1, keepdims=True)
        o_ref[...] = d * lax.rsqrt(var + LN_EPS) * g_ref[...] + b_ref[...]


def _resid_ln(a, w, resid, gamma, beta, *, alpha, tm, tk):
    M, K = a.shape
    N = w.shape[1]
    return pl.pallas_call(
        functools.partial(_resid_ln_kernel, alpha=alpha),
        grid=(M // tm, K // tk),
        in_specs=[pl.BlockSpec((tm, tk), lambda i, k: (i, k)),
                  pl.BlockSpec((tk, N), lambda i, k: (k, 0)),
                  pl.BlockSpec((tm, N), lambda i, k: (i, 0)),
                  pl.BlockSpec((1, N), lambda i, k: (0, 0)),
                  pl.BlockSpec((1, N), lambda i, k: (0, 0))],
        out_specs=pl.BlockSpec((tm, N), lambda i, k: (i, 0)),
        out_shape=jax.ShapeDtypeStruct((M, N), f32),
        scratch_shapes=[pltpu.VMEM((tm, N), f32)],
        compiler_params=_cparams(("parallel", "arbitrary")),
        name="matmul_resid_layernorm",
    )(a, w, resid, gamma, beta)


def _pool_kernel(hist_ref, u_ref, wp_ref, sc_ref, o_ref, buf, *, tm, pos0):
    i = pl.program_id(1)
    H = POOL_MAX

    @pl.when(i == 0)
    def _():
        buf[0:H, :] = hist_ref[...]

    @pl.when(i > 0)
    def _():
        buf[0:H, :] = buf[tm:tm + H, :]

    buf[H:H + tm, :] = u_ref[...]
    pg = wp_ref.shape[1]
    pos = pos0 + i * tm + lax.broadcasted_iota(jnp.int32, (tm, 1), 0)
    for g, w in enumerate(POOL_WINDOWS):
        cols = slice(g * pg, (g + 1) * pg)
        cur = buf[H:H + tm, cols]
        tot = cur
        for k in range(1, w):
            tot = tot + buf[H - k:H - k + tm, cols]
        cnt = jnp.minimum(pos + 1, w).astype(f32)
        d = tot / cnt - cur
        y = _dot(d.astype(bf16), wp_ref[g]) * sc_ref[:, cols]
        o_ref[:, cols] = y.astype(o_ref.dtype)


def _pool_mix(hist, u, w_pool, scale, *, pos0, tm):
    nb, n, W = u.shape
    return pl.pallas_call(
        functools.partial(_pool_kernel, tm=tm, pos0=pos0),
        grid=(nb, n // tm),
        in_specs=[pl.BlockSpec((None, POOL_MAX, W), lambda b, i: (b, 0, 0)),
                  pl.BlockSpec((None, tm, W), lambda b, i: (b, i, 0)),
                  pl.BlockSpec(w_pool.shape, lambda b, i: (0, 0, 0)),
                  pl.BlockSpec((1, W), lambda b, i: (0, 0))],
        out_specs=pl.BlockSpec((None, tm, W), lambda b, i: (b, i, 0)),
        out_shape=jax.ShapeDtypeStruct((nb, n, W), f32),
        scratch_shapes=[pltpu.VMEM((POOL_MAX + tm, W), f32)],
        compiler_params=_cparams(("parallel", "arbitrary")),
        name="pool_mix",
    )(hist, u, w_pool, scale)


def _cmp_part_kernel(pt_ref, *refs, n_pg, G, cpp):
    pages = refs[:n_pg]
    w_ref, o_ref, x_scr = refs[n_pg], refs[n_pg + 1], refs[n_pg + 2]
    rows = n_pg * cpp
    for pg in range(n_pg):
        for p in range(CMP_STRIDE):
            for g in range(G):
                src = (p * G + g) * HEAD_DIM
                x_scr[g * rows + pg * cpp:g * rows + (pg + 1) * cpp, p * HEAD_DIM:(p + 1) * HEAD_DIM] = (
                    pages[pg][:, src:src + HEAD_DIM])
    part = _dot(x_scr[...].astype(bf16), w_ref[...])
    for g in range(G):
        o_ref[g] = part[g * rows:(g + 1) * rows]


def _cmp_part(pages, page_table, w1cat, *, G, n_pg):
    nb, n_pages = page_table.shape
    cpp = pages.shape[1]
    width = pages.shape[2]
    n_chunk = n_pages * cpp
    rows = n_pg * cpp

    def page_spec(pg):
        return pl.BlockSpec((None, cpp, width), lambda b, j, pt: (pt[b, j * n_pg + pg], 0, 0))

    grid_spec = pltpu.PrefetchScalarGridSpec(
        num_scalar_prefetch=1,
        grid=(nb, n_pages // n_pg),
        in_specs=[page_spec(pg) for pg in range(n_pg)]
        + [pl.BlockSpec(w1cat.shape, lambda b, j, pt: (0, 0))],
        out_specs=pl.BlockSpec((None, G, rows, w1cat.shape[1]), lambda b, j, pt: (b, 0, j, 0)),
        scratch_shapes=[pltpu.VMEM((G * rows, CMP_STRIDE * HEAD_DIM), f32)],
    )
    return pl.pallas_call(
        functools.partial(_cmp_part_kernel, n_pg=n_pg, G=G, cpp=cpp),
        grid_spec=grid_spec,
        out_shape=jax.ShapeDtypeStruct((nb, G, n_chunk, w1cat.shape[1]), f32),
        compiler_params=_cparams(("parallel", "arbitrary")),
        name="cmp_part",
    )(page_table, *([pages] * n_pg), w1cat)


def _gelu_tanh(x):
    return 0.5 * x * (1.0 + jnp.tanh(math.sqrt(2.0 / math.pi) * (x + 0.044715 * (x * x * x))))


def _cmp_finish_kernel(part_ref, pe_ref, w1_ref, w2_ref, cs_ref, o_ref, *, rope):
    n_chunk = part_ref.shape[0]
    hid0 = _dot(pe_ref[...].astype(bf16), w1_ref[...])[0:1]
    a = part_ref[:, :HEAD_DIM]
    b = pltpu.roll(part_ref[:, HEAD_DIM:], n_chunk - 1, axis=0)
    y = _dot(_gelu_tanh(hid0 + a + b).astype(bf16), w2_ref[...])
    if rope:
        y = _rope_tile(y, cs_ref[:, :HEAD_DIM], cs_ref[:, HEAD_DIM:])
    o_ref[...] = y.astype(o_ref.dtype)


def _cmp_finish(part, pe, w1flat, w2, cs_cmp, *, rope):
    nb, G, n_chunk, _ = part.shape
    return pl.pallas_call(
        functools.partial(_cmp_finish_kernel, rope=rope),
        grid=(nb, G),
        in_specs=[pl.BlockSpec((None, None, n_chunk, CMP_RATIO * HEAD_DIM), lambda b, g: (b, g, 0, 0)),
                  pl.BlockSpec(pe.shape, lambda b, g: (0, 0)),
                  pl.BlockSpec(w1flat.shape, lambda b, g: (0, 0)),
                  pl.BlockSpec(w2.shape, lambda b, g: (0, 0)),
                  pl.BlockSpec(cs_cmp.shape, lambda b, g: (0, 0))],
        out_specs=pl.BlockSpec((None, None, n_chunk, HEAD_DIM), lambda b, g: (b, g, 0, 0)),
        out_shape=jax.ShapeDtypeStruct((nb, G, n_chunk, HEAD_DIM), bf16),
        compiler_params=_cparams(("parallel", "parallel")),
        name="cmp_finish",
    )(part, pe, w1flat, w2, cs_cmp)


def _masked_softmax(s, mask):
    sm = jnp.where(mask, s, NEG)
    m = jnp.max(sm, axis=-1, keepdims=True)
    e = jnp.where(mask, jnp.exp(sm - m), 0.0)
    return e / jnp.maximum(jnp.sum(e, axis=-1, keepdims=True), 1e-30)


def _dot_split3(p, w01):
    hi = p.astype(bf16)
    r1 = p - hi.astype(f32)
    mid = r1.astype(bf16)
    lo = (r1 - mid.astype(f32)).astype(bf16)
    return _dot(hi, w01) + _dot(mid, w01) + _dot(lo, w01)


def _overlap01(n_chunk, n_cmp, ns):
    n = lax.broadcasted_iota(jnp.int32, (n_chunk, ns), 0)
    s = lax.broadcasted_iota(jnp.int32, (n_chunk, ns), 1)
    ov = ((n * CMP_STRIDE < s * SLC_LEN + SLC_LEN) & (n * CMP_STRIDE + CMP_LEN - 1 >= s * SLC_LEN)
          & (n < n_cmp))
    return jnp.where(ov, 1.0, 0.0).astype(bf16)


def _select_blocks(imp, q_pos, n_slc):
    ns = imp.shape[1]
    s_io = lax.broadcasted_iota(jnp.int32, imp.shape, 1)
    cur = _div_pow2(q_pos, SLC_LEN)
    forced = (s_io == 0) | (s_io == cur) | (s_io == cur - 1)
    val = jnp.where(forced, jnp.inf, jnp.where(s_io <= cur, imp, -jnp.inf))
    cnt = jnp.zeros(imp.shape, f32)
    for sp in range(n_slc):
        col = val[:, sp:sp + 1]
        ahead = (col > val) | ((col == val) & (s_io > sp))
        cnt = cnt + jnp.where(ahead, 1.0, 0.0)
    return jnp.where((cnt < N_SEL) & (val > -jnp.inf), 1.0, 0.0)


def _expand01(ns, n_keys, key0):
    s = lax.broadcasted_iota(jnp.int32, (ns, n_keys), 0)
    key = key0 + lax.broadcasted_iota(jnp.int32, (ns, n_keys), 1)
    return jnp.where(_div_pow2(key, SLC_LEN) == s, 1.0, 0.0).astype(bf16)


def _online_update(s, mask, v, m_old, l_old, acc_old):
    sm = jnp.where(mask, s, NEG)
    m_new = jnp.maximum(m_old, jnp.max(sm, axis=-1, keepdims=True))
    alpha = jnp.exp(m_old - m_new)
    e = jnp.where(mask, jnp.exp(sm - m_new), 0.0)
    l_new = alpha * l_old + jnp.sum(e, axis=-1, keepdims=True)
    acc_new = alpha * acc_old + _dot(e.astype(bf16), v)
    return m_new, l_new, acc_new


def _prompt_attn_kernel(q_ref, gt_ref, kcb_ref, vcb_ref, ks_ref, vs_ref, kw_ref, vw_ref, o_ref,
                        ksb, vsb, kwb, vwb, m_scr, l_scr, acc_scr, *, R, n_cmp, n_slc, kc):
    c = pl.program_id(2)
    tq = Q_TILE
    n_chunk = kcb_ref.shape[0]
    ns = LANES * pl.cdiv(n_slc, LANES)
    win_keys = WINDOW + tq

    @pl.when(c == 0)
    def _():
        ksb[...] = ks_ref[...].astype(bf16)
        vsb[...] = vs_ref[...].astype(bf16)
        kwb[...] = kw_ref[...].astype(bf16)
        vwb[...] = vw_ref[...].astype(bf16)

    q_all = jnp.concatenate([q_ref[:, r * HEAD_DIM:(r + 1) * HEAD_DIM] for r in range(R)], axis=0)
    q_pos = c * tq + lax.broadcasted_iota(jnp.int32, (tq, 1), 0)

    s_c = _dot_nt(q_all, kcb_ref[...])
    n_io = lax.broadcasted_iota(jnp.int32, (tq, n_chunk), 1)
    mask_c = (n_io * CMP_STRIDE + CMP_LEN - 1 <= q_pos) & (n_io < n_cmp)
    vcb = vcb_ref[...]
    o_c = []
    p_sum = jnp.zeros((tq, n_chunk), f32)
    for r in range(R):
        p = _masked_softmax(s_c[r * tq:(r + 1) * tq], mask_c)
        p_sum = p_sum + p
        o_c.append(_dot(p.astype(bf16), vcb))

    imp = _dot_split3(p_sum, _overlap01(n_chunk, n_cmp, ns))
    sel = _select_blocks(imp, q_pos, n_slc).astype(bf16)

    m_scr[...] = jnp.full(m_scr.shape, NEG, f32)
    l_scr[...] = jnp.zeros(l_scr.shape, f32)
    acc_scr[...] = jnp.zeros(acc_scr.shape, f32)
    n_it = (c * tq + tq + kc - 1) // kc

    def sweep(j, carry):
        k0 = pl.multiple_of(j * kc, kc)
        s = _dot_nt(q_all, ksb[pl.ds(k0, kc), :])
        v = vsb[pl.ds(k0, kc), :]
        selx = _dot(sel, _expand01(ns, kc, k0))
        key = k0 + lax.broadcasted_iota(jnp.int32, (tq, kc), 1)
        mask = (selx > 0.5) & (key <= q_pos)
        for r in range(R):
            rows = slice(r * tq, (r + 1) * tq)
            m, l, acc = _online_update(s[rows], mask, v, m_scr[rows], l_scr[rows], acc_scr[rows])
            m_scr[rows] = m
            l_scr[rows] = l
            acc_scr[rows] = acc
        return carry

    lax.fori_loop(0, n_it, sweep, 0)

    w0 = pl.multiple_of(jnp.maximum(c * tq - WINDOW, 0), tq)
    s_w = _dot_nt(q_all, kwb[pl.ds(w0, win_keys), :])
    v_w = vwb[pl.ds(w0, win_keys), :]
    key_w = w0 + lax.broadcasted_iota(jnp.int32, (tq, win_keys), 1)
    mask_w = (key_w <= q_pos) & (q_pos - key_w < WINDOW)

    for r in range(R):
        rows = slice(r * tq, (r + 1) * tq)
        o_s = acc_scr[rows] / jnp.maximum(l_scr[rows], 1e-30)
        o_w = _dot(_masked_softmax(s_w[rows], mask_w).astype(bf16), v_w)
        g = [gt_ref[:, r * N_BRANCH + br:r * N_BRANCH + br + 1] for br in range(N_BRANCH)]
        o = g[0] * o_c[r] + g[1] * o_s + g[2] * o_w
        o_ref[:, r * HEAD_DIM:(r + 1) * HEAD_DIM] = o.astype(o_ref.dtype)


def _prompt_attn(q, gates, kcb, vcb, ks, vs, kw, vw, *, G, R):
    B, T, _ = q.shape
    n_chunk = kcb.shape[2]
    n_cmp = n_chunk - CMP_RATIO + 1
    n_slc = pl.cdiv(T, SLC_LEN)
    kc = _pick_tile(T, 512, Q_TILE)
    assert T % Q_TILE == 0 and T >= WINDOW + Q_TILE
    seq_spec = pl.BlockSpec((None, T, HEAD_DIM), lambda b, g, c: (b, 0, g))
    cmp_spec = pl.BlockSpec((None, None, n_chunk, HEAD_DIM), lambda b, g, c: (b, g, 0, 0))
    return pl.pallas_call(
        functools.partial(_prompt_attn_kernel, R=R, n_cmp=n_cmp, n_slc=n_slc, kc=kc),
        grid=(B, G, T // Q_TILE),
        in_specs=[pl.BlockSpec((None, Q_TILE, R * HEAD_DIM), lambda b, g, c: (b, c, g)),
                  pl.BlockSpec((None, Q_TILE, LANES), lambda b, g, c: (b, c, g)),
                  cmp_spec, cmp_spec, seq_spec, seq_spec, seq_spec, seq_spec],
        out_specs=pl.BlockSpec((None, Q_TILE, R * HEAD_DIM), lambda b, g, c: (b, c, g)),
        out_shape=jax.ShapeDtypeStruct(q.shape, bf16),
        scratch_shapes=[pltpu.VMEM((T, HEAD_DIM), bf16)] * 4
        + [pltpu.VMEM((R * Q_TILE, 1), f32), pltpu.VMEM((R * Q_TILE, 1), f32),
           pltpu.VMEM((R * Q_TILE, HEAD_DIM), f32)],
        compiler_params=_cparams(("parallel", "parallel", "arbitrary")),
        name="nsa_prompt",
    )(q, gates, kcb, vcb, ks, vs, kw, vw)


def _sample_cmp_win_kernel(q_ref, gt_ref, kcb_ref, vcb_ref, kwin_ref, vwin_ref, kwn_ref, vwn_ref,
                           part_ref, sel_ref, kbuf, vbuf, *, R, n_cmp, n_slc, past):
    n_chunk = kcb_ref.shape[0]
    ns = sel_ref.shape[1]
    wb = kwin_ref.shape[0]
    sp = SUBLANES
    rows = R * sp
    q = q_ref[...]
    step = lax.broadcasted_iota(jnp.int32, (rows, 1), 0) & (sp - 1)
    q_pos = past + step

    s_c = _dot_nt(q, kcb_ref[...])
    n_io = lax.broadcasted_iota(jnp.int32, (rows, n_chunk), 1)
    mask_c = (n_io * CMP_STRIDE + CMP_LEN - 1 <= q_pos) & (n_io < n_cmp)
    p_c = _masked_softmax(s_c, mask_c)
    o_c = _dot(p_c.astype(bf16), vcb_ref[...])
    p_sum = p_c[0:sp]
    for r in range(1, R):
        p_sum = p_sum + p_c[r * sp:(r + 1) * sp]
    imp = _dot_split3(p_sum, _overlap01(n_chunk, n_cmp, ns))
    sel_ref[...] = _select_blocks(imp, q_pos[0:sp], n_slc)

    nbuf = kbuf.shape[0]
    kbuf[0:wb, :] = kwin_ref[...].astype(bf16)
    kbuf[wb:nbuf, :] = kwn_ref[...].astype(bf16)
    vbuf[0:wb, :] = vwin_ref[...].astype(bf16)
    vbuf[wb:nbuf, :] = vwn_ref[...].astype(bf16)
    s_w = _dot_nt(q, kbuf[...])
    key_w = past - wb + lax.broadcasted_iota(jnp.int32, (rows, nbuf), 1)
    mask_w = (key_w <= q_pos) & (q_pos - key_w < WINDOW) & (key_w >= 0)
    o_w = _dot(_masked_softmax(s_w, mask_w).astype(bf16), vbuf[...])
    part_ref[...] = gt_ref[:, 0:1] * o_c + gt_ref[:, 2:3] * o_w


def _sample_cmp_win(q, gates, kcb, vcb, kwin, vwin, kw_new, vw_new, *, past, n_slc):
    DB, G, rows, _ = q.shape
    R = rows // SUBLANES
    n_chunk = kcb.shape[2]
    n_cmp = n_chunk - CMP_RATIO + 1
    wb = kwin.shape[1]
    ns = LANES * pl.cdiv(n_slc, LANES)
    npad = kw_new.shape[1]
    qspec = pl.BlockSpec((None, None, rows, HEAD_DIM), lambda b, g: (b, g, 0, 0))
    cspec = pl.BlockSpec((None, None, n_chunk, HEAD_DIM), lambda b, g: (b, g, 0, 0))
    wspec = pl.BlockSpec((None, wb, HEAD_DIM), lambda b, g: (b, 0, g))
    nspec = pl.BlockSpec((None, npad, HEAD_DIM), lambda b, g: (b, 0, g))
    return pl.pallas_call(
        functools.partial(_sample_cmp_win_kernel, R=R, n_cmp=n_cmp, n_slc=n_slc, past=past),
        grid=(DB, G),
        in_specs=[qspec, qspec, cspec, cspec, wspec, wspec, nspec, nspec],
        out_specs=[qspec, pl.BlockSpec((None, None, SUBLANES, ns), lambda b, g: (b, g, 0, 0))],
        out_shape=[jax.ShapeDtypeStruct((DB, G, rows, HEAD_DIM), f32),
                   jax.ShapeDtypeStruct((DB, G, SUBLANES, ns), f32)],
        scratch_shapes=[pltpu.VMEM((wb + npad, HEAD_DIM), bf16)] * 2,
        compiler_params=_cparams(("parallel", "parallel")),
        name="nsa_sample_cmp_win",
    )(q, gates, kcb, vcb, kwin, vwin, kw_new, vw_new)


def _sample_sel_kernel(pt_ref, *refs, n_pg, G, R, past, page):
    kpages = refs[:n_pg]
    vpages = refs[n_pg:2 * n_pg]
    (q_ref, gt_ref, sel_ref, part_ref, ksn_ref, vsn_ref, o_ref, m_scr, l_scr, acc_scr) = refs[2 * n_pg:]
    j = pl.program_id(1)
    nj = pl.num_programs(1)
    sp = SUBLANES
    rows = R * sp
    ns = sel_ref.shape[2]
    step = lax.broadcasted_iota(jnp.int32, (rows, 1), 0) & (sp - 1)
    q_pos = past + step

    @pl.when(j == 0)
    def _():
        m_scr[...] = jnp.full(m_scr.shape, NEG, f32)
        l_scr[...] = jnp.zeros(l_scr.shape, f32)
        acc_scr[...] = jnp.zeros(acc_scr.shape, f32)

    def attend(g, sel_g, k_rows, v_rows, key0):
        n = k_rows.shape[0]
        cols = slice(g * HEAD_DIM, (g + 1) * HEAD_DIM)
        s = _dot_nt(q_ref[g], k_rows[:, cols].astype(bf16))
        selx = _dot(sel_g, _expand01(ns, n, key0))
        key = key0 + lax.broadcasted_iota(jnp.int32, (rows, n), 1)
        mask = (selx > 0.5) & (key <= q_pos)
        m, l, acc = _online_update(s, mask, v_rows[:, cols].astype(bf16), m_scr[g], l_scr[g], acc_scr[g])
        m_scr[g] = m
        l_scr[g] = l
        acc_scr[g] = acc

    sels = [jnp.concatenate([sel_ref[g]] * R, axis=0).astype(bf16) for g in range(G)]
    for pg in range(n_pg):
        key0 = (j * n_pg + pg) * page
        for g in range(G):
            attend(g, sels[g], kpages[pg], vpages[pg], key0)

    @pl.when(j == nj - 1)
    def _():
        for g in range(G):
            attend(g, sels[g], ksn_ref, vsn_ref, past)
            o_s = acc_scr[g] / jnp.maximum(l_scr[g], 1e-30)
            o_ref[g] = part_ref[g] + gt_ref[g][:, 1:2] * o_s


def _sample_sel(kcache, vcache, page_table, q, gates, sel, part, ks_new, vs_new, *, past, n_pg):
    DB, G, rows, _ = q.shape
    R = rows // SUBLANES
    n_pages = page_table.shape[1]
    page = kcache.shape[1]
    width = kcache.shape[2]
    ns = sel.shape[3]
    npad = ks_new.shape[1]

    def page_spec(pg):
        return pl.BlockSpec((None, page, width), lambda b, j, pt: (pt[b, j * n_pg + pg], 0, 0))

    qspec = pl.BlockSpec((None, G, rows, HEAD_DIM), lambda b, j, pt: (b, 0, 0, 0))
    nspec = pl.BlockSpec((None, npad, width), lambda b, j, pt: (b, 0, 0))
    grid_spec = pltpu.PrefetchScalarGridSpec(
        num_scalar_prefetch=1,
        grid=(DB, n_pages // n_pg),
        in_specs=[page_spec(pg) for pg in range(n_pg)] * 2
        + [qspec, qspec, pl.BlockSpec((None, G, SUBLANES, ns), lambda b, j, pt: (b, 0, 0, 0)), qspec,
           nspec, nspec],
        out_specs=qspec,
        scratch_shapes=[pltpu.VMEM((G, rows, 1), f32), pltpu.VMEM((G, rows, 1), f32),
                        pltpu.VMEM((G, rows, HEAD_DIM), f32)],
    )
    return pl.pallas_call(
        functools.partial(_sample_sel_kernel, n_pg=n_pg, G=G, R=R, past=past, page=page),
        grid_spec=grid_spec,
        out_shape=jax.ShapeDtypeStruct((DB, G, rows, HEAD_DIM), f32),
        compiler_params=_cparams(("parallel", "arbitrary")),
        name="nsa_sample_sel",
    )(page_table, *([kcache] * n_pg), *([vcache] * n_pg), q, gates, sel, part, ks_new, vs_new)


def _rope_table(pos):
    half = HEAD_DIM // 2
    inv = ROPE_THETA ** (-jnp.arange(half, dtype=f32) / half)
    ang = pos.astype(f32)[:, None] * inv[None, :]
    cos, sin = jnp.cos(ang), jnp.sin(ang)
    return jnp.concatenate([cos, cos, -sin, sin], axis=-1)


def kernel(x_prompt, x_sample, cache_k_cmp, cache_v_cmp, cache_k_slc, cache_v_slc, state_k_win, state_v_win, state_pool, page_table, w_in, w_cmp1_k, pe_cmp_k, w_cmp2_k, w_cmp1_v, pe_cmp_v, w_cmp2_v, w_pool, pool_scale, w_o, ln1_g, ln1_b, w_gate, w_up, w_down, ln2_g, ln2_b):
    B, T, D = x_prompt.shape
    DB, S, _ = x_sample.shape
    depth, n_phys, page, G, _ = cache_k_cmp.shape
    assert depth == 1 and S <= SUBLANES
    n_pages = page_table.shape[1]
    past = n_pages * page
    PW = state_pool.shape[3]
    n_groups = len(POOL_WINDOWS)
    NW = D - PW
    NH = NW // HEAD_DIM
    R = NH // G
    KVW = G * HEAD_DIM
    F = w_gate.shape[2]
    wb = state_k_win.shape[2]
    wbp = min(WINDOW, T)
    alpha = (2 * depth) ** 0.25
    MP = B * T
    M = MP + DB * S
    cpp = page // CMP_STRIDE
    assert T % page == 0 and page % CMP_STRIDE == 0 and page % SLC_LEN == 0 and past % SLC_LEN == 0

    x_all = jnp.concatenate([x_prompt.reshape(MP, D), x_sample.reshape(DB * S, D)], axis=0)
    x_bf = x_all.astype(bf16)
    w_in0 = w_in[0]
    o_q = PW
    o_kv = PW + NW
    o_g = o_kv + 6 * KVW
    w_u = w_in0[:, :PW].astype(bf16)
    w_q = w_in0[:, o_q:o_kv].astype(bf16)
    w_kv = [w_in0[:, o_kv + i * KVW:o_kv + (i + 1) * KVW].astype(bf16) for i in range(6)]
    w_g = w_in0[:, o_g:o_g + N_BRANCH * NH].reshape(D, G, R * N_BRANCH)
    w_g = jnp.pad(w_g, ((0, 0), (0, 0), (0, LANES - R * N_BRANCH))).reshape(D, G * LANES).astype(bf16)

    pos_all = jnp.concatenate([jnp.tile(jnp.arange(T, dtype=jnp.int32), B),
                               jnp.tile(past + jnp.arange(S, dtype=jnp.int32), DB)])
    cs_all = _rope_table(pos_all)

    tm = _pick_tile(M, 640, SUBLANES)
    proj = functools.partial(_proj, x_bf, cs=cs_all, tm=tm)
    u = proj(w_u, mode="none", tn=_pick_tile(PW, 512, LANES))
    q = proj(w_q, mode="rope", scale=HEAD_DIM ** -0.5, out_dtype=bf16, tn=_pick_tile(NW, 512, LANES))
    tkv = _pick_tile(KVW, 512, LANES)
    kc = proj(w_kv[0], mode="none", tn=tkv)
    vc = proj(w_kv[1], mode="none", tn=tkv)
    ks = proj(w_kv[2], mode="rope", tn=tkv)
    vs = proj(w_kv[3], mode="none", tn=tkv)
    kw = proj(w_kv[4], mode="rope", tn=tkv)
    vw = proj(w_kv[5], mode="none", tn=tkv)
    gates = proj(w_g, mode="sigmoid", tn=_pick_tile(G * LANES, 512, LANES))

    wp_bf = w_pool[0].astype(bf16)
    pscale = pool_scale[0].reshape(1, PW)
    u_p = u[:MP].reshape(B, T, PW)
    u_s = u[MP:].reshape(DB, S, PW)
    pool_p = _pool_mix(jnp.zeros((B, POOL_MAX, PW), f32), u_p, wp_bf, pscale, pos0=0,
                       tm=_pick_tile(T, 512, SUBLANES))
    hist_s = jnp.pad(state_pool[0], ((0, 0), (POOL_MAX - state_pool.shape[2], 0), (0, 0)))
    u_s_pad = jnp.pad(u_s, ((0, 0), (0, SUBLANES - S), (0, 0)))
    pool_s = _pool_mix(hist_s, u_s_pad, wp_bf, pscale, pos0=past, tm=SUBLANES)[:, :S]

    def cmp_weights(w1, pe, w2):
        w1cat = w1[0].reshape(CMP_RATIO, CMP_STRIDE, HEAD_DIM, HEAD_DIM).transpose(1, 2, 0, 3)
        w1cat = w1cat.reshape(CMP_STRIDE * HEAD_DIM, CMP_RATIO * HEAD_DIM).astype(bf16)
        pe8 = jnp.broadcast_to(pe[0].reshape(1, CMP_LEN * HEAD_DIM), (SUBLANES, CMP_LEN * HEAD_DIM))
        return w1cat, pe8, w1[0].reshape(CMP_LEN * HEAD_DIM, HEAD_DIM).astype(bf16), w2[0].astype(bf16)

    cw_k = cmp_weights(w_cmp1_k, pe_cmp_k, w_cmp2_k)
    cw_v = cmp_weights(w_cmp1_v, pe_cmp_v, w_cmp2_v)
    chunk_w = CMP_STRIDE * KVW

    def compress(pages, table, cw, rope):
        n_chunk = table.shape[1] * cpp
        cs_cmp = _rope_table(jnp.arange(n_chunk, dtype=jnp.int32) * CMP_STRIDE + CMP_LEN - 1)
        part = _cmp_part(pages, table, cw[0], G=G, n_pg=_pick_tile(table.shape[1], 8, 1))
        return _cmp_finish(part, cw[1], cw[2], cw[3], cs_cmp, rope=rope)

    table_p = jnp.arange(MP // page, dtype=jnp.int32).reshape(B, T // page)
    kcb_p = compress(kc[:MP].reshape(MP // page, cpp, chunk_w), table_p, cw_k, True)
    vcb_p = compress(vc[:MP].reshape(MP // page, cpp, chunk_w), table_p, cw_v, False)
    kcb_s = compress(cache_k_cmp.reshape(n_phys, cpp, chunk_w), page_table, cw_k, True)
    vcb_s = compress(cache_v_cmp.reshape(n_phys, cpp, chunk_w), page_table, cw_v, False)

    def prow(a):
        return a[:MP].reshape(B, T, a.shape[1])

    nsa_p = _prompt_attn(prow(q), prow(gates), kcb_p, vcb_p, prow(ks), prow(vs), prow(kw), prow(vw),
                         G=G, R=R)

    def srow(a):
        return a[MP:].reshape(DB, S, a.shape[1])

    def to_rows(a):
        a = jnp.pad(a.transpose(0, 2, 3, 1, 4), ((0, 0), (0, 0), (0, 0), (0, SUBLANES - S), (0, 0)))
        return a.reshape(DB, G, R * SUBLANES, a.shape[-1])

    q_s = to_rows(srow(q).reshape(DB, S, G, R, HEAD_DIM))
    g_s = srow(gates).reshape(DB, S, G, LANES)[..., :R * N_BRANCH].reshape(DB, S, G, R, N_BRANCH)
    g_s = to_rows(jnp.pad(g_s, ((0, 0),) * 4 + ((0, LANES - N_BRANCH),)))

    def new_rows(a):
        return jnp.pad(srow(a), ((0, 0), (0, LANES - S), (0, 0)))

    n_slc_s = pl.cdiv(past + S, SLC_LEN)
    part_s, sel_s = _sample_cmp_win(q_s, g_s, kcb_s, vcb_s,
                                    state_k_win[0].reshape(DB, wb, KVW), state_v_win[0].reshape(DB, wb, KVW),
                                    new_rows(kw), new_rows(vw), past=past, n_slc=n_slc_s)
    o_s = _sample_sel(cache_k_slc.reshape(n_phys, page, KVW), cache_v_slc.reshape(n_phys, page, KVW),
                      page_table, q_s, g_s, sel_s, part_s, new_rows(ks), new_rows(vs),
                      past=past, n_pg=_pick_tile(n_pages, 4, 1))
    nsa_s = o_s.reshape(DB, G, R, SUBLANES, HEAD_DIM)[:, :, :, :S].transpose(0, 3, 1, 2, 4)
    nsa_s = nsa_s.reshape(DB * S, NW).astype(bf16)

    mixed = jnp.concatenate([
        jnp.concatenate([pool_p.reshape(MP, PW), pool_s.reshape(DB * S, PW)], axis=0).astype(bf16),
        jnp.concatenate([nsa_p.reshape(MP, NW), nsa_s], axis=0)], axis=1)
    tm_ln = _pick_tile(M, 320, SUBLANES)
    h = _resid_ln(mixed, w_o[0].astype(bf16), x_all, ln1_g[0].reshape(1, D), ln1_b[0].reshape(1, D),
                  alpha=alpha, tm=tm_ln, tk=_pick_tile(D, 512, LANES))
    tf = 512
    F_pad = tf * pl.cdiv(F, tf)
    wg_bf = jnp.pad(w_gate[0], ((0, 0), (0, F_pad - F))).astype(bf16)
    wu_bf = jnp.pad(w_up[0], ((0, 0), (0, F_pad - F))).astype(bf16)
    wd_bf = jnp.pad(w_down[0], ((0, F_pad - F), (0, 0))).astype(bf16)
    ff = _gateup(h.astype(bf16), wg_bf, wu_bf, tm=tm, tf=tf)
    y = _resid_ln(ff, wd_bf, h, ln2_g[0].reshape(1, D), ln2_b[0].reshape(1, D),
                  alpha=alpha, tm=tm_ln, tk=tf)

    def pstate(a):
        return a[:MP].reshape(1, B, T, G, HEAD_DIM)

    def sstate(a):
        return a[MP:].reshape(1, DB, S, G, HEAD_DIM)

    kw_p, vw_p = pstate(kw)[:, :, T - wbp:], pstate(vw)[:, :, T - wbp:]
    pool_state_p = u_p[None, :, T - (POOL_MAX - 1):]
    kw_s = jnp.concatenate([state_k_win, sstate(kw)], axis=2)[:, :, S:]
    vw_s = jnp.concatenate([state_v_win, sstate(vw)], axis=2)[:, :, S:]
    pool_state_s = jnp.concatenate([state_pool, u_s[None]], axis=2)[:, :, S:]
    return (y[:MP].reshape(B, T, D), y[MP:].reshape(DB, S, D),
            pstate(kc), pstate(vc), pstate(ks), pstate(vs), kw_p, vw_p, pool_state_p,
            sstate(kc), sstate(vc), sstate(ks), sstate(vs), kw_s, vw_s, pool_state_s)
```

```python
import functools
import math

import jax
import jax.numpy as jnp
from jax import lax
from jax.experimental import pallas as pl
from jax.experimental.pallas import tpu as pltpu

HEAD_DIM = 128
POOL_WINDOWS = (2, 4, 8, 16)
POOL_MAX = max(POOL_WINDOWS)
N_BRANCH = 3
N_KV_SEG = 6
ROPE_KV_SEGS = (2, 4)
CMP_LEN = 32
CMP_STRIDE = 16
CMP_RATIO = CMP_LEN // CMP_STRIDE
SLC_LEN = 64
N_SEL = 16
WINDOW = 512
Q_TILE = 128
ROPE_THETA = 10000.0
LN_EPS = 1e-5
LANES = 128
SUBLANES = 8
VMEM_LIMIT = 56 * 1024 * 1024
NEG = -1e30

f32 = jnp.float32
bf16 = jnp.bfloat16
i32 = jnp.int32


def _cparams(sem):
    return pltpu.CompilerParams(dimension_semantics=sem, vmem_limit_bytes=VMEM_LIMIT)


def _pick_tile(n, target, align):
    best = None
    for t in range(align, min(n, target) + 1, align):
        if n % t == 0:
            best = t
    if best is None:
        raise ValueError(f"no tile for {n} (target {target}, align {align})")
    return best


def _round_up(n, m):
    return m * pl.cdiv(n, m)


def _dot(a, b):
    return jnp.dot(a, b, preferred_element_type=f32)


def _dot_nt(a, b):
    return lax.dot_general(a, b, (((1,), (1,)), ((), ())), preferred_element_type=f32)


def _rope_tile(a, cosf, sinf):
    return a * cosf + pltpu.roll(a, HEAD_DIM // 2, axis=1) * sinf


def _div_pow2(x, d):
    assert d & (d - 1) == 0
    return lax.shift_right_arithmetic(x, d.bit_length() - 1)


def _sigmoid(x):
    return 1.0 / (1.0 + jnp.exp(-x))


def _proj_kernel(x_ref, w_ref, cs_ref, o_ref, *, mode, scale):
    acc = _dot(x_ref[...], w_ref[...])
    if mode == "rope":
        cosf = cs_ref[:, :HEAD_DIM]
        sinf = cs_ref[:, HEAD_DIM:]
        for h in range(acc.shape[1] // HEAD_DIM):
            sl = slice(h * HEAD_DIM, (h + 1) * HEAD_DIM)
            o_ref[:, sl] = (_rope_tile(acc[:, sl], cosf, sinf) * scale).astype(o_ref.dtype)
    elif mode == "sigmoid":
        o_ref[...] = _sigmoid(acc).astype(o_ref.dtype)
    else:
        o_ref[...] = acc.astype(o_ref.dtype)


def _proj(x, w, cs, *, mode, scale=1.0, out_dtype=f32, tm, tn):
    M, K = x.shape
    N = w.shape[1]
    return pl.pallas_call(
        functools.partial(_proj_kernel, mode=mode, scale=scale),
        grid=(M // tm, N // tn),
        in_specs=[pl.BlockSpec((tm, K), lambda i, j: (i, 0)),
                  pl.BlockSpec((K, tn), lambda i, j: (0, j)),
                  pl.BlockSpec((tm, 2 * HEAD_DIM), lambda i, j: (i, 0))],
        out_specs=pl.BlockSpec((tm, tn), lambda i, j: (i, j)),
        out_shape=jax.ShapeDtypeStruct((M, N), out_dtype),
        compiler_params=_cparams(("parallel", "arbitrary")),
        name=f"proj_{mode}",
    )(x, w, cs)


def _proj_kv_kernel(x_ref, w_ref, cs_ref, st_ref, bf_ref, *, G):
    j = pl.program_id(1)
    acc = _dot(x_ref[...], w_ref[...])
    tm = acc.shape[0]

    def emit(rope):
        for g in range(G):
            y = acc[:, g * HEAD_DIM:(g + 1) * HEAD_DIM]
            if rope:
                y = _rope_tile(y, cs_ref[:, :HEAD_DIM], cs_ref[:, HEAD_DIM:])
            st_ref[pl.ds(g, tm, stride=G), :] = y
            bf_ref[:, g * HEAD_DIM:(g + 1) * HEAD_DIM] = y.astype(bf16)

    is_rope = functools.reduce(jnp.logical_or, [j == s for s in ROPE_KV_SEGS])
    pl.when(is_rope)(lambda: emit(True))
    pl.when(jnp.logical_not(is_rope))(lambda: emit(False))


def _proj_kv(x, w, cs, *, G, tm):
    M, K = x.shape
    N = G * HEAD_DIM
    n_seg = w.shape[1] // N
    return pl.pallas_call(
        functools.partial(_proj_kv_kernel, G=G),
        grid=(M // tm, n_seg),
        in_specs=[pl.BlockSpec((tm, K), lambda i, j: (i, 0)),
                  pl.BlockSpec((K, N), lambda i, j: (0, j)),
                  pl.BlockSpec((tm, 2 * HEAD_DIM), lambda i, j: (i, 0))],
        out_specs=[pl.BlockSpec((None, tm * G, HEAD_DIM), lambda i, j: (j, i, 0)),
                   pl.BlockSpec((None, tm, N), lambda i, j: (j, i, 0))],
        out_shape=[jax.ShapeDtypeStruct((n_seg, M * G, HEAD_DIM), f32),
                   jax.ShapeDtypeStruct((n_seg, M, N), bf16)],
        compiler_params=_cparams(("parallel", "arbitrary")),
        name="proj_kv",
    )(x, w, cs)


def _gateup_kernel(x_ref, wg_ref, wu_ref, o_ref):
    x = x_ref[...]
    g = _dot(x, wg_ref[...])
    u = _dot(x, wu_ref[...])
    o_ref[...] = (g * _sigmoid(g) * u).astype(o_ref.dtype)


def _gateup(x, wg, wu, *, tm, tf):
    M, K = x.shape
    F = wg.shape[1]
    return pl.pallas_call(
        _gateup_kernel,
        grid=(M // tm, F // tf),
        in_specs=[pl.BlockSpec((tm, K), lambda i, j: (i, 0)),
                  pl.BlockSpec((K, tf), lambda i, j: (0, j)),
                  pl.BlockSpec((K, tf), lambda i, j: (0, j))],
        out_specs=pl.BlockSpec((tm, tf), lambda i, j: (i, j)),
        out_shape=jax.ShapeDtypeStruct((M, F), bf16),
        compiler_params=_cparams(("parallel", "arbitrary")),
        name="ffn_gate_up",
    )(x, wg, wu)


def _resid_ln_kernel(*refs, n_pair, alpha, n_j, tn, emit_bf16):
    a_refs = refs[0:2 * n_pair:2]
    w_refs = refs[1:2 * n_pair:2]
    r_ref, g_ref, b_ref, o_ref = refs[2 * n_pair:2 * n_pair + 4]
    obf_ref = refs[2 * n_pair + 4] if emit_bf16 else None
    ybuf = refs[-1]
    j = pl.program_id(1)
    y = alpha * r_ref[...]
    for a_ref, w_ref in zip(a_refs, w_refs):
        y = y + _dot(a_ref[...], w_ref[...])
    ybuf[j] = y

    @pl.when(j == n_j - 1)
    def _():
        n = n_j * tn
        tot = jnp.sum(ybuf[0], axis=-1, keepdims=True)
        for jj in range(1, n_j):
            tot = tot + jnp.sum(ybuf[jj], axis=-1, keepdims=True)
        mu = tot / n
        var = jnp.zeros_like(mu)
        for jj in range(n_j):
            d = ybuf[jj] - mu
            var = var + jnp.sum(d * d, axis=-1, keepdims=True)
        rstd = lax.rsqrt(var / n + LN_EPS)
        for jj in range(n_j):
            cols = slice(jj * tn, (jj + 1) * tn)
            out = (ybuf[jj] - mu) * rstd * g_ref[:, cols] + b_ref[:, cols]
            o_ref[:, cols] = out
            if emit_bf16:
                obf_ref[:, cols] = out.astype(bf16)


def _resid_ln(pairs, resid, gamma, beta, *, alpha, tm, tn, emit_bf16):
    M, N = resid.shape
    n_j = N // tn
    in_specs, args = [], []
    for a, w in pairs:
        k = a.shape[1]
        in_specs += [pl.BlockSpec((tm, k), lambda i, j: (i, 0)), pl.BlockSpec((k, tn), lambda i, j: (0, j))]
        args += [a, w]
    in_specs += [pl.BlockSpec((tm, tn), lambda i, j: (i, j)),
                 pl.BlockSpec((1, N), lambda i, j: (0, 0)),
                 pl.BlockSpec((1, N), lambda i, j: (0, 0))]
    out_dtypes = [f32, bf16] if emit_bf16 else [f32]
    return pl.pallas_call(
        functools.partial(_resid_ln_kernel, n_pair=len(pairs), alpha=alpha, n_j=n_j, tn=tn, emit_bf16=emit_bf16),
        grid=(M // tm, n_j),
        in_specs=in_specs,
        out_specs=[pl.BlockSpec((tm, N), lambda i, j: (i, 0)) for _ in out_dtypes],
        out_shape=[jax.ShapeDtypeStruct((M, N), dt) for dt in out_dtypes],
        scratch_shapes=[pltpu.VMEM((n_j, tm, tn), f32)],
        compiler_params=_cparams(("parallel", "arbitrary")),
        name="matmul_resid_layernorm",
    )(*args, resid, gamma, beta)


def _pool_kernel(hist_ref, u_ref, wp_ref, sc_ref, o_ref, buf, *, tm, pos0):
    i = pl.program_id(1)
    H = POOL_MAX

    @pl.when(i == 0)
    def _():
        buf[0:H, :] = hist_ref[...]

    @pl.when(i > 0)
    def _():
        buf[0:H, :] = buf[tm:tm + H, :]

    buf[H:H + tm, :] = u_ref[...]
    pg = wp_ref.shape[1]
    pos = pos0 + i * tm + lax.broadcasted_iota(i32, (tm, 1), 0)
    for g, w in enumerate(POOL_WINDOWS):
        cols = slice(g * pg, (g + 1) * pg)
        cur = buf[H:H + tm, cols]
        tot = cur
        for k in range(1, w):
            tot = tot + buf[H - k:H - k + tm, cols]
        cnt = jnp.minimum(pos + 1, w).astype(f32)
        d = tot / cnt - cur
        y = _dot(d.astype(bf16), wp_ref[g]) * sc_ref[:, cols]
        o_ref[:, cols] = y.astype(o_ref.dtype)


def _pool_mix(hist, u, w_pool, scale, *, nb, n, pos0, tm, out_dtype):
    W = u.shape[1]
    nt = n // tm
    return pl.pallas_call(
        functools.partial(_pool_kernel, tm=tm, pos0=pos0),
        grid=(nb, nt),
        in_specs=[pl.BlockSpec((None, POOL_MAX, W), lambda b, i: (b, 0, 0)),
                  pl.BlockSpec((tm, W), lambda b, i: (b * nt + i, 0)),
                  pl.BlockSpec(w_pool.shape, lambda b, i: (0, 0, 0)),
                  pl.BlockSpec((1, W), lambda b, i: (0, 0))],
        out_specs=pl.BlockSpec((tm, W), lambda b, i: (b * nt + i, 0)),
        out_shape=jax.ShapeDtypeStruct((nb * n, W), out_dtype),
        scratch_shapes=[pltpu.VMEM((POOL_MAX + tm, W), f32)],
        compiler_params=_cparams(("parallel", "arbitrary")),
        name="pool_mix",
    )(hist, u, w_pool, scale)


def _cmp_part_kernel(pt_ref, *refs, n_pg, G, cpp):
    pages = refs[:n_pg]
    w_ref, o_ref, x_scr = refs[n_pg], refs[n_pg + 1], refs[n_pg + 2]
    rows = n_pg * cpp
    for pg in range(n_pg):
        for p in range(CMP_STRIDE):
            for g in range(G):
                x_scr[g * rows + pg * cpp:g * rows + (pg + 1) * cpp, p * HEAD_DIM:(p + 1) * HEAD_DIM] = (
                    pages[pg][pl.ds(p * G + g, cpp, stride=CMP_STRIDE * G), :])
    part = _dot(x_scr[...].astype(bf16), w_ref[...])
    for g in range(G):
        o_ref[g] = part[g * rows:(g + 1) * rows]


def _cmp_part(pages, page_table, w1cat, *, G, n_pg):
    nb, n_pages = page_table.shape
    prow = pages.shape[1]
    cpp = prow // (CMP_STRIDE * G)
    n_chunk = n_pages * cpp
    rows = n_pg * cpp

    def page_spec(pg):
        return pl.BlockSpec((None, prow, HEAD_DIM), lambda b, j, pt: (pt[b, j * n_pg + pg], 0, 0))

    grid_spec = pltpu.PrefetchScalarGridSpec(
        num_scalar_prefetch=1,
        grid=(nb, n_pages // n_pg),
        in_specs=[page_spec(pg) for pg in range(n_pg)]
        + [pl.BlockSpec(w1cat.shape, lambda b, j, pt: (0, 0))],
        out_specs=pl.BlockSpec((None, G, rows, w1cat.shape[1]), lambda b, j, pt: (b, 0, j, 0)),
        scratch_shapes=[pltpu.VMEM((G * rows, CMP_STRIDE * HEAD_DIM), f32)],
    )
    return pl.pallas_call(
        functools.partial(_cmp_part_kernel, n_pg=n_pg, G=G, cpp=cpp),
        grid_spec=grid_spec,
        out_shape=jax.ShapeDtypeStruct((nb, G, n_chunk, w1cat.shape[1]), f32),
        compiler_params=_cparams(("parallel", "arbitrary")),
        name="cmp_part",
    )(page_table, *([pages] * n_pg), w1cat)


def _gelu_tanh(x):
    return 0.5 * x * (1.0 + jnp.tanh(math.sqrt(2.0 / math.pi) * (x + 0.044715 * (x * x * x))))


def _cmp_finish_kernel(part_ref, pe_ref, w1_ref, w2_ref, cs_ref, o_ref, *, rope, transposed):
    n_chunk = part_ref.shape[0]
    hid0 = _dot(pe_ref[...].astype(bf16), w1_ref[...])[0:1]
    a = part_ref[:, :HEAD_DIM]
    b = pltpu.roll(part_ref[:, HEAD_DIM:], n_chunk - 1, axis=0)
    y = _dot(_gelu_tanh(hid0 + a + b).astype(bf16), w2_ref[...])
    if rope:
        y = _rope_tile(y, cs_ref[:, :HEAD_DIM], cs_ref[:, HEAD_DIM:])
    o_ref[...] = (y.T if transposed else y).astype(o_ref.dtype)


def _cmp_finish(part, pe, w1flat, w2, cs_cmp, *, rope, transposed):
    nb, G, n_chunk, _ = part.shape
    out_dims = (HEAD_DIM, n_chunk) if transposed else (n_chunk, HEAD_DIM)
    return pl.pallas_call(
        functools.partial(_cmp_finish_kernel, rope=rope, transposed=transposed),
        grid=(nb, G),
        in_specs=[pl.BlockSpec((None, None, n_chunk, CMP_RATIO * HEAD_DIM), lambda b, g: (b, g, 0, 0)),
                  pl.BlockSpec(pe.shape, lambda b, g: (0, 0)),
                  pl.BlockSpec(w1flat.shape, lambda b, g: (0, 0)),
                  pl.BlockSpec(w2.shape, lambda b, g: (0, 0)),
                  pl.BlockSpec(cs_cmp.shape, lambda b, g: (0, 0))],
        out_specs=pl.BlockSpec((None, None) + out_dims, lambda b, g: (b, g, 0, 0)),
        out_shape=jax.ShapeDtypeStruct((nb, G) + out_dims, bf16),
        compiler_params=_cparams(("parallel", "parallel")),
        name="cmp_finish",
    )(part, pe, w1flat, w2, cs_cmp)


def _softmax_terms(s_biased, floor=None):
    m = jnp.max(s_biased, axis=-1, keepdims=True)
    if floor is not None:
        m = jnp.maximum(m, floor)
    e = jnp.exp(s_biased - m)
    return e, 1.0 / jnp.maximum(jnp.sum(e, axis=-1, keepdims=True), 1e-30)


def _dot_split3(p, w01):
    hi = p.astype(bf16)
    r1 = p - hi.astype(f32)
    mid = r1.astype(bf16)
    lo = (r1 - mid.astype(f32)).astype(bf16)
    return _dot(hi, w01) + _dot(mid, w01) + _dot(lo, w01)


def _overlap01(n_chunk, n_cmp, ns):
    n = lax.broadcasted_iota(i32, (n_chunk, ns), 0)
    s = lax.broadcasted_iota(i32, (n_chunk, ns), 1)
    ov = ((n * CMP_STRIDE < s * SLC_LEN + SLC_LEN) & (n * CMP_STRIDE + CMP_LEN - 1 >= s * SLC_LEN)
          & (n < n_cmp))
    return jnp.where(ov, 1.0, 0.0).astype(bf16)


def _cmp_bias(q_pos, n_chunk, n_cmp):
    n_io = lax.broadcasted_iota(i32, (q_pos.shape[0], n_chunk), 1)
    return jnp.where((n_io * CMP_STRIDE + CMP_LEN - 1 <= q_pos) & (n_io < n_cmp), 0.0, NEG)


def _selection_values(imp, s_io, cur):
    forced = (s_io == 0) | (s_io == cur) | (s_io == cur - 1)
    return jnp.where(forced, jnp.inf, jnp.where(s_io <= cur, imp, -jnp.inf))


def _ahead(other, val, other_first):
    return (other > val) | ((other == val) & other_first)


def _online_step(s, v, m_old, l_old, acc_old):
    m_new = jnp.maximum(m_old, jnp.max(s, axis=-1, keepdims=True))
    alpha = jnp.exp(m_old - m_new)
    e = jnp.exp(s - m_new)
    l_new = alpha * l_old + jnp.sum(e, axis=-1, keepdims=True)
    acc_new = alpha * acc_old + _dot(e.astype(bf16), v)
    return m_new, l_new, acc_new


def _prompt_attn_kernel(qt_ref, gtt_ref, kcb_ref, vcbt_ref, ks_ref, vst_ref, kw_ref, vwt_ref, oh_ref, o_ref,
                        kaug, qaug, val_scr, cnt_scr, acc_scr, *, R, n_cmp, n_slc, kc):
    c = pl.program_id(2)
    tq = Q_TILE
    nq = R * tq
    n_chunk = kcb_ref.shape[0]
    ns = oh_ref.shape[1]
    nr = val_scr.shape[0]
    win_keys = WINDOW + tq

    @pl.when(c == 0)
    def _():
        kaug[:, :HEAD_DIM] = ks_ref[...]
        kaug[:, HEAD_DIM:] = oh_ref[...]

    def all_heads(a):
        return jnp.concatenate([a] * R, axis=1)

    def softmax_down(s, floor=None):
        m = jnp.max(s, axis=0, keepdims=True)
        if floor is not None:
            m = jnp.maximum(m, floor)
        e = jnp.exp(s - m)
        return e, 1.0 / jnp.maximum(jnp.sum(e, axis=0, keepdims=True), 1e-30)

    for r in range(R):
        qaug[0:HEAD_DIM, r * tq:(r + 1) * tq] = qt_ref[r * HEAD_DIM:(r + 1) * HEAD_DIM, :]
    q_t = qaug[0:HEAD_DIM, :]
    q_pos = c * tq + lax.broadcasted_iota(i32, (1, tq), 1)

    n_io = lax.broadcasted_iota(i32, (n_chunk, tq), 0)
    bias_c = jnp.where((n_io * CMP_STRIDE + CMP_LEN - 1 <= q_pos) & (n_io < n_cmp), 0.0, NEG)
    e, inv = softmax_down(_dot(kcb_ref[...], q_t) + all_heads(bias_c), floor=0.5 * NEG)
    p_c = e * inv
    o_c = _dot(vcbt_ref[...], p_c.astype(bf16))
    p_sum = p_c[:, 0:tq]
    for r in range(1, R):
        p_sum = p_sum + p_c[:, r * tq:(r + 1) * tq]

    blk = lax.broadcasted_iota(i32, (nr, n_chunk), 0)
    n_col = lax.broadcasted_iota(i32, (nr, n_chunk), 1)
    ov = jnp.where((n_col * CMP_STRIDE < blk * SLC_LEN + SLC_LEN)
                   & (n_col * CMP_STRIDE + CMP_LEN - 1 >= blk * SLC_LEN) & (n_col < n_cmp), 1.0, 0.0).astype(bf16)
    hi = p_sum.astype(bf16)
    r1 = p_sum - hi.astype(f32)
    mid = r1.astype(bf16)
    lo = (r1 - mid.astype(f32)).astype(bf16)
    imp_t = _dot(ov, hi) + _dot(ov, mid) + _dot(ov, lo)
    s_io = lax.broadcasted_iota(i32, (nr, tq), 0)
    cur = _div_pow2(c * tq + lax.broadcasted_iota(i32, (nr, tq), 1), SLC_LEN)
    val_scr[...] = _selection_values(imp_t, s_io, cur)
    cnt_scr[...] = jnp.zeros(cnt_scr.shape, f32)
    n_live = _div_pow2(c * tq + tq - 1, SLC_LEN) + 1
    for grp in range(nr // SUBLANES):
        @pl.when(grp * SUBLANES < n_live)
        def _():
            val = val_scr[...]
            cnt = cnt_scr[...]
            for sp in range(grp * SUBLANES, min((grp + 1) * SUBLANES, n_slc)):
                cnt = cnt + jnp.where(_ahead(val_scr[sp:sp + 1, :], val, s_io > sp), 1.0, 0.0)
            cnt_scr[...] = cnt

    first_blk = _div_pow2(c * tq, SLC_LEN)
    keep = (cnt_scr[...] < N_SEL) & (val_scr[...] > -jnp.inf) & (s_io < first_blk)
    mask_t = jnp.where(keep, 0.0, NEG)
    if nr < ns:
        mask_t = jnp.concatenate([mask_t, jnp.full((ns - nr, tq), NEG, f32)], axis=0)
    qaug[HEAD_DIM:, :] = all_heads(mask_t.astype(bf16))

    d0 = pl.multiple_of(c * tq, tq)
    tri = jnp.where(lax.broadcasted_iota(i32, (tq, tq), 0) <= lax.broadcasted_iota(i32, (tq, tq), 1), 0.0, NEG)
    s_d = _dot(ks_ref[pl.ds(d0, tq), :], q_t) + all_heads(tri)
    m_d = jnp.max(s_d, axis=0, keepdims=True)
    e_d = jnp.exp(s_d - m_d)
    l_d = jnp.sum(e_d, axis=0, keepdims=True)
    acc_scr[...] = _dot(vst_ref[c], e_d.astype(bf16))

    tiles_per_chunk = kc // tq

    def sweep(j, carry):
        m_old, l_old = carry
        k0 = pl.multiple_of(j * kc, kc)
        s = _dot(kaug[pl.ds(k0, kc), :], qaug[...])
        m_new = jnp.maximum(m_old, jnp.max(s, axis=0, keepdims=True))
        alpha = jnp.exp(m_old - m_new)
        e = jnp.exp(s - m_new)
        v_t = jnp.concatenate([vst_ref[j * tiles_per_chunk + i] for i in range(tiles_per_chunk)], axis=1)
        acc_scr[...] = alpha * acc_scr[...] + _dot(v_t, e.astype(bf16))
        return m_new, alpha * l_old + jnp.sum(e, axis=0, keepdims=True)

    _, l_s = lax.fori_loop(0, (c * tq + kc - 1) // kc, sweep, (m_d, l_d))
    o_s = acc_scr[...] * (1.0 / jnp.maximum(l_s, 1e-30))

    w_tile = jnp.maximum(c - WINDOW // tq, 0)
    w0 = pl.multiple_of(w_tile * tq, tq)
    key_w = w0 + lax.broadcasted_iota(i32, (win_keys, tq), 0)
    bias_w = jnp.where((key_w <= q_pos) & (q_pos - key_w < WINDOW), 0.0, NEG)
    e, inv = softmax_down(_dot(kw_ref[pl.ds(w0, win_keys), :], q_t) + all_heads(bias_w))
    vw_t = jnp.concatenate([vwt_ref[w_tile + i] for i in range(win_keys // tq)], axis=1)
    o_w = _dot(vw_t, e.astype(bf16)) * inv

    for r in range(R):
        cols = slice(r * tq, (r + 1) * tq)
        g = [gtt_ref[r * N_BRANCH + br:r * N_BRANCH + br + 1, :] for br in range(N_BRANCH)]
        o_t = g[0] * o_c[:, cols] + g[1] * o_s[:, cols] + g[2] * o_w[:, cols]
        o_ref[:, r * HEAD_DIM:(r + 1) * HEAD_DIM] = o_t.T.astype(o_ref.dtype)


def _prompt_attn(q_t, gates_t, kcb, vcb_t, ks, vs_t, kw, vw_t, onehot, *, B, T, G, R):
    n_chunk = kcb.shape[2]
    n_cmp = n_chunk - CMP_RATIO + 1
    n_slc = pl.cdiv(T, SLC_LEN)
    ns = onehot.shape[1]
    nr = _round_up(n_slc, SUBLANES)
    kc = _pick_tile(T, 512, Q_TILE)
    nt = T // Q_TILE
    assert T % Q_TILE == 0 and T >= WINDOW + Q_TILE and Q_TILE % SLC_LEN == 0 and Q_TILE <= 2 * SLC_LEN
    seq_spec = pl.BlockSpec((T, HEAD_DIM), lambda b, g, c: (b, g))
    tile_spec = pl.BlockSpec((nt, HEAD_DIM, Q_TILE), lambda b, g, c: (b, g, 0))
    return pl.pallas_call(
        functools.partial(_prompt_attn_kernel, R=R, n_cmp=n_cmp, n_slc=n_slc, kc=kc),
        grid=(B, G, nt),
        in_specs=[pl.BlockSpec((R * HEAD_DIM, Q_TILE), lambda b, g, c: (g, b * nt + c)),
                  pl.BlockSpec((LANES, Q_TILE), lambda b, g, c: (g, b * nt + c)),
                  pl.BlockSpec((None, None, n_chunk, HEAD_DIM), lambda b, g, c: (b, g, 0, 0)),
                  pl.BlockSpec((None, None, HEAD_DIM, n_chunk), lambda b, g, c: (b, g, 0, 0)),
                  seq_spec, tile_spec, seq_spec, tile_spec,
                  pl.BlockSpec((T, ns), lambda b, g, c: (0, 0))],
        out_specs=pl.BlockSpec((Q_TILE, R * HEAD_DIM), lambda b, g, c: (b * nt + c, g)),
        out_shape=jax.ShapeDtypeStruct((B * T, G * R * HEAD_DIM), bf16),
        scratch_shapes=[pltpu.VMEM((T, HEAD_DIM + ns), bf16),
                        pltpu.VMEM((HEAD_DIM + ns, R * Q_TILE), bf16),
                        pltpu.VMEM((nr, Q_TILE), f32), pltpu.VMEM((nr, Q_TILE), f32),
                        pltpu.VMEM((HEAD_DIM, R * Q_TILE), f32)],
        compiler_params=_cparams(("parallel", "parallel", "arbitrary")),
        name="nsa_prompt",
    )(q_t, gates_t, kcb, vcb_t, ks, vs_t, kw, vw_t, onehot)


def _sample_cmp_win_kernel(q_ref, gt_ref, kcb_ref, vcb_ref, kwin_ref, vwin_ref, kwn_ref, vwn_ref,
                           part_ref, qaug_ref, kbuf, vbuf, *, G, R, n_cmp, n_slc, past):
    n_chunk = kcb_ref.shape[1]
    ns = qaug_ref.shape[1] - G * HEAD_DIM
    wb = kwin_ref.shape[0] // G
    sp = SUBLANES
    rows = R * sp
    nbuf = kbuf.shape[0]
    q_pos = past + (lax.broadcasted_iota(i32, (rows, 1), 0) & (sp - 1))
    bias_c = _cmp_bias(q_pos, n_chunk, n_cmp)
    key_w = past - wb + lax.broadcasted_iota(i32, (rows, nbuf), 1)
    bias_w = jnp.where((key_w <= q_pos) & (q_pos - key_w < WINDOW) & (key_w >= 0), 0.0, NEG)
    ov = _overlap01(n_chunk, n_cmp, ns)
    s_io = lax.broadcasted_iota(i32, (sp, ns), 1)
    cur = _div_pow2(q_pos[0:sp], SLC_LEN)
    qaug_ref[:, :G * HEAD_DIM] = jnp.zeros((G * rows, G * HEAD_DIM), bf16)

    for g in range(G):
        q = q_ref[g]
        cols = slice(g * HEAD_DIM, (g + 1) * HEAD_DIM)

        e, inv = _softmax_terms(_dot_nt(q, kcb_ref[g]) + bias_c, floor=0.5 * NEG)
        p_c = e * inv
        o_c = _dot(p_c.astype(bf16), vcb_ref[g])
        p_sum = p_c[0:sp]
        for r in range(1, R):
            p_sum = p_sum + p_c[r * sp:(r + 1) * sp]
        val = _selection_values(_dot_split3(p_sum, ov), s_io, cur)
        cnt = jnp.zeros((sp, ns), f32)
        for blk in range(n_slc):
            cnt = cnt + jnp.where(_ahead(val[:, blk:blk + 1], val, s_io > blk), 1.0, 0.0)
        mask = jnp.where((cnt < N_SEL) & (val > -jnp.inf), 0.0, NEG).astype(bf16)

        qaug_ref[g * rows:(g + 1) * rows, cols] = q
        qaug_ref[g * rows:(g + 1) * rows, G * HEAD_DIM:] = jnp.concatenate([mask] * R, axis=0)

        kbuf[0:wb, :] = kwin_ref[pl.ds(g, wb, stride=G), :].astype(bf16)
        kbuf[wb:nbuf, :] = kwn_ref[:, cols]
        vbuf[0:wb, :] = vwin_ref[pl.ds(g, wb, stride=G), :].astype(bf16)
        vbuf[wb:nbuf, :] = vwn_ref[:, cols]
        e, inv = _softmax_terms(_dot_nt(q, kbuf[...]) + bias_w)
        o_w = _dot(e.astype(bf16), vbuf[...]) * inv
        part_ref[g] = gt_ref[g][:, 0:1] * o_c + gt_ref[g][:, 2:3] * o_w


def _sample_cmp_win(q, gates, kcb, vcb, kwin, vwin, kw_new, vw_new, *, past, n_slc, ns):
    DB, G, rows, _ = q.shape
    R = rows // SUBLANES
    n_chunk = kcb.shape[2]
    n_cmp = n_chunk - CMP_RATIO + 1
    wbg = kwin.shape[1]
    npad = kw_new.shape[1]
    qspec = pl.BlockSpec((None, G, rows, HEAD_DIM), lambda b: (b, 0, 0, 0))
    cspec = pl.BlockSpec((None, G, n_chunk, HEAD_DIM), lambda b: (b, 0, 0, 0))
    wspec = pl.BlockSpec((None, wbg, HEAD_DIM), lambda b: (b, 0, 0))
    nspec = pl.BlockSpec((None, npad, G * HEAD_DIM), lambda b: (b, 0, 0))
    aug_w = G * HEAD_DIM + ns
    return pl.pallas_call(
        functools.partial(_sample_cmp_win_kernel, G=G, R=R, n_cmp=n_cmp, n_slc=n_slc, past=past),
        grid=(DB,),
        in_specs=[qspec, qspec, cspec, cspec, wspec, wspec, nspec, nspec],
        out_specs=[qspec, pl.BlockSpec((None, G * rows, aug_w), lambda b: (b, 0, 0))],
        out_shape=[jax.ShapeDtypeStruct((DB, G, rows, HEAD_DIM), f32),
                   jax.ShapeDtypeStruct((DB, G * rows, aug_w), bf16)],
        scratch_shapes=[pltpu.VMEM((wbg // G + npad, HEAD_DIM), bf16)] * 2,
        compiler_params=_cparams(("parallel",)),
        name="nsa_sample_cmp_win",
    )(q, gates, kcb, vcb, kwin, vwin, kw_new, vw_new)


def _sample_sel_kernel(pt_ref, *refs, n_pg, G, R, page):
    kpages = refs[:n_pg]
    vpages = refs[n_pg:2 * n_pg]
    (oh_ref, ohn_ref, qaug_ref, gt_ref, part_ref, ksn_ref, vsn_ref, o_ref,
     kaug, vall, m_scr, l_scr, acc_scr) = refs[2 * n_pg:]
    j = pl.program_id(1)
    sp = SUBLANES
    rows = R * sp
    kvw = G * HEAD_DIM

    @pl.when(j == 0)
    def _():
        m_scr[...] = jnp.full(m_scr.shape, NEG, f32)
        l_scr[...] = jnp.zeros(l_scr.shape, f32)
        acc_scr[...] = jnp.zeros(acc_scr.shape, f32)

    def update(n_keys, bias):
        s = _dot_nt(qaug_ref[...], kaug[0:n_keys, :])
        if bias is not None:
            s = s + bias
        m, l, acc = _online_step(s, vall[0:n_keys, :], m_scr[...], l_scr[...], acc_scr[...])
        m_scr[...] = m
        l_scr[...] = l
        acc_scr[...] = acc

    for pg in range(n_pg):
        for g in range(G):
            kaug[pg * page:(pg + 1) * page, g * HEAD_DIM:(g + 1) * HEAD_DIM] = (
                kpages[pg][pl.ds(g, page, stride=G), :].astype(bf16))
            vall[pg * page:(pg + 1) * page, g * HEAD_DIM:(g + 1) * HEAD_DIM] = (
                vpages[pg][pl.ds(g, page, stride=G), :].astype(bf16))
    kaug[:, kvw:] = oh_ref[...]
    update(n_pg * page, None)

    @pl.when(j == pl.num_programs(1) - 1)
    def _():
        kaug[0:page, :kvw] = ksn_ref[...]
        kaug[0:page, kvw:] = ohn_ref[...]
        vall[0:page, :] = vsn_ref[...]
        step = lax.broadcasted_iota(i32, (G * rows, page), 0) & (sp - 1)
        causal = jnp.where(lax.broadcasted_iota(i32, (G * rows, page), 1) <= step, 0.0, NEG)
        update(page, causal)
        inv = 1.0 / jnp.maximum(l_scr[...], 1e-30)
        for g in range(G):
            rs = slice(g * rows, (g + 1) * rows)
            o_s = acc_scr[rs, g * HEAD_DIM:(g + 1) * HEAD_DIM] * inv[rs]
            o_ref[g] = part_ref[g] + gt_ref[g][:, 1:2] * o_s


def _sample_sel(kcache, vcache, page_table, onehot, qaug, gates, part, ks_new, vs_new, *, G, n_pg):
    DB, _, rows, _ = part.shape
    R = rows // SUBLANES
    n_pages = page_table.shape[1]
    page = kcache.shape[1] // G
    ns = onehot.shape[1]
    aug_w = G * HEAD_DIM + ns
    n_keys = n_pg * page

    def page_spec(pg):
        return pl.BlockSpec((None, page * G, HEAD_DIM), lambda b, j, pt: (pt[b, j * n_pg + pg], 0, 0))

    pspec = pl.BlockSpec((None, G, rows, HEAD_DIM), lambda b, j, pt: (b, 0, 0, 0))
    nspec = pl.BlockSpec((None, page, G * HEAD_DIM), lambda b, j, pt: (b, 0, 0))
    grid_spec = pltpu.PrefetchScalarGridSpec(
        num_scalar_prefetch=1,
        grid=(DB, n_pages // n_pg),
        in_specs=[page_spec(pg) for pg in range(n_pg)] * 2
        + [pl.BlockSpec((n_keys, ns), lambda b, j, pt: (j, 0)),
           pl.BlockSpec((page, ns), lambda b, j, pt: (n_pages, 0)),
           pl.BlockSpec((None, G * rows, aug_w), lambda b, j, pt: (b, 0, 0)),
           pspec, pspec, nspec, nspec],
        out_specs=pspec,
        scratch_shapes=[pltpu.VMEM((n_keys, aug_w), bf16), pltpu.VMEM((n_keys, G * HEAD_DIM), bf16),
                        pltpu.VMEM((G * rows, 1), f32), pltpu.VMEM((G * rows, 1), f32),
                        pltpu.VMEM((G * rows, G * HEAD_DIM), f32)],
    )
    return pl.pallas_call(
        functools.partial(_sample_sel_kernel, n_pg=n_pg, G=G, R=R, page=page),
        grid_spec=grid_spec,
        out_shape=jax.ShapeDtypeStruct(part.shape, f32),
        compiler_params=_cparams(("parallel", "arbitrary")),
        name="nsa_sample_sel",
    )(page_table, *([kcache] * n_pg), *([vcache] * n_pg), onehot, onehot, qaug, gates, part, ks_new, vs_new)


def _rope_table(pos):
    half = HEAD_DIM // 2
    inv = ROPE_THETA ** (-jnp.arange(half, dtype=f32) / half)
    ang = pos.astype(f32)[:, None] * inv[None, :]
    cos, sin = jnp.cos(ang), jnp.sin(ang)
    return jnp.concatenate([cos, cos, -sin, sin], axis=-1)


def _block_onehot(n_keys, ns):
    blk = jnp.arange(n_keys, dtype=i32)[:, None] // SLC_LEN
    return (blk == jnp.arange(ns, dtype=i32)[None, :]).astype(bf16)


def kernel(x_prompt, x_sample, cache_k_cmp, cache_v_cmp, cache_k_slc, cache_v_slc, state_k_win, state_v_win, state_pool, page_table, w_in, w_cmp1_k, pe_cmp_k, w_cmp2_k, w_cmp1_v, pe_cmp_v, w_cmp2_v, w_pool, pool_scale, w_o, ln1_g, ln1_b, w_gate, w_up, w_down, ln2_g, ln2_b):
    B, T, D = x_prompt.shape
    DB, S, _ = x_sample.shape
    depth, n_phys, page, G, _ = cache_k_cmp.shape
    assert depth == 1 and S <= SUBLANES
    n_pages = page_table.shape[1]
    past = n_pages * page
    PW = state_pool.shape[3]
    NW = D - PW
    NH = NW // HEAD_DIM
    R = NH // G
    KVW = G * HEAD_DIM
    F = w_gate.shape[2]
    wb = state_k_win.shape[2]
    wbp = min(WINDOW, T)
    alpha = (2 * depth) ** 0.25
    MP = B * T
    MS = DB * S
    M = MP + MS
    assert T % page == 0 and page % CMP_STRIDE == 0 and page % SLC_LEN == 0 and past % SLC_LEN == 0

    x_all = jnp.concatenate([x_prompt.reshape(MP, D), x_sample.reshape(MS, D)], axis=0)
    x_bf = x_all.astype(bf16)
    w_in0 = w_in[0]
    o_kv = PW + NW
    o_g = o_kv + N_KV_SEG * KVW
    w_u = w_in0[:, :PW].astype(bf16)
    w_q = w_in0[:, PW:o_kv].astype(bf16)
    w_kv = w_in0[:, o_kv:o_g].astype(bf16)
    w_g = w_in0[:, o_g:o_g + N_BRANCH * NH].reshape(D, G, R * N_BRANCH)
    w_g = jnp.pad(w_g, ((0, 0), (0, 0), (0, LANES - R * N_BRANCH))).reshape(D, G * LANES).astype(bf16)

    pos_all = jnp.concatenate([jnp.tile(jnp.arange(T, dtype=i32), B),
                               jnp.tile(past + jnp.arange(S, dtype=i32), DB)])
    cs_all = _rope_table(pos_all)

    tm = _pick_tile(M, 640, SUBLANES)
    u = _proj(x_bf, w_u, cs_all, mode="none", tm=tm, tn=_pick_tile(PW, 512, LANES))
    q = _proj(x_bf, w_q, cs_all, mode="rope", scale=HEAD_DIM ** -0.5, out_dtype=bf16, tm=tm,
              tn=_pick_tile(NW, 512, LANES))
    gates = _proj(x_bf, w_g, cs_all, mode="sigmoid", tm=tm, tn=_pick_tile(G * LANES, 512, LANES))
    kv_st, kv_bf = _proj_kv(x_bf, w_kv, cs_all, G=G, tm=tm)
    kc_st, vc_st = kv_st[0], kv_st[1]
    ks_bf, vs_bf, kw_bf, vw_bf = kv_bf[2], kv_bf[3], kv_bf[4], kv_bf[5]

    wp_bf = w_pool[0].astype(bf16)
    pscale = pool_scale[0].reshape(1, PW)
    pool_p = _pool_mix(jnp.zeros((B, POOL_MAX, PW), f32), u, wp_bf, pscale, nb=B, n=T, pos0=0,
                       tm=_pick_tile(T, 512, SUBLANES), out_dtype=bf16)
    u_s = u[MP:].reshape(DB, S, PW)
    hist_s = jnp.pad(state_pool[0], ((0, 0), (POOL_MAX - state_pool.shape[2], 0), (0, 0)))
    u_s_pad = jnp.pad(u_s, ((0, 0), (0, SUBLANES - S), (0, 0))).reshape(DB * SUBLANES, PW)
    pool_s = _pool_mix(hist_s, u_s_pad, wp_bf, pscale, nb=DB, n=SUBLANES, pos0=past, tm=SUBLANES,
                       out_dtype=f32).reshape(DB, SUBLANES, PW)[:, :S]

    def cmp_weights(w1, pe, w2):
        w1cat = w1[0].reshape(CMP_RATIO, CMP_STRIDE, HEAD_DIM, HEAD_DIM).transpose(1, 2, 0, 3)
        w1cat = w1cat.reshape(CMP_STRIDE * HEAD_DIM, CMP_RATIO * HEAD_DIM).astype(bf16)
        pe8 = jnp.broadcast_to(pe[0].reshape(1, CMP_LEN * HEAD_DIM), (SUBLANES, CMP_LEN * HEAD_DIM))
        return w1cat, pe8, w1[0].reshape(CMP_LEN * HEAD_DIM, HEAD_DIM).astype(bf16), w2[0].astype(bf16)

    cw_k = cmp_weights(w_cmp1_k, pe_cmp_k, w_cmp2_k)
    cw_v = cmp_weights(w_cmp1_v, pe_cmp_v, w_cmp2_v)

    def compress(pages, table, cw, rope, transposed=False):
        n_chunk = table.shape[1] * (page // CMP_STRIDE)
        cs_cmp = _rope_table(jnp.arange(n_chunk, dtype=i32) * CMP_STRIDE + CMP_LEN - 1)
        part = _cmp_part(pages, table, cw[0], G=G, n_pg=_pick_tile(table.shape[1], 8, 1))
        return _cmp_finish(part, cw[1], cw[2], cw[3], cs_cmp, rope=rope, transposed=transposed)

    def as_pages(a):
        return a.reshape(-1, page * G, HEAD_DIM)

    table_p = jnp.arange(MP // page, dtype=i32).reshape(B, T // page)
    kcb_p = compress(as_pages(kc_st[:MP * G]), table_p, cw_k, True)
    vcb_p_t = compress(as_pages(vc_st[:MP * G]), table_p, cw_v, False, transposed=True)
    kcb_s = compress(as_pages(cache_k_cmp.reshape(-1, HEAD_DIM)), page_table, cw_k, True)
    vcb_s = compress(as_pages(cache_v_cmp.reshape(-1, HEAD_DIM)), page_table, cw_v, False)

    ns_p = _round_up(pl.cdiv(T, SLC_LEN), LANES)
    def value_tiles_t(a):
        return a[:MP].reshape(MP // Q_TILE, Q_TILE, KVW).transpose(0, 2, 1)

    nsa_p = _prompt_attn(q.T, gates.T, kcb_p, vcb_p_t, ks_bf, value_tiles_t(vs_bf), kw_bf, value_tiles_t(vw_bf),
                         _block_onehot(T, ns_p), B=B, T=T, G=G, R=R)

    def srow(a):
        return a[MP:].reshape(DB, S, a.shape[1])

    def to_rows(a):
        a = jnp.pad(a.transpose(0, 2, 3, 1, 4), ((0, 0), (0, 0), (0, 0), (0, SUBLANES - S), (0, 0)))
        return a.reshape(DB, G, R * SUBLANES, a.shape[-1])

    q_s = to_rows(srow(q).reshape(DB, S, G, R, HEAD_DIM))
    g_s = srow(gates).reshape(DB, S, G, LANES)[..., :R * N_BRANCH].reshape(DB, S, G, R, N_BRANCH)
    g_s = to_rows(jnp.pad(g_s, ((0, 0),) * 4 + ((0, LANES - N_BRANCH),)))

    def new_rows(a):
        return jnp.pad(srow(a), ((0, 0), (0, page - S), (0, 0)))

    n_slc_s = pl.cdiv(past + S, SLC_LEN)
    ns_s = _round_up(n_slc_s, LANES)
    part_s, qaug_s = _sample_cmp_win(q_s, g_s, kcb_s, vcb_s,
                                     state_k_win.reshape(DB, wb * G, HEAD_DIM),
                                     state_v_win.reshape(DB, wb * G, HEAD_DIM),
                                     new_rows(kw_bf), new_rows(vw_bf), past=past, n_slc=n_slc_s, ns=ns_s)
    o_s = _sample_sel(as_pages(cache_k_slc.reshape(-1, HEAD_DIM)), as_pages(cache_v_slc.reshape(-1, HEAD_DIM)),
                      page_table, _block_onehot(past + page, ns_s), qaug_s, g_s, part_s,
                      new_rows(ks_bf), new_rows(vs_bf), G=G, n_pg=_pick_tile(n_pages, 8, 1))
    nsa_s = o_s.reshape(DB, G, R, SUBLANES, HEAD_DIM)[:, :, :, :S].transpose(0, 3, 1, 2, 4)
    nsa_s = nsa_s.reshape(MS, NW).astype(bf16)

    pool_all = jnp.concatenate([pool_p, pool_s.reshape(MS, PW).astype(bf16)], axis=0)
    nsa_all = jnp.concatenate([nsa_p, nsa_s], axis=0)
    wo_bf = w_o[0].astype(bf16)
    tm_ln = _pick_tile(M, 320, SUBLANES)
    h, h_bf = _resid_ln([(pool_all, wo_bf[:PW]), (nsa_all, wo_bf[PW:])], x_all,
                        ln1_g[0].reshape(1, D), ln1_b[0].reshape(1, D),
                        alpha=alpha, tm=tm_ln, tn=_pick_tile(D, 512, LANES), emit_bf16=True)
    tf = 512
    F_pad = _round_up(F, tf)
    wg_bf = jnp.pad(w_gate[0], ((0, 0), (0, F_pad - F))).astype(bf16)
    wu_bf = jnp.pad(w_up[0], ((0, 0), (0, F_pad - F))).astype(bf16)
    wd_bf = jnp.pad(w_down[0], ((0, F_pad - F), (0, 0))).astype(bf16)
    ff = _gateup(h_bf, wg_bf, wu_bf, tm=tm, tf=tf)
    y, = _resid_ln([(ff, wd_bf)], h, ln2_g[0].reshape(1, D), ln2_b[0].reshape(1, D),
                   alpha=alpha, tm=tm_ln, tn=_pick_tile(D, 256, LANES), emit_bf16=False)

    def pstate(i):
        return kv_st[i, :MP * G].reshape(1, B, T, G, HEAD_DIM)

    def sstate(i):
        return kv_st[i, MP * G:].reshape(1, DB, S, G, HEAD_DIM)

    kw_p, vw_p = pstate(4)[:, :, T - wbp:], pstate(5)[:, :, T - wbp:]
    pool_state_p = u[:MP].reshape(1, B, T, PW)[:, :, T - (POOL_MAX - 1):]
    kw_s = jnp.concatenate([state_k_win, sstate(4)], axis=2)[:, :, S:]
    vw_s = jnp.concatenate([state_v_win, sstate(5)], axis=2)[:, :, S:]
    pool_state_s = jnp.concatenate([state_pool, u_s[None]], axis=2)[:, :, S:]
    return (y[:MP].reshape(B, T, D), y[MP:].reshape(DB, S, D),
            pstate(0), pstate(1), pstate(2), pstate(3), kw_p, vw_p, pool_state_p,
            sstate(0), sstate(1), sstate(2), sstate(3), kw_s, vw_s, pool_state_s)
```

```python
import functools
import math

import jax
import jax.numpy as jnp
from jax import lax
from jax.experimental import pallas as pl
from jax.experimental.pallas import tpu as pltpu

HEAD_DIM = 128
POOL_WINDOWS = (2, 4, 8, 16)
POOL_MAX = max(POOL_WINDOWS)
N_BRANCH = 3
N_KV_SEG = 6
ROPE_KV_SEGS = (2, 4)
CMP_LEN = 32
CMP_STRIDE = 16
CMP_RATIO = CMP_LEN // CMP_STRIDE
SLC_LEN = 64
N_SEL = 16
WINDOW = 512
Q_TILE = 128
ROPE_THETA = 10000.0
LN_EPS = 1e-5
LANES = 128
SUBLANES = 8
VMEM_LIMIT = 56 * 1024 * 1024
NEG = -1e30

f32 = jnp.float32
bf16 = jnp.bfloat16
i32 = jnp.int32


def _cparams(sem):
    return pltpu.CompilerParams(dimension_semantics=sem, vmem_limit_bytes=VMEM_LIMIT)


def _pick_tile(n, target, align):
    best = None
    for t in range(align, min(n, target) + 1, align):
        if n % t == 0:
            best = t
    if best is None:
        raise ValueError(f"no tile for {n} (target {target}, align {align})")
    return best


def _round_up(n, m):
    return m * pl.cdiv(n, m)


def _dot(a, b):
    return jnp.dot(a, b, preferred_element_type=f32)


def _dot_nt(a, b):
    return lax.dot_general(a, b, (((1,), (1,)), ((), ())), preferred_element_type=f32)


def _rope_tile(a, cosf, sinf):
    return a * cosf + pltpu.roll(a, HEAD_DIM // 2, axis=1) * sinf


def _div_pow2(x, d):
    assert d & (d - 1) == 0
    return lax.shift_right_arithmetic(x, d.bit_length() - 1)


def _sigmoid(x):
    return 1.0 / (1.0 + jnp.exp(-x))


def _proj_kernel(x_ref, w_ref, cs_ref, o_ref, *, mode, scale):
    acc = _dot(x_ref[...], w_ref[...])
    if mode == "rope":
        cosf = cs_ref[:, :HEAD_DIM]
        sinf = cs_ref[:, HEAD_DIM:]
        for h in range(acc.shape[1] // HEAD_DIM):
            sl = slice(h * HEAD_DIM, (h + 1) * HEAD_DIM)
            o_ref[:, sl] = (_rope_tile(acc[:, sl], cosf, sinf) * scale).astype(o_ref.dtype)
    elif mode == "sigmoid":
        o_ref[...] = _sigmoid(acc).astype(o_ref.dtype)
    else:
        o_ref[...] = acc.astype(o_ref.dtype)


def _proj(x, w, cs, *, mode, scale=1.0, out_dtype=f32, tm, tn):
    M, K = x.shape
    N = w.shape[1]
    return pl.pallas_call(
        functools.partial(_proj_kernel, mode=mode, scale=scale),
        grid=(M // tm, N // tn),
        in_specs=[pl.BlockSpec((tm, K), lambda i, j: (i, 0)),
                  pl.BlockSpec((K, tn), lambda i, j: (0, j)),
                  pl.BlockSpec((tm, 2 * HEAD_DIM), lambda i, j: (i, 0))],
        out_specs=pl.BlockSpec((tm, tn), lambda i, j: (i, j)),
        out_shape=jax.ShapeDtypeStruct((M, N), out_dtype),
        compiler_params=_cparams(("parallel", "arbitrary")),
        name=f"proj_{mode}",
    )(x, w, cs)


def _proj_kv_kernel(x_ref, w_ref, cs_ref, st_ref, bf_ref, *, G):
    j = pl.program_id(1)
    acc = _dot(x_ref[...], w_ref[...])
    tm = acc.shape[0]

    def emit(rope):
        for g in range(G):
            y = acc[:, g * HEAD_DIM:(g + 1) * HEAD_DIM]
            if rope:
                y = _rope_tile(y, cs_ref[:, :HEAD_DIM], cs_ref[:, HEAD_DIM:])
            st_ref[pl.ds(g, tm, stride=G), :] = y
            bf_ref[:, g * HEAD_DIM:(g + 1) * HEAD_DIM] = y.astype(bf16)

    is_rope = functools.reduce(jnp.logical_or, [j == s for s in ROPE_KV_SEGS])
    pl.when(is_rope)(lambda: emit(True))
    pl.when(jnp.logical_not(is_rope))(lambda: emit(False))


def _proj_kv(x, w, cs, *, G, tm):
    M, K = x.shape
    N = G * HEAD_DIM
    n_seg = w.shape[1] // N
    return pl.pallas_call(
        functools.partial(_proj_kv_kernel, G=G),
        grid=(M // tm, n_seg),
        in_specs=[pl.BlockSpec((tm, K), lambda i, j: (i, 0)),
                  pl.BlockSpec((K, N), lambda i, j: (0, j)),
                  pl.BlockSpec((tm, 2 * HEAD_DIM), lambda i, j: (i, 0))],
        out_specs=[pl.BlockSpec((None, tm * G, HEAD_DIM), lambda i, j: (j, i, 0)),
                   pl.BlockSpec((None, tm, N), lambda i, j: (j, i, 0))],
        out_shape=[jax.ShapeDtypeStruct((n_seg, M * G, HEAD_DIM), f32),
                   jax.ShapeDtypeStruct((n_seg, M, N), bf16)],
        compiler_params=_cparams(("parallel", "arbitrary")),
        name="proj_kv",
    )(x, w, cs)


def _gateup_kernel(x_ref, wg_ref, wu_ref, o_ref):
    x = x_ref[...]
    g = _dot(x, wg_ref[...])
    u = _dot(x, wu_ref[...])
    o_ref[...] = (g * _sigmoid(g) * u).astype(o_ref.dtype)


def _gateup(x, wg, wu, *, tm, tf):
    M, K = x.shape
    F = wg.shape[1]
    return pl.pallas_call(
        _gateup_kernel,
        grid=(M // tm, F // tf),
        in_specs=[pl.BlockSpec((tm, K), lambda i, j: (i, 0)),
                  pl.BlockSpec((K, tf), lambda i, j: (0, j)),
                  pl.BlockSpec((K, tf), lambda i, j: (0, j))],
        out_specs=pl.BlockSpec((tm, tf), lambda i, j: (i, j)),
        out_shape=jax.ShapeDtypeStruct((M, F), bf16),
        compiler_params=_cparams(("parallel", "arbitrary")),
        name="ffn_gate_up",
    )(x, wg, wu)


def _resid_ln_kernel(*refs, n_pair, alpha, n_j, n_kseg, tn, emit_bf16):
    a_refs = refs[0:2 * n_pair:2]
    w_refs = refs[1:2 * n_pair:2]
    r_ref, g_ref, b_ref, o_ref = refs[2 * n_pair:2 * n_pair + 4]
    obf_ref = refs[2 * n_pair + 4] if emit_bf16 else None
    ybuf, mu_scr, rstd_scr = refs[-3:]
    k = pl.program_id(1)
    j = pl.program_id(2)

    def partial_product():
        acc = _dot(a_refs[0][...], w_refs[0][...])
        for a_ref, w_ref in zip(a_refs[1:], w_refs[1:]):
            acc = acc + _dot(a_ref[...], w_ref[...])
        return acc

    @pl.when(k == 0)
    def _():
        ybuf[j] = alpha * r_ref[...] + partial_product()

    if n_kseg > 1:
        @pl.when((k > 0) & (k < n_kseg))
        def _():
            ybuf[j] = ybuf[j] + partial_product()

    @pl.when(k == n_kseg)
    def _():
        @pl.when(j == 0)
        def _():
            n = n_j * tn
            tot = jnp.sum(ybuf[0], axis=-1, keepdims=True)
            for jj in range(1, n_j):
                tot = tot + jnp.sum(ybuf[jj], axis=-1, keepdims=True)
            mu = tot / n
            var = jnp.zeros_like(mu)
            for jj in range(n_j):
                d = ybuf[jj] - mu
                var = var + jnp.sum(d * d, axis=-1, keepdims=True)
            mu_scr[...] = mu
            rstd_scr[...] = lax.rsqrt(var / n + LN_EPS)

        out = (ybuf[j] - mu_scr[...]) * rstd_scr[...] * g_ref[...] + b_ref[...]
        o_ref[...] = out
        if emit_bf16:
            obf_ref[...] = out.astype(bf16)


def _resid_ln(pairs, resid, gamma, beta, *, alpha, tm, tn, n_kseg, emit_bf16):
    M, N = resid.shape
    n_j = N // tn
    last = n_kseg - 1

    def seg(k):
        return jnp.minimum(k, last)

    def col(k, j):
        return jnp.where(k < n_kseg, j, n_j - 1)

    in_specs, args = [], []
    for a, w in pairs:
        ks = a.shape[1] // n_kseg
        assert ks * n_kseg == a.shape[1] and ks % LANES == 0
        in_specs += [pl.BlockSpec((tm, ks), lambda i, k, j: (i, seg(k))),
                     pl.BlockSpec((ks, tn), lambda i, k, j: (seg(k), col(k, j)))]
        args += [a, w]
    in_specs += [pl.BlockSpec((tm, tn), lambda i, k, j: (i, jnp.where(k == 0, j, n_j - 1))),
                 pl.BlockSpec((1, tn), lambda i, k, j: (0, jnp.where(k == n_kseg, j, 0))),
                 pl.BlockSpec((1, tn), lambda i, k, j: (0, jnp.where(k == n_kseg, j, 0)))]
    out_dtypes = [f32, bf16] if emit_bf16 else [f32]
    return pl.pallas_call(
        functools.partial(_resid_ln_kernel, n_pair=len(pairs), alpha=alpha, n_j=n_j, n_kseg=n_kseg, tn=tn,
                          emit_bf16=emit_bf16),
        grid=(M // tm, n_kseg + 1, n_j),
        in_specs=in_specs,
        out_specs=[pl.BlockSpec((tm, tn), lambda i, k, j: (i, jnp.where(k == n_kseg, j, 0))) for _ in out_dtypes],
        out_shape=[jax.ShapeDtypeStruct((M, N), dt) for dt in out_dtypes],
        scratch_shapes=[pltpu.VMEM((n_j, tm, tn), f32), pltpu.VMEM((tm, 1), f32), pltpu.VMEM((tm, 1), f32)],
        compiler_params=_cparams(("parallel", "arbitrary", "arbitrary")),
        name="matmul_resid_layernorm",
    )(*args, resid, gamma, beta)


def _pool_kernel(hist_ref, u_ref, wp_ref, sc_ref, o_ref, buf, *, tm, pos0):
    i = pl.program_id(1)
    H = POOL_MAX

    @pl.when(i == 0)
    def _():
        buf[0:H, :] = hist_ref[...]

    @pl.when(i > 0)
    def _():
        buf[0:H, :] = buf[tm:tm + H, :]

    buf[H:H + tm, :] = u_ref[...]
    pg = wp_ref.shape[1]
    pos = pos0 + i * tm + lax.broadcasted_iota(i32, (tm, 1), 0)
    for g, w in enumerate(POOL_WINDOWS):
        cols = slice(g * pg, (g + 1) * pg)
        cur = buf[H:H + tm, cols]
        tot = cur
        for k in range(1, w):
            tot = tot + buf[H - k:H - k + tm, cols]
        cnt = jnp.minimum(pos + 1, w).astype(f32)
        d = tot / cnt - cur
        y = _dot(d.astype(bf16), wp_ref[g]) * sc_ref[:, cols]
        o_ref[:, cols] = y.astype(o_ref.dtype)


def _pool_mix(hist, u, w_pool, scale, *, nb, n, pos0, tm, out_dtype):
    W = u.shape[1]
    nt = n // tm
    return pl.pallas_call(
        functools.partial(_pool_kernel, tm=tm, pos0=pos0),
        grid=(nb, nt),
        in_specs=[pl.BlockSpec((None, POOL_MAX, W), lambda b, i: (b, 0, 0)),
                  pl.BlockSpec((tm, W), lambda b, i: (b * nt + i, 0)),
                  pl.BlockSpec(w_pool.shape, lambda b, i: (0, 0, 0)),
                  pl.BlockSpec((1, W), lambda b, i: (0, 0))],
        out_specs=pl.BlockSpec((tm, W), lambda b, i: (b * nt + i, 0)),
        out_shape=jax.ShapeDtypeStruct((nb * n, W), out_dtype),
        scratch_shapes=[pltpu.VMEM((POOL_MAX + tm, W), f32)],
        compiler_params=_cparams(("parallel", "arbitrary")),
        name="pool_mix",
    )(hist, u, w_pool, scale)


def _cmp_part_kernel(pt_ref, *refs, n_pg, G):
    pages = refs[:n_pg]
    w_ref, o_ref, y_scr = refs[n_pg:]
    cpp = pages[0].shape[0]
    tok_per_tile = SUBLANES // G
    n_tile = n_pg * cpp
    rows = n_tile * SUBLANES
    width = w_ref.shape[2]
    row_in_tile = lax.broadcasted_iota(i32, (rows, HEAD_DIM), 0) & (SUBLANES - 1)
    acc = None
    for pp in range(CMP_STRIDE // tok_per_tile):
        x = jnp.concatenate([pages[pg][:, pp * SUBLANES:(pp + 1) * SUBLANES, :] for pg in range(n_pg)], axis=0)
        x = x.reshape(rows, HEAD_DIM)
        routed = [jnp.where((row_in_tile >= t * G) & (row_in_tile < (t + 1) * G), x, 0.0)
                  for t in range(tok_per_tile)]
        d = _dot(jnp.concatenate(routed, axis=1).astype(bf16), w_ref[pp])
        acc = d if acc is None else acc + d
    y = acc
    for t in range(1, tok_per_tile):
        y = y + pltpu.roll(acc, rows - t * G, axis=0)
    y_scr[...] = y.reshape(n_tile, SUBLANES, width)
    for g in range(G):
        o_ref[g] = y_scr[:, g, :]


def _cmp_part(pages, page_table, w1cat, *, G, n_pg):
    nb, n_pages = page_table.shape
    assert SUBLANES % G == 0
    chunk_rows = CMP_STRIDE * G
    cpp = pages.shape[1] // chunk_rows
    n_chunk = n_pages * cpp
    n_tile = n_pg * cpp
    width = w1cat.shape[1]
    tok_per_tile = SUBLANES // G
    pages = pages.reshape(pages.shape[0], cpp, chunk_rows, HEAD_DIM)
    w_tiles = w1cat.reshape(CMP_STRIDE // tok_per_tile, tok_per_tile * HEAD_DIM, width)

    def page_spec(pg):
        return pl.BlockSpec((None, cpp, chunk_rows, HEAD_DIM), lambda b, j, pt: (pt[b, j * n_pg + pg], 0, 0, 0))

    grid_spec = pltpu.PrefetchScalarGridSpec(
        num_scalar_prefetch=1,
        grid=(nb, n_pages // n_pg),
        in_specs=[page_spec(pg) for pg in range(n_pg)]
        + [pl.BlockSpec(w_tiles.shape, lambda b, j, pt: (0, 0, 0))],
        out_specs=pl.BlockSpec((None, G, n_tile, width), lambda b, j, pt: (b, 0, j, 0)),
        scratch_shapes=[pltpu.VMEM((n_tile, SUBLANES, width), f32)],
    )
    return pl.pallas_call(
        functools.partial(_cmp_part_kernel, n_pg=n_pg, G=G),
        grid_spec=grid_spec,
        out_shape=jax.ShapeDtypeStruct((nb, G, n_chunk, width), f32),
        compiler_params=_cparams(("parallel", "arbitrary")),
        name="cmp_part",
    )(page_table, *([pages] * n_pg), w_tiles)


def _gelu_tanh(x):
    return 0.5 * x * (1.0 + jnp.tanh(math.sqrt(2.0 / math.pi) * (x + 0.044715 * (x * x * x))))


def _cmp_finish_kernel(part_ref, pe_ref, w1_ref, w2_ref, cs_ref, o_ref, *, rope, transposed):
    n_chunk = part_ref.shape[0]
    hid0 = _dot(pe_ref[...].astype(bf16), w1_ref[...])[0:1]
    a = part_ref[:, :HEAD_DIM]
    b = pltpu.roll(part_ref[:, HEAD_DIM:], n_chunk - 1, axis=0)
    y = _dot(_gelu_tanh(hid0 + a + b).astype(bf16), w2_ref[...])
    if rope:
        y = _rope_tile(y, cs_ref[:, :HEAD_DIM], cs_ref[:, HEAD_DIM:])
    o_ref[...] = (y.T if transposed else y).astype(o_ref.dtype)


def _cmp_finish(part, pe, w1flat, w2, cs_cmp, *, rope, transposed):
    nb, G, n_chunk, _ = part.shape
    out_dims = (HEAD_DIM, n_chunk) if transposed else (n_chunk, HEAD_DIM)
    return pl.pallas_call(
        functools.partial(_cmp_finish_kernel, rope=rope, transposed=transposed),
        grid=(nb, G),
        in_specs=[pl.BlockSpec((None, None, n_chunk, CMP_RATIO * HEAD_DIM), lambda b, g: (b, g, 0, 0)),
                  pl.BlockSpec(pe.shape, lambda b, g: (0, 0)),
                  pl.BlockSpec(w1flat.shape, lambda b, g: (0, 0)),
                  pl.BlockSpec(w2.shape, lambda b, g: (0, 0)),
                  pl.BlockSpec(cs_cmp.shape, lambda b, g: (0, 0))],
        out_specs=pl.BlockSpec((None, None) + out_dims, lambda b, g: (b, g, 0, 0)),
        out_shape=jax.ShapeDtypeStruct((nb, G) + out_dims, bf16),
        compiler_params=_cparams(("parallel", "parallel")),
        name="cmp_finish",
    )(part, pe, w1flat, w2, cs_cmp)


def _softmax_terms(s_biased, floor=None):
    m = jnp.max(s_biased, axis=-1, keepdims=True)
    if floor is not None:
        m = jnp.maximum(m, floor)
    e = jnp.exp(s_biased - m)
    return e, 1.0 / jnp.maximum(jnp.sum(e, axis=-1, keepdims=True), 1e-30)


def _dot_split3(p, w01):
    hi = p.astype(bf16)
    r1 = p - hi.astype(f32)
    mid = r1.astype(bf16)
    lo = (r1 - mid.astype(f32)).astype(bf16)
    return _dot(hi, w01) + _dot(mid, w01) + _dot(lo, w01)


def _overlap01(n_chunk, n_cmp, ns):
    n = lax.broadcasted_iota(i32, (n_chunk, ns), 0)
    s = lax.broadcasted_iota(i32, (n_chunk, ns), 1)
    ov = ((n * CMP_STRIDE < s * SLC_LEN + SLC_LEN) & (n * CMP_STRIDE + CMP_LEN - 1 >= s * SLC_LEN)
          & (n < n_cmp))
    return jnp.where(ov, 1.0, 0.0).astype(bf16)


def _cmp_bias(q_pos, n_chunk, n_cmp):
    n_io = lax.broadcasted_iota(i32, (q_pos.shape[0], n_chunk), 1)
    return jnp.where((n_io * CMP_STRIDE + CMP_LEN - 1 <= q_pos) & (n_io < n_cmp), 0.0, NEG)


def _selection_values(imp, s_io, cur):
    forced = (s_io == 0) | (s_io == cur) | (s_io == cur - 1)
    return jnp.where(forced, jnp.inf, jnp.where(s_io <= cur, imp, -jnp.inf))


def _ahead(other, val, other_first):
    return (other > val) | ((other == val) & other_first)


def _online_step(s, v, m_old, l_old, acc_old):
    m_new = jnp.maximum(m_old, jnp.max(s, axis=-1, keepdims=True))
    alpha = jnp.exp(m_old - m_new)
    e = jnp.exp(s - m_new)
    l_new = alpha * l_old + jnp.sum(e, axis=-1, keepdims=True)
    acc_new = alpha * acc_old + _dot(e.astype(bf16), v)
    return m_new, l_new, acc_new


def _prompt_attn_kernel(qt_ref, gtt_ref, kcb_ref, vcbt_ref, ks_ref, vst_ref, kw_ref, vwt_ref, oh_ref, o_ref,
                        kaug, qaug, val_scr, cnt_scr, acc_scr, *, R, n_cmp, n_slc, kc):
    c = pl.program_id(2)
    tq = Q_TILE
    nq = R * tq
    n_chunk = kcb_ref.shape[0]
    ns = oh_ref.shape[1]
    nr = val_scr.shape[0]
    win_keys = WINDOW + tq

    @pl.when(c == 0)
    def _():
        kaug[:, :HEAD_DIM] = ks_ref[...]
        kaug[:, HEAD_DIM:] = oh_ref[...]

    def all_heads(a):
        return jnp.concatenate([a] * R, axis=1)

    def softmax_down(s, floor=None):
        m = jnp.max(s, axis=0, keepdims=True)
        if floor is not None:
            m = jnp.maximum(m, floor)
        e = jnp.exp(s - m)
        return e, 1.0 / jnp.maximum(jnp.sum(e, axis=0, keepdims=True), 1e-30)

    for r in range(R):
        qaug[0:HEAD_DIM, r * tq:(r + 1) * tq] = qt_ref[r * HEAD_DIM:(r + 1) * HEAD_DIM, :]
    q_t = qaug[0:HEAD_DIM, :]
    q_pos = c * tq + lax.broadcasted_iota(i32, (1, tq), 1)

    n_io = lax.broadcasted_iota(i32, (n_chunk, tq), 0)
    bias_c = jnp.where((n_io * CMP_STRIDE + CMP_LEN - 1 <= q_pos) & (n_io < n_cmp), 0.0, NEG)
    e, inv = softmax_down(_dot(kcb_ref[...], q_t) + all_heads(bias_c), floor=0.5 * NEG)
    p_c = e * inv
    o_c = _dot(vcbt_ref[...], p_c.astype(bf16))
    p_sum = p_c[:, 0:tq]
    for r in range(1, R):
        p_sum = p_sum + p_c[:, r * tq:(r + 1) * tq]

    blk = lax.broadcasted_iota(i32, (nr, n_chunk), 0)
    n_col = lax.broadcasted_iota(i32, (nr, n_chunk), 1)
    ov = jnp.where((n_col * CMP_STRIDE < blk * SLC_LEN + SLC_LEN)
                   & (n_col * CMP_STRIDE + CMP_LEN - 1 >= blk * SLC_LEN) & (n_col < n_cmp), 1.0, 0.0).astype(bf16)
    hi = p_sum.astype(bf16)
    r1 = p_sum - hi.astype(f32)
    mid = r1.astype(bf16)
    lo = (r1 - mid.astype(f32)).astype(bf16)
    imp_t = _dot(ov, hi) + _dot(ov, mid) + _dot(ov, lo)
    s_io = lax.broadcasted_iota(i32, (nr, tq), 0)
    cur = _div_pow2(c * tq + lax.broadcasted_iota(i32, (nr, tq), 1), SLC_LEN)
    val_scr[...] = _selection_values(imp_t, s_io, cur)
    cnt_scr[...] = jnp.zeros(cnt_scr.shape, f32)
    n_live = _div_pow2(c * tq + tq - 1, SLC_LEN) + 1
    for grp in range(nr // SUBLANES):
        @pl.when(grp * SUBLANES < n_live)
        def _():
            val = val_scr[...]
            cnt = cnt_scr[...]
            for sp in range(grp * SUBLANES, min((grp + 1) * SUBLANES, n_slc)):
                cnt = cnt + jnp.where(_ahead(val_scr[sp:sp + 1, :], val, s_io > sp), 1.0, 0.0)
            cnt_scr[...] = cnt

    first_blk = _div_pow2(c * tq, SLC_LEN)
    keep = (cnt_scr[...] < N_SEL) & (val_scr[...] > -jnp.inf) & (s_io < first_blk)
    mask_t = jnp.where(keep, 0.0, NEG)
    if nr < ns:
        mask_t = jnp.concatenate([mask_t, jnp.full((ns - nr, tq), NEG, f32)], axis=0)
    qaug[HEAD_DIM:, :] = all_heads(mask_t.astype(bf16))

    d0 = pl.multiple_of(c * tq, tq)
    tri = jnp.where(lax.broadcasted_iota(i32, (tq, tq), 0) <= lax.broadcasted_iota(i32, (tq, tq), 1), 0.0, NEG)
    s_d = _dot(ks_ref[pl.ds(d0, tq), :], q_t) + all_heads(tri)
    m_d = jnp.max(s_d, axis=0, keepdims=True)
    e_d = jnp.exp(s_d - m_d)
    l_d = jnp.sum(e_d, axis=0, keepdims=True)
    acc_scr[...] = _dot(vst_ref[c], e_d.astype(bf16))

    tiles_per_chunk = kc // tq

    def sweep(j, carry):
        m_old, l_old = carry
        k0 = pl.multiple_of(j * kc, kc)
        s = _dot(kaug[pl.ds(k0, kc), :], qaug[...])
        m_new = jnp.maximum(m_old, jnp.max(s, axis=0, keepdims=True))
        alpha = jnp.exp(m_old - m_new)
        e = jnp.exp(s - m_new)
        v_t = jnp.concatenate([vst_ref[j * tiles_per_chunk + i] for i in range(tiles_per_chunk)], axis=1)
        acc_scr[...] = alpha * acc_scr[...] + _dot(v_t, e.astype(bf16))
        return m_new, alpha * l_old + jnp.sum(e, axis=0, keepdims=True)

    _, l_s = lax.fori_loop(0, (c * tq + kc - 1) // kc, sweep, (m_d, l_d))
    o_s = acc_scr[...] * (1.0 / jnp.maximum(l_s, 1e-30))

    w_tile = jnp.maximum(c - WINDOW // tq, 0)
    w0 = pl.multiple_of(w_tile * tq, tq)
    key_w = w0 + lax.broadcasted_iota(i32, (win_keys, tq), 0)
    bias_w = jnp.where((key_w <= q_pos) & (q_pos - key_w < WINDOW), 0.0, NEG)
    e, inv = softmax_down(_dot(kw_ref[pl.ds(w0, win_keys), :], q_t) + all_heads(bias_w))
    vw_t = jnp.concatenate([vwt_ref[w_tile + i] for i in range(win_keys // tq)], axis=1)
    o_w = _dot(vw_t, e.astype(bf16)) * inv

    for r in range(R):
        cols = slice(r * tq, (r + 1) * tq)
        g = [gtt_ref[r * N_BRANCH + br:r * N_BRANCH + br + 1, :] for br in range(N_BRANCH)]
        o_t = g[0] * o_c[:, cols] + g[1] * o_s[:, cols] + g[2] * o_w[:, cols]
        o_ref[:, r * HEAD_DIM:(r + 1) * HEAD_DIM] = o_t.T.astype(o_ref.dtype)


def _prompt_attn(q_t, gates_t, kcb, vcb_t, ks, vs_t, kw, vw_t, onehot, *, B, T, G, R):
    n_chunk = kcb.shape[2]
    n_cmp = n_chunk - CMP_RATIO + 1
    n_slc = pl.cdiv(T, SLC_LEN)
    ns = onehot.shape[1]
    nr = _round_up(n_slc, SUBLANES)
    kc = _pick_tile(T, 1024, Q_TILE)
    nt = T // Q_TILE
    assert T % Q_TILE == 0 and T >= WINDOW + Q_TILE and Q_TILE % SLC_LEN == 0 and Q_TILE <= 2 * SLC_LEN
    seq_spec = pl.BlockSpec((T, HEAD_DIM), lambda b, g, c: (b, g))
    tile_spec = pl.BlockSpec((nt, HEAD_DIM, Q_TILE), lambda b, g, c: (b, g, 0))
    return pl.pallas_call(
        functools.partial(_prompt_attn_kernel, R=R, n_cmp=n_cmp, n_slc=n_slc, kc=kc),
        grid=(B, G, nt),
        in_specs=[pl.BlockSpec((R * HEAD_DIM, Q_TILE), lambda b, g, c: (g, b * nt + c)),
                  pl.BlockSpec((LANES, Q_TILE), lambda b, g, c: (g, b * nt + c)),
                  pl.BlockSpec((None, None, n_chunk, HEAD_DIM), lambda b, g, c: (b, g, 0, 0)),
                  pl.BlockSpec((None, None, HEAD_DIM, n_chunk), lambda b, g, c: (b, g, 0, 0)),
                  seq_spec, tile_spec, seq_spec, tile_spec,
                  pl.BlockSpec((T, ns), lambda b, g, c: (0, 0))],
        out_specs=pl.BlockSpec((Q_TILE, R * HEAD_DIM), lambda b, g, c: (b * nt + c, g)),
        out_shape=jax.ShapeDtypeStruct((B * T, G * R * HEAD_DIM), bf16),
        scratch_shapes=[pltpu.VMEM((T, HEAD_DIM + ns), bf16),
                        pltpu.VMEM((HEAD_DIM + ns, R * Q_TILE), bf16),
                        pltpu.VMEM((nr, Q_TILE), f32), pltpu.VMEM((nr, Q_TILE), f32),
                        pltpu.VMEM((HEAD_DIM, R * Q_TILE), f32)],
        compiler_params=_cparams(("parallel", "parallel", "arbitrary")),
        name="nsa_prompt",
    )(q_t, gates_t, kcb, vcb_t, ks, vs_t, kw, vw_t, onehot)


def _sample_cmp_win_kernel(q_ref, gt_ref, kcb_ref, vcb_ref, kwin_ref, vwin_ref, kwn_ref, vwn_ref,
                           part_ref, qaug_ref, kbuf, vbuf, *, G, R, n_cmp, n_slc, past):
    n_chunk = kcb_ref.shape[1]
    ns = qaug_ref.shape[1] - G * HEAD_DIM
    wb = kwin_ref.shape[0] // G
    sp = SUBLANES
    rows = R * sp
    nbuf = kbuf.shape[0]
    q_pos = past + (lax.broadcasted_iota(i32, (rows, 1), 0) & (sp - 1))
    bias_c = _cmp_bias(q_pos, n_chunk, n_cmp)
    key_w = past - wb + lax.broadcasted_iota(i32, (rows, nbuf), 1)
    bias_w = jnp.where((key_w <= q_pos) & (q_pos - key_w < WINDOW) & (key_w >= 0), 0.0, NEG)
    ov = _overlap01(n_chunk, n_cmp, ns)
    s_io = lax.broadcasted_iota(i32, (sp, ns), 1)
    cur = _div_pow2(q_pos[0:sp], SLC_LEN)
    qaug_ref[:, :G * HEAD_DIM] = jnp.zeros((G * rows, G * HEAD_DIM), bf16)

    for g in range(G):
        q = q_ref[g]
        cols = slice(g * HEAD_DIM, (g + 1) * HEAD_DIM)

        e, inv = _softmax_terms(_dot_nt(q, kcb_ref[g]) + bias_c, floor=0.5 * NEG)
        p_c = e * inv
        o_c = _dot(p_c.astype(bf16), vcb_ref[g])
        p_sum = p_c[0:sp]
        for r in range(1, R):
            p_sum = p_sum + p_c[r * sp:(r + 1) * sp]
        val = _selection_values(_dot_split3(p_sum, ov), s_io, cur)
        cnt = jnp.zeros((sp, ns), f32)
        for blk in range(n_slc):
            cnt = cnt + jnp.where(_ahead(val[:, blk:blk + 1], val, s_io > blk), 1.0, 0.0)
        mask = jnp.where((cnt < N_SEL) & (val > -jnp.inf), 0.0, NEG).astype(bf16)

        qaug_ref[g * rows:(g + 1) * rows, cols] = q
        qaug_ref[g * rows:(g + 1) * rows, G * HEAD_DIM:] = jnp.concatenate([mask] * R, axis=0)

        kbuf[0:wb, :] = kwin_ref[pl.ds(g, wb, stride=G), :].astype(bf16)
        kbuf[wb:nbuf, :] = kwn_ref[:, cols]
        vbuf[0:wb, :] = vwin_ref[pl.ds(g, wb, stride=G), :].astype(bf16)
        vbuf[wb:nbuf, :] = vwn_ref[:, cols]
        e, inv = _softmax_terms(_dot_nt(q, kbuf[...]) + bias_w)
        o_w = _dot(e.astype(bf16), vbuf[...]) * inv
        part_ref[g] = gt_ref[g][:, 0:1] * o_c + gt_ref[g][:, 2:3] * o_w


def _sample_cmp_win(q, gates, kcb, vcb, kwin, vwin, kw_new, vw_new, *, past, n_slc, ns):
    DB, G, rows, _ = q.shape
    R = rows // SUBLANES
    n_chunk = kcb.shape[2]
    n_cmp = n_chunk - CMP_RATIO + 1
    wbg = kwin.shape[1]
    npad = kw_new.shape[1]
    qspec = pl.BlockSpec((None, G, rows, HEAD_DIM), lambda b: (b, 0, 0, 0))
    cspec = pl.BlockSpec((None, G, n_chunk, HEAD_DIM), lambda b: (b, 0, 0, 0))
    wspec = pl.BlockSpec((None, wbg, HEAD_DIM), lambda b: (b, 0, 0))
    nspec = pl.BlockSpec((None, npad, G * HEAD_DIM), lambda b: (b, 0, 0))
    aug_w = G * HEAD_DIM + ns
    return pl.pallas_call(
        functools.partial(_sample_cmp_win_kernel, G=G, R=R, n_cmp=n_cmp, n_slc=n_slc, past=past),
        grid=(DB,),
        in_specs=[qspec, qspec, cspec, cspec, wspec, wspec, nspec, nspec],
        out_specs=[qspec, pl.BlockSpec((None, G * rows, aug_w), lambda b: (b, 0, 0))],
        out_shape=[jax.ShapeDtypeStruct((DB, G, rows, HEAD_DIM), f32),
                   jax.ShapeDtypeStruct((DB, G * rows, aug_w), bf16)],
        scratch_shapes=[pltpu.VMEM((wbg // G + npad, HEAD_DIM), bf16)] * 2,
        compiler_params=_cparams(("parallel",)),
        name="nsa_sample_cmp_win",
    )(q, gates, kcb, vcb, kwin, vwin, kw_new, vw_new)


def _sample_sel_kernel(pt_ref, *refs, n_pg, G, R, page):
    kpages = refs[:n_pg]
    vpages = refs[n_pg:2 * n_pg]
    (oh_ref, ohn_ref, qaug_ref, gt_ref, part_ref, ksn_ref, vsn_ref, o_ref,
     kaug, vall, m_scr, l_scr, acc_scr) = refs[2 * n_pg:]
    j = pl.program_id(1)
    sp = SUBLANES
    rows = R * sp
    kvw = G * HEAD_DIM

    @pl.when(j == 0)
    def _():
        m_scr[...] = jnp.full(m_scr.shape, NEG, f32)
        l_scr[...] = jnp.zeros(l_scr.shape, f32)
        acc_scr[...] = jnp.zeros(acc_scr.shape, f32)

    def update(n_keys, bias):
        s = _dot_nt(qaug_ref[...], kaug[0:n_keys, :])
        if bias is not None:
            s = s + bias
        m, l, acc = _online_step(s, vall[0:n_keys, :], m_scr[...], l_scr[...], acc_scr[...])
        m_scr[...] = m
        l_scr[...] = l
        acc_scr[...] = acc

    for pg in range(n_pg):
        for g in range(G):
            kaug[pg * page:(pg + 1) * page, g * HEAD_DIM:(g + 1) * HEAD_DIM] = (
                kpages[pg][pl.ds(g, page, stride=G), :].astype(bf16))
            vall[pg * page:(pg + 1) * page, g * HEAD_DIM:(g + 1) * HEAD_DIM] = (
                vpages[pg][pl.ds(g, page, stride=G), :].astype(bf16))
    kaug[:, kvw:] = oh_ref[...]
    update(n_pg * page, None)

    @pl.when(j == pl.num_programs(1) - 1)
    def _():
        kaug[0:page, :kvw] = ksn_ref[...]
        kaug[0:page, kvw:] = ohn_ref[...]
        vall[0:page, :] = vsn_ref[...]
        step = lax.broadcasted_iota(i32, (G * rows, page), 0) & (sp - 1)
        causal = jnp.where(lax.broadcasted_iota(i32, (G * rows, page), 1) <= step, 0.0, NEG)
        update(page, causal)
        inv = 1.0 / jnp.maximum(l_scr[...], 1e-30)
        for g in range(G):
            rs = slice(g * rows, (g + 1) * rows)
            o_s = acc_scr[rs, g * HEAD_DIM:(g + 1) * HEAD_DIM] * inv[rs]
            o_ref[g] = part_ref[g] + gt_ref[g][:, 1:2] * o_s


def _sample_sel(kcache, vcache, page_table, onehot, qaug, gates, part, ks_new, vs_new, *, G, n_pg):
    DB, _, rows, _ = part.shape
    R = rows // SUBLANES
    n_pages = page_table.shape[1]
    page = kcache.shape[1] // G
    ns = onehot.shape[1]
    aug_w = G * HEAD_DIM + ns
    n_keys = n_pg * page

    def page_spec(pg):
        return pl.BlockSpec((None, page * G, HEAD_DIM), lambda b, j, pt: (pt[b, j * n_pg + pg], 0, 0))

    pspec = pl.BlockSpec((None, G, rows, HEAD_DIM), lambda b, j, pt: (b, 0, 0, 0))
    nspec = pl.BlockSpec((None, page, G * HEAD_DIM), lambda b, j, pt: (b, 0, 0))
    grid_spec = pltpu.PrefetchScalarGridSpec(
        num_scalar_prefetch=1,
        grid=(DB, n_pages // n_pg),
        in_specs=[page_spec(pg) for pg in range(n_pg)] * 2
        + [pl.BlockSpec((n_keys, ns), lambda b, j, pt: (j, 0)),
           pl.BlockSpec((page, ns), lambda b, j, pt: (n_pages, 0)),
           pl.BlockSpec((None, G * rows, aug_w), lambda b, j, pt: (b, 0, 0)),
           pspec, pspec, nspec, nspec],
        out_specs=pspec,
        scratch_shapes=[pltpu.VMEM((n_keys, aug_w), bf16), pltpu.VMEM((n_keys, G * HEAD_DIM), bf16),
                        pltpu.VMEM((G * rows, 1), f32), pltpu.VMEM((G * rows, 1), f32),
                        pltpu.VMEM((G * rows, G * HEAD_DIM), f32)],
    )
    return pl.pallas_call(
        functools.partial(_sample_sel_kernel, n_pg=n_pg, G=G, R=R, page=page),
        grid_spec=grid_spec,
        out_shape=jax.ShapeDtypeStruct(part.shape, f32),
        compiler_params=_cparams(("parallel", "arbitrary")),
        name="nsa_sample_sel",
    )(page_table, *([kcache] * n_pg), *([vcache] * n_pg), onehot, onehot, qaug, gates, part, ks_new, vs_new)


def _rope_table(pos):
    half = HEAD_DIM // 2
    inv = ROPE_THETA ** (-jnp.arange(half, dtype=f32) / half)
    ang = pos.astype(f32)[:, None] * inv[None, :]
    cos, sin = jnp.cos(ang), jnp.sin(ang)
    return jnp.concatenate([cos, cos, -sin, sin], axis=-1)


def _block_onehot(n_keys, ns):
    blk = jnp.arange(n_keys, dtype=i32)[:, None] // SLC_LEN
    return (blk == jnp.arange(ns, dtype=i32)[None, :]).astype(bf16)


def kernel(x_prompt, x_sample, cache_k_cmp, cache_v_cmp, cache_k_slc, cache_v_slc, state_k_win, state_v_win, state_pool, page_table, w_in, w_cmp1_k, pe_cmp_k, w_cmp2_k, w_cmp1_v, pe_cmp_v, w_cmp2_v, w_pool, pool_scale, w_o, ln1_g, ln1_b, w_gate, w_up, w_down, ln2_g, ln2_b):
    B, T, D = x_prompt.shape
    DB, S, _ = x_sample.shape
    depth, n_phys, page, G, _ = cache_k_cmp.shape
    assert depth == 1 and S <= SUBLANES
    n_pages = page_table.shape[1]
    past = n_pages * page
    PW = state_pool.shape[3]
    NW = D - PW
    NH = NW // HEAD_DIM
    R = NH // G
    KVW = G * HEAD_DIM
    F = w_gate.shape[2]
    wb = state_k_win.shape[2]
    wbp = min(WINDOW, T)
    alpha = (2 * depth) ** 0.25
    MP = B * T
    MS = DB * S
    M = MP + MS
    assert T % page == 0 and page % CMP_STRIDE == 0 and page % SLC_LEN == 0 and past % SLC_LEN == 0

    x_all = jnp.concatenate([x_prompt.reshape(MP, D), x_sample.reshape(MS, D)], axis=0)
    x_bf = x_all.astype(bf16)
    w_in0 = w_in[0]
    o_kv = PW + NW
    o_g = o_kv + N_KV_SEG * KVW
    w_u = w_in0[:, :PW].astype(bf16)
    w_q = w_in0[:, PW:o_kv].astype(bf16)
    w_kv = w_in0[:, o_kv:o_g].astype(bf16)
    w_g = w_in0[:, o_g:o_g + N_BRANCH * NH].reshape(D, G, R * N_BRANCH)
    w_g = jnp.pad(w_g, ((0, 0), (0, 0), (0, LANES - R * N_BRANCH))).reshape(D, G * LANES).astype(bf16)

    pos_all = jnp.concatenate([jnp.tile(jnp.arange(T, dtype=i32), B),
                               jnp.tile(past + jnp.arange(S, dtype=i32), DB)])
    cs_all = _rope_table(pos_all)

    tm = _pick_tile(M, 640, SUBLANES)
    u = _proj(x_bf, w_u, cs_all, mode="none", tm=tm, tn=_pick_tile(PW, 512, LANES))
    q = _proj(x_bf, w_q, cs_all, mode="rope", scale=HEAD_DIM ** -0.5, out_dtype=bf16, tm=tm,
              tn=_pick_tile(NW, 512, LANES))
    gates = _proj(x_bf, w_g, cs_all, mode="sigmoid", tm=tm, tn=_pick_tile(G * LANES, 512, LANES))
    kv_st, kv_bf = _proj_kv(x_bf, w_kv, cs_all, G=G, tm=tm)
    kc_st, vc_st = kv_st[0], kv_st[1]
    ks_bf, vs_bf, kw_bf, vw_bf = kv_bf[2], kv_bf[3], kv_bf[4], kv_bf[5]

    wp_bf = w_pool[0].astype(bf16)
    pscale = pool_scale[0].reshape(1, PW)
    pool_p = _pool_mix(jnp.zeros((B, POOL_MAX, PW), f32), u, wp_bf, pscale, nb=B, n=T, pos0=0,
                       tm=_pick_tile(T, 512, SUBLANES), out_dtype=bf16)
    u_s = u[MP:].reshape(DB, S, PW)
    hist_s = jnp.pad(state_pool[0], ((0, 0), (POOL_MAX - state_pool.shape[2], 0), (0, 0)))
    u_s_pad = jnp.pad(u_s, ((0, 0), (0, SUBLANES - S), (0, 0))).reshape(DB * SUBLANES, PW)
    pool_s = _pool_mix(hist_s, u_s_pad, wp_bf, pscale, nb=DB, n=SUBLANES, pos0=past, tm=SUBLANES,
                       out_dtype=f32).reshape(DB, SUBLANES, PW)[:, :S]

    def cmp_weights(w1, pe, w2):
        w1cat = w1[0].reshape(CMP_RATIO, CMP_STRIDE, HEAD_DIM, HEAD_DIM).transpose(1, 2, 0, 3)
        w1cat = w1cat.reshape(CMP_STRIDE * HEAD_DIM, CMP_RATIO * HEAD_DIM).astype(bf16)
        pe8 = jnp.broadcast_to(pe[0].reshape(1, CMP_LEN * HEAD_DIM), (SUBLANES, CMP_LEN * HEAD_DIM))
        return w1cat, pe8, w1[0].reshape(CMP_LEN * HEAD_DIM, HEAD_DIM).astype(bf16), w2[0].astype(bf16)

    cw_k = cmp_weights(w_cmp1_k, pe_cmp_k, w_cmp2_k)
    cw_v = cmp_weights(w_cmp1_v, pe_cmp_v, w_cmp2_v)

    def compress(pages, table, cw, rope, transposed=False):
        n_chunk = table.shape[1] * (page // CMP_STRIDE)
        cs_cmp = _rope_table(jnp.arange(n_chunk, dtype=i32) * CMP_STRIDE + CMP_LEN - 1)
        part = _cmp_part(pages, table, cw[0], G=G, n_pg=_pick_tile(table.shape[1], 8, 1))
        return _cmp_finish(part, cw[1], cw[2], cw[3], cs_cmp, rope=rope, transposed=transposed)

    def as_pages(a):
        return a.reshape(-1, page * G, HEAD_DIM)

    table_p = jnp.arange(MP // page, dtype=i32).reshape(B, T // page)
    kcb_p = compress(as_pages(kc_st[:MP * G]), table_p, cw_k, True)
    vcb_p_t = compress(as_pages(vc_st[:MP * G]), table_p, cw_v, False, transposed=True)
    kcb_s = compress(as_pages(cache_k_cmp.reshape(-1, HEAD_DIM)), page_table, cw_k, True)
    vcb_s = compress(as_pages(cache_v_cmp.reshape(-1, HEAD_DIM)), page_table, cw_v, False)

    ns_p = _round_up(pl.cdiv(T, SLC_LEN), LANES)
    def value_tiles_t(a):
        return a[:MP].reshape(MP // Q_TILE, Q_TILE, KVW).transpose(0, 2, 1)

    nsa_p = _prompt_attn(q.T, gates.T, kcb_p, vcb_p_t, ks_bf, value_tiles_t(vs_bf), kw_bf, value_tiles_t(vw_bf),
                         _block_onehot(T, ns_p), B=B, T=T, G=G, R=R)

    def srow(a):
        return a[MP:].reshape(DB, S, a.shape[1])

    def to_rows(a):
        a = jnp.pad(a.transpose(0, 2, 3, 1, 4), ((0, 0), (0, 0), (0, 0), (0, SUBLANES - S), (0, 0)))
        return a.reshape(DB, G, R * SUBLANES, a.shape[-1])

    q_s = to_rows(srow(q).reshape(DB, S, G, R, HEAD_DIM))
    g_s = srow(gates).reshape(DB, S, G, LANES)[..., :R * N_BRANCH].reshape(DB, S, G, R, N_BRANCH)
    g_s = to_rows(jnp.pad(g_s, ((0, 0),) * 4 + ((0, LANES - N_BRANCH),)))

    def new_rows(a):
        return jnp.pad(srow(a), ((0, 0), (0, page - S), (0, 0)))

    n_slc_s = pl.cdiv(past + S, SLC_LEN)
    ns_s = _round_up(n_slc_s, LANES)
    part_s, qaug_s = _sample_cmp_win(q_s, g_s, kcb_s, vcb_s,
                                     state_k_win.reshape(DB, wb * G, HEAD_DIM),
                                     state_v_win.reshape(DB, wb * G, HEAD_DIM),
                                     new_rows(kw_bf), new_rows(vw_bf), past=past, n_slc=n_slc_s, ns=ns_s)
    o_s = _sample_sel(as_pages(cache_k_slc.reshape(-1, HEAD_DIM)), as_pages(cache_v_slc.reshape(-1, HEAD_DIM)),
                      page_table, _block_onehot(past + page, ns_s), qaug_s, g_s, part_s,
                      new_rows(ks_bf), new_rows(vs_bf), G=G, n_pg=_pick_tile(n_pages, 8, 1))
    nsa_s = o_s.reshape(DB, G, R, SUBLANES, HEAD_DIM)[:, :, :, :S].transpose(0, 3, 1, 2, 4)
    nsa_s = nsa_s.reshape(MS, NW).astype(bf16)

    pool_all = jnp.concatenate([pool_p, pool_s.reshape(MS, PW).astype(bf16)], axis=0)
    nsa_all = jnp.concatenate([nsa_p, nsa_s], axis=0)
    wo_bf = w_o[0].astype(bf16)
    h, h_bf = _resid_ln([(pool_all, wo_bf[:PW]), (nsa_all, wo_bf[PW:])], x_all,
                        ln1_g[0].reshape(1, D), ln1_b[0].reshape(1, D),
                        alpha=alpha, tm=tm, tn=_pick_tile(D, 512, LANES), n_kseg=1, emit_bf16=True)
    ff = _gateup(h_bf, w_gate[0].astype(bf16), w_up[0].astype(bf16), tm=tm, tf=_pick_tile(F, 512, LANES))
    n_kseg = 2 if F % (2 * LANES) == 0 else 1
    y, = _resid_ln([(ff, w_down[0].astype(bf16))], h, ln2_g[0].reshape(1, D), ln2_b[0].reshape(1, D),
                   alpha=alpha, tm=tm, tn=_pick_tile(D, 256, LANES), n_kseg=n_kseg, emit_bf16=False)

    def pstate(i):
        return kv_st[i, :MP * G].reshape(1, B, T, G, HEAD_DIM)

    def sstate(i):
        return kv_st[i, MP * G:].reshape(1, DB, S, G, HEAD_DIM)

    kw_p, vw_p = pstate(4)[:, :, T - wbp:], pstate(5)[:, :, T - wbp:]
    pool_state_p = u[:MP].reshape(1, B, T, PW)[:, :, T - (POOL_MAX - 1):]
    kw_s = jnp.concatenate([state_k_win, sstate(4)], axis=2)[:, :, S:]
    vw_s = jnp.concatenate([state_v_win, sstate(5)], axis=2)[:, :, S:]
    pool_state_s = jnp.concatenate([state_pool, u_s[None]], axis=2)[:, :, S:]
    return (y[:MP].reshape(B, T, D), y[MP:].reshape(DB, S, D),
            pstate(0), pstate(1), pstate(2), pstate(3), kw_p, vw_p, pool_state_p,
            sstate(0), sstate(1), sstate(2), sstate(3), kw_s, vw_s, pool_state_s)
```

```python
import functools
import math

import jax
import jax.numpy as jnp
from jax import lax
from jax.experimental import pallas as pl
from jax.experimental.pallas import tpu as pltpu

HEAD_DIM = 128
POOL_WINDOWS = (2, 4, 8, 16)
POOL_MAX = max(POOL_WINDOWS)
N_BRANCH = 3
N_KV_SEG = 6
ROPE_KV_SEGS = (2, 4)
CMP_LEN = 32
CMP_STRIDE = 16
CMP_RATIO = CMP_LEN // CMP_STRIDE
SLC_LEN = 64
N_SEL = 16
WINDOW = 512
Q_TILE = 128
ROPE_THETA = 10000.0
LN_EPS = 1e-5
LANES = 128
SUBLANES = 8
VMEM_LIMIT = 56 * 1024 * 1024
NEG = -1e30

f32 = jnp.float32
bf16 = jnp.bfloat16
i32 = jnp.int32


def _cparams(sem):
    return pltpu.CompilerParams(dimension_semantics=sem, vmem_limit_bytes=VMEM_LIMIT)


def _pick_tile(n, target, align):
    best = None
    for t in range(align, min(n, target) + 1, align):
        if n % t == 0:
            best = t
    if best is None:
        raise ValueError(f"no tile for {n} (target {target}, align {align})")
    return best


def _round_up(n, m):
    return m * pl.cdiv(n, m)


def _dot(a, b):
    return jnp.dot(a, b, preferred_element_type=f32)


def _dot_nt(a, b):
    return lax.dot_general(a, b, (((1,), (1,)), ((), ())), preferred_element_type=f32)


def _rope_tile(a, cosf, sinf):
    return a * cosf + pltpu.roll(a, HEAD_DIM // 2, axis=1) * sinf


def _div_pow2(x, d):
    assert d & (d - 1) == 0
    return lax.shift_right_arithmetic(x, d.bit_length() - 1)


def _sigmoid(x):
    return 1.0 / (1.0 + jnp.exp(-x))


def _cast_weights_once(w_refs, wbf_refs):
    @pl.when(pl.program_id(1) == 0)
    def _():
        for w_ref, wbf in zip(w_refs, wbf_refs):
            wbf[...] = w_ref[...].astype(bf16)


def _proj_kernel(x_ref, w_ref, cs_ref, o_ref, wbf, *, mode, scale):
    _cast_weights_once([w_ref], [wbf])
    acc = _dot(x_ref[...], wbf[...])
    if mode == "rope":
        cosf = cs_ref[:, :HEAD_DIM]
        sinf = cs_ref[:, HEAD_DIM:]
        for h in range(acc.shape[1] // HEAD_DIM):
            sl = slice(h * HEAD_DIM, (h + 1) * HEAD_DIM)
            o_ref[:, sl] = (_rope_tile(acc[:, sl], cosf, sinf) * scale).astype(o_ref.dtype)
    elif mode == "sigmoid":
        o_ref[...] = _sigmoid(acc).astype(o_ref.dtype)
    else:
        o_ref[...] = acc.astype(o_ref.dtype)


def _proj(x, w, cs, *, col0, n_col, mode, scale=1.0, out_dtype=f32, tm, tn):
    M, K = x.shape
    return pl.pallas_call(
        functools.partial(_proj_kernel, mode=mode, scale=scale),
        grid=(n_col, M // tm),
        in_specs=[pl.BlockSpec((tm, K), lambda j, i: (i, 0)),
                  pl.BlockSpec((K, tn), lambda j, i: (0, col0 + j)),
                  pl.BlockSpec((tm, 2 * HEAD_DIM), lambda j, i: (i, 0))],
        out_specs=pl.BlockSpec((tm, tn), lambda j, i: (i, j)),
        out_shape=jax.ShapeDtypeStruct((M, n_col * tn), out_dtype),
        scratch_shapes=[pltpu.VMEM((K, tn), bf16)],
        compiler_params=_cparams(("parallel", "arbitrary")),
        name=f"proj_{mode}",
    )(x, w, cs)


def _proj_kv_kernel(x_ref, w_ref, cs_ref, st_ref, bf_ref, wbf, *, G):
    j = pl.program_id(0)
    _cast_weights_once([w_ref], [wbf])
    acc = _dot(x_ref[...], wbf[...])
    tm = acc.shape[0]

    def emit(rope):
        for g in range(G):
            y = acc[:, g * HEAD_DIM:(g + 1) * HEAD_DIM]
            if rope:
                y = _rope_tile(y, cs_ref[:, :HEAD_DIM], cs_ref[:, HEAD_DIM:])
            st_ref[pl.ds(g, tm, stride=G), :] = y
            bf_ref[:, g * HEAD_DIM:(g + 1) * HEAD_DIM] = y.astype(bf16)

    is_rope = functools.reduce(jnp.logical_or, [j == s for s in ROPE_KV_SEGS])
    pl.when(is_rope)(lambda: emit(True))
    pl.when(jnp.logical_not(is_rope))(lambda: emit(False))


def _proj_kv(x, w, cs, *, col0, n_seg, G, tm):
    M, K = x.shape
    N = G * HEAD_DIM
    return pl.pallas_call(
        functools.partial(_proj_kv_kernel, G=G),
        grid=(n_seg, M // tm),
        in_specs=[pl.BlockSpec((tm, K), lambda j, i: (i, 0)),
                  pl.BlockSpec((K, N), lambda j, i: (0, col0 + j)),
                  pl.BlockSpec((tm, 2 * HEAD_DIM), lambda j, i: (i, 0))],
        out_specs=[pl.BlockSpec((None, tm * G, HEAD_DIM), lambda j, i: (j, i, 0)),
                   pl.BlockSpec((None, tm, N), lambda j, i: (j, i, 0))],
        out_shape=[jax.ShapeDtypeStruct((n_seg, M * G, HEAD_DIM), f32),
                   jax.ShapeDtypeStruct((n_seg, M, N), bf16)],
        scratch_shapes=[pltpu.VMEM((K, N), bf16)],
        compiler_params=_cparams(("parallel", "arbitrary")),
        name="proj_kv",
    )(x, w, cs)


def _gateup_kernel(x_ref, wg_ref, wu_ref, o_ref, wg_bf, wu_bf):
    _cast_weights_once([wg_ref, wu_ref], [wg_bf, wu_bf])
    x = x_ref[...]
    g = _dot(x, wg_bf[...])
    u = _dot(x, wu_bf[...])
    o_ref[...] = (g * _sigmoid(g) * u).astype(o_ref.dtype)


def _gateup(x, wg, wu, *, tm, tf):
    M, K = x.shape
    F = wg.shape[1]
    return pl.pallas_call(
        _gateup_kernel,
        grid=(F // tf, M // tm),
        in_specs=[pl.BlockSpec((tm, K), lambda j, i: (i, 0)),
                  pl.BlockSpec((K, tf), lambda j, i: (0, j)),
                  pl.BlockSpec((K, tf), lambda j, i: (0, j))],
        out_specs=pl.BlockSpec((tm, tf), lambda j, i: (i, j)),
        out_shape=jax.ShapeDtypeStruct((M, F), bf16),
        scratch_shapes=[pltpu.VMEM((K, tf), bf16)] * 2,
        compiler_params=_cparams(("parallel", "arbitrary")),
        name="ffn_gate_up",
    )(x, wg, wu)


def _resid_ln_kernel(*refs, n_pair, alpha, n_j, n_kseg, tn, emit_bf16):
    a_refs = refs[0:2 * n_pair:2]
    w_refs = refs[1:2 * n_pair:2]
    r_ref, g_ref, b_ref, o_ref = refs[2 * n_pair:2 * n_pair + 4]
    obf_ref = refs[2 * n_pair + 4] if emit_bf16 else None
    ybuf, mu_scr, rstd_scr = refs[-3:]
    k = pl.program_id(1)
    j = pl.program_id(2)

    def partial_product():
        acc = _dot(a_refs[0][...], w_refs[0][...])
        for a_ref, w_ref in zip(a_refs[1:], w_refs[1:]):
            acc = acc + _dot(a_ref[...], w_ref[...])
        return acc

    @pl.when(k == 0)
    def _():
        ybuf[j] = alpha * r_ref[...] + partial_product()

    if n_kseg > 1:
        @pl.when((k > 0) & (k < n_kseg))
        def _():
            ybuf[j] = ybuf[j] + partial_product()

    @pl.when(k == n_kseg)
    def _():
        @pl.when(j == 0)
        def _():
            n = n_j * tn
            tot = jnp.sum(ybuf[0], axis=-1, keepdims=True)
            for jj in range(1, n_j):
                tot = tot + jnp.sum(ybuf[jj], axis=-1, keepdims=True)
            mu = tot / n
            var = jnp.zeros_like(mu)
            for jj in range(n_j):
                d = ybuf[jj] - mu
                var = var + jnp.sum(d * d, axis=-1, keepdims=True)
            mu_scr[...] = mu
            rstd_scr[...] = lax.rsqrt(var / n + LN_EPS)

        out = (ybuf[j] - mu_scr[...]) * rstd_scr[...] * g_ref[...] + b_ref[...]
        o_ref[...] = out
        if emit_bf16:
            obf_ref[...] = out.astype(bf16)


def _resid_ln(pairs, resid, gamma, beta, *, alpha, tm, tn, n_kseg, emit_bf16):
    M, N = resid.shape
    n_j = N // tn
    last = n_kseg - 1

    def seg(k):
        return jnp.minimum(k, last)

    def col(k, j):
        return jnp.where(k < n_kseg, j, n_j - 1)

    in_specs, args = [], []
    for a, w in pairs:
        ks = a.shape[1] // n_kseg
        assert ks * n_kseg == a.shape[1] and ks % LANES == 0
        in_specs += [pl.BlockSpec((tm, ks), lambda i, k, j: (i, seg(k))),
                     pl.BlockSpec((ks, tn), lambda i, k, j: (seg(k), col(k, j)))]
        args += [a, w]
    in_specs += [pl.BlockSpec((tm, tn), lambda i, k, j: (i, jnp.where(k == 0, j, n_j - 1))),
                 pl.BlockSpec((1, tn), lambda i, k, j: (0, jnp.where(k == n_kseg, j, 0))),
                 pl.BlockSpec((1, tn), lambda i, k, j: (0, jnp.where(k == n_kseg, j, 0)))]
    out_dtypes = [f32, bf16] if emit_bf16 else [f32]
    return pl.pallas_call(
        functools.partial(_resid_ln_kernel, n_pair=len(pairs), alpha=alpha, n_j=n_j, n_kseg=n_kseg, tn=tn,
                          emit_bf16=emit_bf16),
        grid=(M // tm, n_kseg + 1, n_j),
        in_specs=in_specs,
        out_specs=[pl.BlockSpec((tm, tn), lambda i, k, j: (i, jnp.where(k == n_kseg, j, 0))) for _ in out_dtypes],
        out_shape=[jax.ShapeDtypeStruct((M, N), dt) for dt in out_dtypes],
        scratch_shapes=[pltpu.VMEM((n_j, tm, tn), f32), pltpu.VMEM((tm, 1), f32), pltpu.VMEM((tm, 1), f32)],
        compiler_params=_cparams(("parallel", "arbitrary", "arbitrary")),
        name="matmul_resid_layernorm",
    )(*args, resid, gamma, beta)


def _pool_kernel(hist_ref, u_ref, wp_ref, sc_ref, o_ref, buf, *, tm, pos0):
    i = pl.program_id(1)
    H = POOL_MAX

    @pl.when(i == 0)
    def _():
        buf[0:H, :] = hist_ref[...]

    @pl.when(i > 0)
    def _():
        buf[0:H, :] = buf[tm:tm + H, :]

    buf[H:H + tm, :] = u_ref[...]
    pg = wp_ref.shape[1]
    pos = pos0 + i * tm + lax.broadcasted_iota(i32, (tm, 1), 0)
    for g, w in enumerate(POOL_WINDOWS):
        cols = slice(g * pg, (g + 1) * pg)
        cur = buf[H:H + tm, cols]
        tot = cur
        for k in range(1, w):
            tot = tot + buf[H - k:H - k + tm, cols]
        cnt = jnp.minimum(pos + 1, w).astype(f32)
        d = tot / cnt - cur
        y = _dot(d.astype(bf16), wp_ref[g]) * sc_ref[:, cols]
        o_ref[:, cols] = y.astype(o_ref.dtype)


def _pool_mix(hist, u, w_pool, scale, *, nb, n, pos0, tm, out_dtype):
    W = u.shape[1]
    nt = n // tm
    return pl.pallas_call(
        functools.partial(_pool_kernel, tm=tm, pos0=pos0),
        grid=(nb, nt),
        in_specs=[pl.BlockSpec((None, POOL_MAX, W), lambda b, i: (b, 0, 0)),
                  pl.BlockSpec((tm, W), lambda b, i: (b * nt + i, 0)),
                  pl.BlockSpec(w_pool.shape, lambda b, i: (0, 0, 0)),
                  pl.BlockSpec((1, W), lambda b, i: (0, 0))],
        out_specs=pl.BlockSpec((tm, W), lambda b, i: (b * nt + i, 0)),
        out_shape=jax.ShapeDtypeStruct((nb * n, W), out_dtype),
        scratch_shapes=[pltpu.VMEM((POOL_MAX + tm, W), f32)],
        compiler_params=_cparams(("parallel", "arbitrary")),
        name="pool_mix",
    )(hist, u, w_pool, scale)


def _cmp_part_kernel(pt_ref, *refs, n_pg, G):
    pages = refs[:n_pg]
    w_ref, o_ref, y_scr = refs[n_pg:]
    cpp = pages[0].shape[0]
    tok_per_tile = SUBLANES // G
    n_tile = n_pg * cpp
    rows = n_tile * SUBLANES
    width = w_ref.shape[2]
    row_in_tile = lax.broadcasted_iota(i32, (rows, HEAD_DIM), 0) & (SUBLANES - 1)
    acc = None
    for pp in range(CMP_STRIDE // tok_per_tile):
        x = jnp.concatenate([pages[pg][:, pp * SUBLANES:(pp + 1) * SUBLANES, :] for pg in range(n_pg)], axis=0)
        x = x.reshape(rows, HEAD_DIM)
        routed = [jnp.where((row_in_tile >= t * G) & (row_in_tile < (t + 1) * G), x, 0.0)
                  for t in range(tok_per_tile)]
        d = _dot(jnp.concatenate(routed, axis=1).astype(bf16), w_ref[pp])
        acc = d if acc is None else acc + d
    y = acc
    for t in range(1, tok_per_tile):
        y = y + pltpu.roll(acc, rows - t * G, axis=0)
    y_scr[...] = y.reshape(n_tile, SUBLANES, width)
    for g in range(G):
        o_ref[g] = y_scr[:, g, :]


def _cmp_part(pages, page_table, w1cat, *, G, n_pg):
    nb, n_pages = page_table.shape
    assert SUBLANES % G == 0
    chunk_rows = CMP_STRIDE * G
    cpp = pages.shape[1] // chunk_rows
    n_chunk = n_pages * cpp
    n_tile = n_pg * cpp
    width = w1cat.shape[1]
    tok_per_tile = SUBLANES // G
    pages = pages.reshape(pages.shape[0], cpp, chunk_rows, HEAD_DIM)
    w_tiles = w1cat.reshape(CMP_STRIDE // tok_per_tile, tok_per_tile * HEAD_DIM, width)

    def page_spec(pg):
        return pl.BlockSpec((None, cpp, chunk_rows, HEAD_DIM), lambda b, j, pt: (pt[b, j * n_pg + pg], 0, 0, 0))

    grid_spec = pltpu.PrefetchScalarGridSpec(
        num_scalar_prefetch=1,
        grid=(nb, n_pages // n_pg),
        in_specs=[page_spec(pg) for pg in range(n_pg)]
        + [pl.BlockSpec(w_tiles.shape, lambda b, j, pt: (0, 0, 0))],
        out_specs=pl.BlockSpec((None, G, n_tile, width), lambda b, j, pt: (b, 0, j, 0)),
        scratch_shapes=[pltpu.VMEM((n_tile, SUBLANES, width), f32)],
    )
    return pl.pallas_call(
        functools.partial(_cmp_part_kernel, n_pg=n_pg, G=G),
        grid_spec=grid_spec,
        out_shape=jax.ShapeDtypeStruct((nb, G, n_chunk, width), f32),
        compiler_params=_cparams(("parallel", "arbitrary")),
        name="cmp_part",
    )(page_table, *([pages] * n_pg), w_tiles)


def _gelu_tanh(x):
    return 0.5 * x * (1.0 + jnp.tanh(math.sqrt(2.0 / math.pi) * (x + 0.044715 * (x * x * x))))


def _cmp_finish_kernel(part_ref, pe_ref, w1_ref, w2_ref, cs_ref, o_ref, *, rope, transposed):
    n_chunk = part_ref.shape[0]
    hid0 = _dot(pe_ref[...].astype(bf16), w1_ref[...])[0:1]
    a = part_ref[:, :HEAD_DIM]
    b = pltpu.roll(part_ref[:, HEAD_DIM:], n_chunk - 1, axis=0)
    y = _dot(_gelu_tanh(hid0 + a + b).astype(bf16), w2_ref[...])
    if rope:
        y = _rope_tile(y, cs_ref[:, :HEAD_DIM], cs_ref[:, HEAD_DIM:])
    o_ref[...] = (y.T if transposed else y).astype(o_ref.dtype)


def _cmp_finish(part, pe, w1flat, w2, cs_cmp, *, rope, transposed):
    nb, G, n_chunk, _ = part.shape
    out_dims = (HEAD_DIM, n_chunk) if transposed else (n_chunk, HEAD_DIM)
    return pl.pallas_call(
        functools.partial(_cmp_finish_kernel, rope=rope, transposed=transposed),
        grid=(nb, G),
        in_specs=[pl.BlockSpec((None, None, n_chunk, CMP_RATIO * HEAD_DIM), lambda b, g: (b, g, 0, 0)),
                  pl.BlockSpec(pe.shape, lambda b, g: (0, 0)),
                  pl.BlockSpec(w1flat.shape, lambda b, g: (0, 0)),
                  pl.BlockSpec(w2.shape, lambda b, g: (0, 0)),
                  pl.BlockSpec(cs_cmp.shape, lambda b, g: (0, 0))],
        out_specs=pl.BlockSpec((None, None) + out_dims, lambda b, g: (b, g, 0, 0)),
        out_shape=jax.ShapeDtypeStruct((nb, G) + out_dims, bf16),
        compiler_params=_cparams(("parallel", "parallel")),
        name="cmp_finish",
    )(part, pe, w1flat, w2, cs_cmp)


def _softmax_terms(s_biased, floor=None):
    m = jnp.max(s_biased, axis=-1, keepdims=True)
    if floor is not None:
        m = jnp.maximum(m, floor)
    e = jnp.exp(s_biased - m)
    return e, 1.0 / jnp.maximum(jnp.sum(e, axis=-1, keepdims=True), 1e-30)


def _dot_split3(p, w01):
    hi = p.astype(bf16)
    r1 = p - hi.astype(f32)
    mid = r1.astype(bf16)
    lo = (r1 - mid.astype(f32)).astype(bf16)
    return _dot(hi, w01) + _dot(mid, w01) + _dot(lo, w01)


def _overlap01(n_chunk, n_cmp, ns):
    n = lax.broadcasted_iota(i32, (n_chunk, ns), 0)
    s = lax.broadcasted_iota(i32, (n_chunk, ns), 1)
    ov = ((n * CMP_STRIDE < s * SLC_LEN + SLC_LEN) & (n * CMP_STRIDE + CMP_LEN - 1 >= s * SLC_LEN)
          & (n < n_cmp))
    return jnp.where(ov, 1.0, 0.0).astype(bf16)


def _cmp_bias(q_pos, n_chunk, n_cmp):
    n_io = lax.broadcasted_iota(i32, (q_pos.shape[0], n_chunk), 1)
    return jnp.where((n_io * CMP_STRIDE + CMP_LEN - 1 <= q_pos) & (n_io < n_cmp), 0.0, NEG)


def _selection_values(imp, s_io, cur):
    forced = (s_io == 0) | (s_io == cur) | (s_io == cur - 1)
    return jnp.where(forced, jnp.inf, jnp.where(s_io <= cur, imp, -jnp.inf))


def _ahead(other, val, other_first):
    return (other > val) | ((other == val) & other_first)


def _online_step(s, v, m_old, l_old, acc_old):
    m_new = jnp.maximum(m_old, jnp.max(s, axis=-1, keepdims=True))
    alpha = jnp.exp(m_old - m_new)
    e = jnp.exp(s - m_new)
    l_new = alpha * l_old + jnp.sum(e, axis=-1, keepdims=True)
    acc_new = alpha * acc_old + _dot(e.astype(bf16), v)
    return m_new, l_new, acc_new


def _prompt_attn_kernel(q_ref, gt_ref, kcb_ref, vcbt_ref, ks_ref, vs_ref, kw_ref, vw_ref, oh_ref, o_ref,
                        kaug, vst_ref, vwt_ref, qaug, val_scr, cnt_scr, acc_scr, *, R, n_cmp, n_slc, kc):
    c = pl.program_id(2)
    tq = Q_TILE
    nq = R * tq
    n_chunk = kcb_ref.shape[0]
    ns = oh_ref.shape[1]
    nr = val_scr.shape[0]
    win_keys = WINDOW + tq

    def transposed(tile):
        return tile.astype(f32).T.astype(bf16)

    @pl.when(c == 0)
    def _():
        kaug[:, :HEAD_DIM] = ks_ref[...]
        kaug[:, HEAD_DIM:] = oh_ref[...]
        for t in range(vst_ref.shape[0]):
            vst_ref[t] = transposed(vs_ref[t * tq:(t + 1) * tq, :])
            vwt_ref[t] = transposed(vw_ref[t * tq:(t + 1) * tq, :])

    gtt = gt_ref[...].T

    def all_heads(a):
        return jnp.concatenate([a] * R, axis=1)

    def softmax_down(s, floor=None):
        m = jnp.max(s, axis=0, keepdims=True)
        if floor is not None:
            m = jnp.maximum(m, floor)
        e = jnp.exp(s - m)
        return e, 1.0 / jnp.maximum(jnp.sum(e, axis=0, keepdims=True), 1e-30)

    for r in range(R):
        qaug[0:HEAD_DIM, r * tq:(r + 1) * tq] = transposed(q_ref[:, r * HEAD_DIM:(r + 1) * HEAD_DIM])
    q_t = qaug[0:HEAD_DIM, :]
    q_pos = c * tq + lax.broadcasted_iota(i32, (1, tq), 1)

    n_io = lax.broadcasted_iota(i32, (n_chunk, tq), 0)
    bias_c = jnp.where((n_io * CMP_STRIDE + CMP_LEN - 1 <= q_pos) & (n_io < n_cmp), 0.0, NEG)
    e, inv = softmax_down(_dot(kcb_ref[...], q_t) + all_heads(bias_c), floor=0.5 * NEG)
    p_c = e * inv
    o_c = _dot(vcbt_ref[...], p_c.astype(bf16))
    p_sum = p_c[:, 0:tq]
    for r in range(1, R):
        p_sum = p_sum + p_c[:, r * tq:(r + 1) * tq]

    blk = lax.broadcasted_iota(i32, (nr, n_chunk), 0)
    n_col = lax.broadcasted_iota(i32, (nr, n_chunk), 1)
    ov = jnp.where((n_col * CMP_STRIDE < blk * SLC_LEN + SLC_LEN)
                   & (n_col * CMP_STRIDE + CMP_LEN - 1 >= blk * SLC_LEN) & (n_col < n_cmp), 1.0, 0.0).astype(bf16)
    hi = p_sum.astype(bf16)
    r1 = p_sum - hi.astype(f32)
    mid = r1.astype(bf16)
    lo = (r1 - mid.astype(f32)).astype(bf16)
    imp_t = _dot(ov, hi) + _dot(ov, mid) + _dot(ov, lo)
    s_io = lax.broadcasted_iota(i32, (nr, tq), 0)
    cur = _div_pow2(c * tq + lax.broadcasted_iota(i32, (nr, tq), 1), SLC_LEN)
    val_scr[...] = _selection_values(imp_t, s_io, cur)
    cnt_scr[...] = jnp.zeros(cnt_scr.shape, f32)
    n_live = _div_pow2(c * tq + tq - 1, SLC_LEN) + 1
    for grp in range(nr // SUBLANES):
        @pl.when(grp * SUBLANES < n_live)
        def _():
            val = val_scr[...]
            cnt = cnt_scr[...]
            for sp in range(grp * SUBLANES, min((grp + 1) * SUBLANES, n_slc)):
                cnt = cnt + jnp.where(_ahead(val_scr[sp:sp + 1, :], val, s_io > sp), 1.0, 0.0)
            cnt_scr[...] = cnt

    first_blk = _div_pow2(c * tq, SLC_LEN)
    keep = (cnt_scr[...] < N_SEL) & (val_scr[...] > -jnp.inf) & (s_io < first_blk)
    mask_t = jnp.where(keep, 0.0, NEG)
    if nr < ns:
        mask_t = jnp.concatenate([mask_t, jnp.full((ns - nr, tq), NEG, f32)], axis=0)
    qaug[HEAD_DIM:, :] = all_heads(mask_t.astype(bf16))

    d0 = pl.multiple_of(c * tq, tq)
    tri = jnp.where(lax.broadcasted_iota(i32, (tq, tq), 0) <= lax.broadcasted_iota(i32, (tq, tq), 1), 0.0, NEG)
    s_d = _dot(ks_ref[pl.ds(d0, tq), :], q_t) + all_heads(tri)
    m_d = jnp.max(s_d, axis=0, keepdims=True)
    e_d = jnp.exp(s_d - m_d)
    l_d = jnp.sum(e_d, axis=0, keepdims=True)
    acc_scr[...] = _dot(vst_ref[c], e_d.astype(bf16))

    tiles_per_chunk = kc // tq

    def sweep(j, carry):
        m_old, l_old = carry
        k0 = pl.multiple_of(j * kc, kc)
        s = _dot(kaug[pl.ds(k0, kc), :], qaug[...])
        m_new = jnp.maximum(m_old, jnp.max(s, axis=0, keepdims=True))
        alpha = jnp.exp(m_old - m_new)
        e = jnp.exp(s - m_new)
        v_t = jnp.concatenate([vst_ref[j * tiles_per_chunk + i] for i in range(tiles_per_chunk)], axis=1)
        acc_scr[...] = alpha * acc_scr[...] + _dot(v_t, e.astype(bf16))
        return m_new, alpha * l_old + jnp.sum(e, axis=0, keepdims=True)

    _, l_s = lax.fori_loop(0, (c * tq + kc - 1) // kc, sweep, (m_d, l_d))
    o_s = acc_scr[...] * (1.0 / jnp.maximum(l_s, 1e-30))

    w_tile = jnp.maximum(c - WINDOW // tq, 0)
    w0 = pl.multiple_of(w_tile * tq, tq)
    key_w = w0 + lax.broadcasted_iota(i32, (win_keys, tq), 0)
    bias_w = jnp.where((key_w <= q_pos) & (q_pos - key_w < WINDOW), 0.0, NEG)
    e, inv = softmax_down(_dot(kw_ref[pl.ds(w0, win_keys), :], q_t) + all_heads(bias_w))
    vw_t = jnp.concatenate([vwt_ref[w_tile + i] for i in range(win_keys // tq)], axis=1)
    o_w = _dot(vw_t, e.astype(bf16)) * inv

    for r in range(R):
        cols = slice(r * tq, (r + 1) * tq)
        g = [gtt[r * N_BRANCH + br:r * N_BRANCH + br + 1, :] for br in range(N_BRANCH)]
        o_t = g[0] * o_c[:, cols] + g[1] * o_s[:, cols] + g[2] * o_w[:, cols]
        o_ref[:, r * HEAD_DIM:(r + 1) * HEAD_DIM] = o_t.T.astype(o_ref.dtype)


def _prompt_attn(q, gates, kcb, vcb_t, ks, vs, kw, vw, onehot, *, B, T, G, R):
    n_chunk = kcb.shape[2]
    n_cmp = n_chunk - CMP_RATIO + 1
    n_slc = pl.cdiv(T, SLC_LEN)
    ns = onehot.shape[1]
    nr = _round_up(n_slc, SUBLANES)
    kc = _pick_tile(T, 1024, Q_TILE)
    nt = T // Q_TILE
    assert T % Q_TILE == 0 and T >= WINDOW + Q_TILE and Q_TILE % SLC_LEN == 0 and Q_TILE <= 2 * SLC_LEN
    seq_spec = pl.BlockSpec((T, HEAD_DIM), lambda b, g, c: (b, g))
    return pl.pallas_call(
        functools.partial(_prompt_attn_kernel, R=R, n_cmp=n_cmp, n_slc=n_slc, kc=kc),
        grid=(B, G, nt),
        in_specs=[pl.BlockSpec((Q_TILE, R * HEAD_DIM), lambda b, g, c: (b * nt + c, g)),
                  pl.BlockSpec((Q_TILE, LANES), lambda b, g, c: (b * nt + c, g)),
                  pl.BlockSpec((None, None, n_chunk, HEAD_DIM), lambda b, g, c: (b, g, 0, 0)),
                  pl.BlockSpec((None, None, HEAD_DIM, n_chunk), lambda b, g, c: (b, g, 0, 0)),
                  seq_spec, seq_spec, seq_spec, seq_spec,
                  pl.BlockSpec((T, ns), lambda b, g, c: (0, 0))],
        out_specs=pl.BlockSpec((Q_TILE, R * HEAD_DIM), lambda b, g, c: (b * nt + c, g)),
        out_shape=jax.ShapeDtypeStruct((B * T, G * R * HEAD_DIM), bf16),
        scratch_shapes=[pltpu.VMEM((T, HEAD_DIM + ns), bf16),
                        pltpu.VMEM((nt, HEAD_DIM, Q_TILE), bf16), pltpu.VMEM((nt, HEAD_DIM, Q_TILE), bf16),
                        pltpu.VMEM((HEAD_DIM + ns, R * Q_TILE), bf16),
                        pltpu.VMEM((nr, Q_TILE), f32), pltpu.VMEM((nr, Q_TILE), f32),
                        pltpu.VMEM((HEAD_DIM, R * Q_TILE), f32)],
        compiler_params=_cparams(("parallel", "parallel", "arbitrary")),
        name="nsa_prompt",
    )(q, gates, kcb, vcb_t, ks, vs, kw, vw, onehot)


def _sample_cmp_win_kernel(q_ref, gt_ref, kcb_ref, vcb_ref, kwin_ref, vwin_ref, kwn_ref, vwn_ref,
                           part_ref, qaug_ref, kbuf, vbuf, *, G, R, n_cmp, n_slc, past):
    n_chunk = kcb_ref.shape[1]
    ns = qaug_ref.shape[1] - G * HEAD_DIM
    wb = kwin_ref.shape[0] // G
    sp = SUBLANES
    rows = R * sp
    nbuf = kbuf.shape[0]
    q_pos = past + (lax.broadcasted_iota(i32, (rows, 1), 0) & (sp - 1))
    bias_c = _cmp_bias(q_pos, n_chunk, n_cmp)
    key_w = past - wb + lax.broadcasted_iota(i32, (rows, nbuf), 1)
    bias_w = jnp.where((key_w <= q_pos) & (q_pos - key_w < WINDOW) & (key_w >= 0), 0.0, NEG)
    ov = _overlap01(n_chunk, n_cmp, ns)
    s_io = lax.broadcasted_iota(i32, (sp, ns), 1)
    cur = _div_pow2(q_pos[0:sp], SLC_LEN)
    qaug_ref[:, :G * HEAD_DIM] = jnp.zeros((G * rows, G * HEAD_DIM), bf16)

    for g in range(G):
        q = q_ref[g]
        cols = slice(g * HEAD_DIM, (g + 1) * HEAD_DIM)

        e, inv = _softmax_terms(_dot_nt(q, kcb_ref[g]) + bias_c, floor=0.5 * NEG)
        p_c = e * inv
        o_c = _dot(p_c.astype(bf16), vcb_ref[g])
        p_sum = p_c[0:sp]
        for r in range(1, R):
            p_sum = p_sum + p_c[r * sp:(r + 1) * sp]
        val = _selection_values(_dot_split3(p_sum, ov), s_io, cur)
        cnt = jnp.zeros((sp, ns), f32)
        for blk in range(n_slc):
            cnt = cnt + jnp.where(_ahead(val[:, blk:blk + 1], val, s_io > blk), 1.0, 0.0)
        mask = jnp.where((cnt < N_SEL) & (val > -jnp.inf), 0.0, NEG).astype(bf16)

        qaug_ref[g * rows:(g + 1) * rows, cols] = q
        qaug_ref[g * rows:(g + 1) * rows, G * HEAD_DIM:] = jnp.concatenate([mask] * R, axis=0)

        kbuf[0:wb, :] = kwin_ref[pl.ds(g, wb, stride=G), :].astype(bf16)
        kbuf[wb:nbuf, :] = kwn_ref[:, cols]
        vbuf[0:wb, :] = vwin_ref[pl.ds(g, wb, stride=G), :].astype(bf16)
        vbuf[wb:nbuf, :] = vwn_ref[:, cols]
        e, inv = _softmax_terms(_dot_nt(q, kbuf[...]) + bias_w)
        o_w = _dot(e.astype(bf16), vbuf[...]) * inv
        part_ref[g] = gt_ref[g][:, 0:1] * o_c + gt_ref[g][:, 2:3] * o_w


def _sample_cmp_win(q, gates, kcb, vcb, kwin, vwin, kw_new, vw_new, *, past, n_slc, ns):
    DB, G, rows, _ = q.shape
    R = rows // SUBLANES
    n_chunk = kcb.shape[2]
    n_cmp = n_chunk - CMP_RATIO + 1
    wbg = kwin.shape[1]
    npad = kw_new.shape[1]
    qspec = pl.BlockSpec((None, G, rows, HEAD_DIM), lambda b: (b, 0, 0, 0))
    cspec = pl.BlockSpec((None, G, n_chunk, HEAD_DIM), lambda b: (b, 0, 0, 0))
    wspec = pl.BlockSpec((None, wbg, HEAD_DIM), lambda b: (b, 0, 0))
    nspec = pl.BlockSpec((None, npad, G * HEAD_DIM), lambda b: (b, 0, 0))
    aug_w = G * HEAD_DIM + ns
    return pl.pallas_call(
        functools.partial(_sample_cmp_win_kernel, G=G, R=R, n_cmp=n_cmp, n_slc=n_slc, past=past),
        grid=(DB,),
        in_specs=[qspec, qspec, cspec, cspec, wspec, wspec, nspec, nspec],
        out_specs=[qspec, pl.BlockSpec((None, G * rows, aug_w), lambda b: (b, 0, 0))],
        out_shape=[jax.ShapeDtypeStruct((DB, G, rows, HEAD_DIM), f32),
                   jax.ShapeDtypeStruct((DB, G * rows, aug_w), bf16)],
        scratch_shapes=[pltpu.VMEM((wbg // G + npad, HEAD_DIM), bf16)] * 2,
        compiler_params=_cparams(("parallel",)),
        name="nsa_sample_cmp_win",
    )(q, gates, kcb, vcb, kwin, vwin, kw_new, vw_new)


def _sample_sel_kernel(pt_ref, *refs, n_pg, G, R, page):
    kpages = refs[:n_pg]
    vpages = refs[n_pg:2 * n_pg]
    (oh_ref, ohn_ref, qaug_ref, gt_ref, part_ref, ksn_ref, vsn_ref, o_ref,
     kaug, vall, m_scr, l_scr, acc_scr) = refs[2 * n_pg:]
    j = pl.program_id(1)
    sp = SUBLANES
    rows = R * sp
    kvw = G * HEAD_DIM

    @pl.when(j == 0)
    def _():
        m_scr[...] = jnp.full(m_scr.shape, NEG, f32)
        l_scr[...] = jnp.zeros(l_scr.shape, f32)
        acc_scr[...] = jnp.zeros(acc_scr.shape, f32)

    def update(n_keys, bias):
        s = _dot_nt(qaug_ref[...], kaug[0:n_keys, :])
        if bias is not None:
            s = s + bias
        m, l, acc = _online_step(s, vall[0:n_keys, :], m_scr[...], l_scr[...], acc_scr[...])
        m_scr[...] = m
        l_scr[...] = l
        acc_scr[...] = acc

    for pg in range(n_pg):
        for g in range(G):
            kaug[pg * page:(pg + 1) * page, g * HEAD_DIM:(g + 1) * HEAD_DIM] = (
                kpages[pg][pl.ds(g, page, stride=G), :].astype(bf16))
            vall[pg * page:(pg + 1) * page, g * HEAD_DIM:(g + 1) * HEAD_DIM] = (
                vpages[pg][pl.ds(g, page, stride=G), :].astype(bf16))
    kaug[:, kvw:] = oh_ref[...]
    update(n_pg * page, None)

    @pl.when(j == pl.num_programs(1) - 1)
    def _():
        kaug[0:page, :kvw] = ksn_ref[...]
        kaug[0:page, kvw:] = ohn_ref[...]
        vall[0:page, :] = vsn_ref[...]
        step = lax.broadcasted_iota(i32, (G * rows, page), 0) & (sp - 1)
        causal = jnp.where(lax.broadcasted_iota(i32, (G * rows, page), 1) <= step, 0.0, NEG)
        update(page, causal)
        inv = 1.0 / jnp.maximum(l_scr[...], 1e-30)
        for g in range(G):
            rs = slice(g * rows, (g + 1) * rows)
            o_s = acc_scr[rs, g * HEAD_DIM:(g + 1) * HEAD_DIM] * inv[rs]
            o_ref[g] = part_ref[g] + gt_ref[g][:, 1:2] * o_s


def _sample_sel(kcache, vcache, page_table, onehot, qaug, gates, part, ks_new, vs_new, *, G, n_pg):
    DB, _, rows, _ = part.shape
    R = rows // SUBLANES
    n_pages = page_table.shape[1]
    page = kcache.shape[1] // G
    ns = onehot.shape[1]
    aug_w = G * HEAD_DIM + ns
    n_keys = n_pg * page

    def page_spec(pg):
        return pl.BlockSpec((None, page * G, HEAD_DIM), lambda b, j, pt: (pt[b, j * n_pg + pg], 0, 0))

    pspec = pl.BlockSpec((None, G, rows, HEAD_DIM), lambda b, j, pt: (b, 0, 0, 0))
    nspec = pl.BlockSpec((None, page, G * HEAD_DIM), lambda b, j, pt: (b, 0, 0))
    grid_spec = pltpu.PrefetchScalarGridSpec(
        num_scalar_prefetch=1,
        grid=(DB, n_pages // n_pg),
        in_specs=[page_spec(pg) for pg in range(n_pg)] * 2
        + [pl.BlockSpec((n_keys, ns), lambda b, j, pt: (j, 0)),
           pl.BlockSpec((page, ns), lambda b, j, pt: (n_pages, 0)),
           pl.BlockSpec((None, G * rows, aug_w), lambda b, j, pt: (b, 0, 0)),
           pspec, pspec, nspec, nspec],
        out_specs=pspec,
        scratch_shapes=[pltpu.VMEM((n_keys, aug_w), bf16), pltpu.VMEM((n_keys, G * HEAD_DIM), bf16),
                        pltpu.VMEM((G * rows, 1), f32), pltpu.VMEM((G * rows, 1), f32),
                        pltpu.VMEM((G * rows, G * HEAD_DIM), f32)],
    )
    return pl.pallas_call(
        functools.partial(_sample_sel_kernel, n_pg=n_pg, G=G, R=R, page=page),
        grid_spec=grid_spec,
        out_shape=jax.ShapeDtypeStruct(part.shape, f32),
        compiler_params=_cparams(("parallel", "arbitrary")),
        name="nsa_sample_sel",
    )(page_table, *([kcache] * n_pg), *([vcache] * n_pg), onehot, onehot, qaug, gates, part, ks_new, vs_new)


def _rope_table(pos):
    half = HEAD_DIM // 2
    inv = ROPE_THETA ** (-jnp.arange(half, dtype=f32) / half)
    ang = pos.astype(f32)[:, None] * inv[None, :]
    cos, sin = jnp.cos(ang), jnp.sin(ang)
    return jnp.concatenate([cos, cos, -sin, sin], axis=-1)


def _block_onehot(n_keys, ns):
    blk = jnp.arange(n_keys, dtype=i32)[:, None] // SLC_LEN
    return (blk == jnp.arange(ns, dtype=i32)[None, :]).astype(bf16)


def kernel(x_prompt, x_sample, cache_k_cmp, cache_v_cmp, cache_k_slc, cache_v_slc, state_k_win, state_v_win, state_pool, page_table, w_in, w_cmp1_k, pe_cmp_k, w_cmp2_k, w_cmp1_v, pe_cmp_v, w_cmp2_v, w_pool, pool_scale, w_o, ln1_g, ln1_b, w_gate, w_up, w_down, ln2_g, ln2_b):
    B, T, D = x_prompt.shape
    DB, S, _ = x_sample.shape
    depth, n_phys, page, G, _ = cache_k_cmp.shape
    assert depth == 1 and S <= SUBLANES
    n_pages = page_table.shape[1]
    past = n_pages * page
    PW = state_pool.shape[3]
    NW = D - PW
    NH = NW // HEAD_DIM
    R = NH // G
    KVW = G * HEAD_DIM
    F = w_gate.shape[2]
    wb = state_k_win.shape[2]
    wbp = min(WINDOW, T)
    alpha = (2 * depth) ** 0.25
    MP = B * T
    MS = DB * S
    M = MP + MS
    assert T % page == 0 and page % CMP_STRIDE == 0 and page % SLC_LEN == 0 and past % SLC_LEN == 0

    x_all = jnp.concatenate([x_prompt.reshape(MP, D), x_sample.reshape(MS, D)], axis=0)
    x_bf = x_all.astype(bf16)
    w_in0 = w_in[0]
    o_kv = PW + NW
    o_g = o_kv + N_KV_SEG * KVW
    w_g = w_in0[:, o_g:o_g + N_BRANCH * NH].reshape(D, G, R * N_BRANCH)
    w_g = jnp.pad(w_g, ((0, 0), (0, 0), (0, LANES - R * N_BRANCH))).reshape(D, G * LANES)

    pos_all = jnp.concatenate([jnp.tile(jnp.arange(T, dtype=i32), B),
                               jnp.tile(past + jnp.arange(S, dtype=i32), DB)])
    cs_all = _rope_table(pos_all)

    tm = _pick_tile(M, 640, SUBLANES)
    assert PW % KVW == 0 and NW % KVW == 0
    u = _proj(x_bf, w_in0, cs_all, col0=0, n_col=PW // KVW, mode="none", tm=tm, tn=KVW)
    q = _proj(x_bf, w_in0, cs_all, col0=PW // KVW, n_col=NW // KVW, mode="rope", scale=HEAD_DIM ** -0.5,
              out_dtype=bf16, tm=tm, tn=KVW)
    gates = _proj(x_bf, w_g, cs_all, col0=0, n_col=1, mode="sigmoid", tm=tm, tn=G * LANES)
    kv_st, kv_bf = _proj_kv(x_bf, w_in0, cs_all, col0=o_kv // KVW, n_seg=N_KV_SEG, G=G, tm=tm)
    kc_st, vc_st = kv_st[0], kv_st[1]
    ks_bf, vs_bf, kw_bf, vw_bf = kv_bf[2], kv_bf[3], kv_bf[4], kv_bf[5]

    wp_bf = w_pool[0].astype(bf16)
    pscale = pool_scale[0].reshape(1, PW)
    pool_p = _pool_mix(jnp.zeros((B, POOL_MAX, PW), f32), u, wp_bf, pscale, nb=B, n=T, pos0=0,
                       tm=_pick_tile(T, 512, SUBLANES), out_dtype=bf16)
    u_s = u[MP:].reshape(DB, S, PW)
    hist_s = jnp.pad(state_pool[0], ((0, 0), (POOL_MAX - state_pool.shape[2], 0), (0, 0)))
    u_s_pad = jnp.pad(u_s, ((0, 0), (0, SUBLANES - S), (0, 0))).reshape(DB * SUBLANES, PW)
    pool_s = _pool_mix(hist_s, u_s_pad, wp_bf, pscale, nb=DB, n=SUBLANES, pos0=past, tm=SUBLANES,
                       out_dtype=f32).reshape(DB, SUBLANES, PW)[:, :S]

    def cmp_weights(w1, pe, w2):
        w1cat = w1[0].reshape(CMP_RATIO, CMP_STRIDE, HEAD_DIM, HEAD_DIM).transpose(1, 2, 0, 3)
        w1cat = w1cat.reshape(CMP_STRIDE * HEAD_DIM, CMP_RATIO * HEAD_DIM).astype(bf16)
        pe8 = jnp.broadcast_to(pe[0].reshape(1, CMP_LEN * HEAD_DIM), (SUBLANES, CMP_LEN * HEAD_DIM))
        return w1cat, pe8, w1[0].reshape(CMP_LEN * HEAD_DIM, HEAD_DIM).astype(bf16), w2[0].astype(bf16)

    cw_k = cmp_weights(w_cmp1_k, pe_cmp_k, w_cmp2_k)
    cw_v = cmp_weights(w_cmp1_v, pe_cmp_v, w_cmp2_v)

    def compress(pages, table, cw, rope, transposed=False):
        n_chunk = table.shape[1] * (page // CMP_STRIDE)
        cs_cmp = _rope_table(jnp.arange(n_chunk, dtype=i32) * CMP_STRIDE + CMP_LEN - 1)
        part = _cmp_part(pages, table, cw[0], G=G, n_pg=_pick_tile(table.shape[1], 16, 1))
        return _cmp_finish(part, cw[1], cw[2], cw[3], cs_cmp, rope=rope, transposed=transposed)

    def as_pages(a):
        return a.reshape(-1, page * G, HEAD_DIM)

    table_p = jnp.arange(MP // page, dtype=i32).reshape(B, T // page)
    kcb_p = compress(as_pages(kc_st[:MP * G]), table_p, cw_k, True)
    vcb_p_t = compress(as_pages(vc_st[:MP * G]), table_p, cw_v, False, transposed=True)
    kcb_s = compress(as_pages(cache_k_cmp.reshape(-1, HEAD_DIM)), page_table, cw_k, True)
    vcb_s = compress(as_pages(cache_v_cmp.reshape(-1, HEAD_DIM)), page_table, cw_v, False)

    ns_p = _round_up(pl.cdiv(T, SLC_LEN), LANES)
    nsa_p = _prompt_attn(q, gates, kcb_p, vcb_p_t, ks_bf, vs_bf, kw_bf, vw_bf, _block_onehot(T, ns_p),
                         B=B, T=T, G=G, R=R)

    def srow(a):
        return a[MP:].reshape(DB, S, a.shape[1])

    def to_rows(a):
        a = jnp.pad(a.transpose(0, 2, 3, 1, 4), ((0, 0), (0, 0), (0, 0), (0, SUBLANES - S), (0, 0)))
        return a.reshape(DB, G, R * SUBLANES, a.shape[-1])

    q_s = to_rows(srow(q).reshape(DB, S, G, R, HEAD_DIM))
    g_s = srow(gates).reshape(DB, S, G, LANES)[..., :R * N_BRANCH].reshape(DB, S, G, R, N_BRANCH)
    g_s = to_rows(jnp.pad(g_s, ((0, 0),) * 4 + ((0, LANES - N_BRANCH),)))

    def new_rows(a):
        return jnp.pad(srow(a), ((0, 0), (0, page - S), (0, 0)))

    n_slc_s = pl.cdiv(past + S, SLC_LEN)
    ns_s = _round_up(n_slc_s, LANES)
    part_s, qaug_s = _sample_cmp_win(q_s, g_s, kcb_s, vcb_s,
                                     state_k_win.reshape(DB, wb * G, HEAD_DIM),
                                     state_v_win.reshape(DB, wb * G, HEAD_DIM),
                                     new_rows(kw_bf), new_rows(vw_bf), past=past, n_slc=n_slc_s, ns=ns_s)
    o_s = _sample_sel(as_pages(cache_k_slc.reshape(-1, HEAD_DIM)), as_pages(cache_v_slc.reshape(-1, HEAD_DIM)),
                      page_table, _block_onehot(past + page, ns_s), qaug_s, g_s, part_s,
                      new_rows(ks_bf), new_rows(vs_bf), G=G, n_pg=_pick_tile(n_pages, 16, 1))
    nsa_s = o_s.reshape(DB, G, R, SUBLANES, HEAD_DIM)[:, :, :, :S].transpose(0, 3, 1, 2, 4)
    nsa_s = nsa_s.reshape(MS, NW).astype(bf16)

    pool_all = jnp.concatenate([pool_p, pool_s.reshape(MS, PW).astype(bf16)], axis=0)
    nsa_all = jnp.concatenate([nsa_p, nsa_s], axis=0)
    wo_bf = w_o[0].astype(bf16)
    h, h_bf = _resid_ln([(pool_all, wo_bf[:PW]), (nsa_all, wo_bf[PW:])], x_all,
                        ln1_g[0].reshape(1, D), ln1_b[0].reshape(1, D),
                        alpha=alpha, tm=tm, tn=_pick_tile(D, 512, LANES), n_kseg=1, emit_bf16=True)
    ff = _gateup(h_bf, w_gate[0], w_up[0], tm=tm, tf=_pick_tile(F, 512, LANES))
    n_kseg = 2 if F % (2 * LANES) == 0 else 1
    y, = _resid_ln([(ff, w_down[0].astype(bf16))], h, ln2_g[0].reshape(1, D), ln2_b[0].reshape(1, D),
                   alpha=alpha, tm=tm, tn=_pick_tile(D, 256, LANES), n_kseg=n_kseg, emit_bf16=False)

    def pstate(i):
        return kv_st[i, :MP * G].reshape(1, B, T, G, HEAD_DIM)

    def sstate(i):
        return kv_st[i, MP * G:].reshape(1, DB, S, G, HEAD_DIM)

    kw_p, vw_p = pstate(4)[:, :, T - wbp:], pstate(5)[:, :, T - wbp:]
    pool_state_p = u[:MP].reshape(1, B, T, PW)[:, :, T - (POOL_MAX - 1):]
    kw_s = jnp.concatenate([state_k_win, sstate(4)], axis=2)[:, :, S:]
    vw_s = jnp.concatenate([state_v_win, sstate(5)], axis=2)[:, :, S:]
    pool_state_s = jnp.concatenate([state_pool, u_s[None]], axis=2)[:, :, S:]
    return (y[:MP].reshape(B, T, D), y[MP:].reshape(DB, S, D),
            pstate(0), pstate(1), pstate(2), pstate(3), kw_p, vw_p, pool_state_p,
            sstate(0), sstate(1), sstate(2), sstate(3), kw_s, vw_s, pool_state_s)
```

```python
import functools
import math

import jax
import jax.numpy as jnp
from jax import lax
from jax.experimental import pallas as pl
from jax.experimental.pallas import tpu as pltpu

HEAD_DIM = 128
POOL_WINDOWS = (2, 4, 8, 16)
POOL_MAX = max(POOL_WINDOWS)
N_BRANCH = 3
N_KV_SEG = 6
ROPE_KV_SEGS = (2, 4)
CMP_LEN = 32
CMP_STRIDE = 16
CMP_RATIO = CMP_LEN // CMP_STRIDE
SLC_LEN = 64
N_SEL = 16
WINDOW = 512
Q_TILE = 128
ROPE_THETA = 10000.0
LN_EPS = 1e-5
LANES = 128
SUBLANES = 8
VMEM_LIMIT = 56 * 1024 * 1024
NEG = -1e30

f32 = jnp.float32
bf16 = jnp.bfloat16
i32 = jnp.int32


def _cparams(sem):
    return pltpu.CompilerParams(dimension_semantics=sem, vmem_limit_bytes=VMEM_LIMIT)


def _pick_tile(n, target, align):
    best = None
    for t in range(align, min(n, target) + 1, align):
        if n % t == 0:
            best = t
    if best is None:
        raise ValueError(f"no tile for {n} (target {target}, align {align})")
    return best


def _round_up(n, m):
    return m * pl.cdiv(n, m)


def _dot(a, b):
    return jnp.dot(a, b, preferred_element_type=f32)


def _dot_nt(a, b):
    return lax.dot_general(a, b, (((1,), (1,)), ((), ())), preferred_element_type=f32)


def _rope_tile(a, cosf, sinf):
    return a * cosf + pltpu.roll(a, HEAD_DIM // 2, axis=1) * sinf


def _div_pow2(x, d):
    assert d & (d - 1) == 0
    return lax.shift_right_arithmetic(x, d.bit_length() - 1)


def _sigmoid(x):
    return 1.0 / (1.0 + jnp.exp(-x))


def _cast_weights_once(w_refs, wbf_refs, transpose=False):
    @pl.when(pl.program_id(1) == 0)
    def _():
        for w_ref, wbf in zip(w_refs, wbf_refs):
            w = w_ref[...]
            wbf[...] = (w.T if transpose else w).astype(bf16)


def _proj_kernel(x_ref, w_ref, cs_ref, o_ref, wbf, *, mode, scale):
    _cast_weights_once([w_ref], [wbf], transpose=True)
    acc = _dot(x_ref[...], wbf[...])
    if mode == "rope":
        cosf = cs_ref[:, :HEAD_DIM]
        sinf = cs_ref[:, HEAD_DIM:]
        for h in range(acc.shape[1] // HEAD_DIM):
            sl = slice(h * HEAD_DIM, (h + 1) * HEAD_DIM)
            o_ref[:, sl] = (_rope_tile(acc[:, sl], cosf, sinf) * scale).astype(o_ref.dtype)
    elif mode == "sigmoid":
        o_ref[...] = _sigmoid(acc).astype(o_ref.dtype)
    else:
        o_ref[...] = acc.astype(o_ref.dtype)


def _proj(x, w_t, cs, *, col0, n_col, mode, scale=1.0, out_dtype=f32, tm, tn):
    M, K = x.shape
    return pl.pallas_call(
        functools.partial(_proj_kernel, mode=mode, scale=scale),
        grid=(n_col, M // tm),
        in_specs=[pl.BlockSpec((tm, K), lambda j, i: (i, 0)),
                  pl.BlockSpec((tn, K), lambda j, i: (col0 + j, 0)),
                  pl.BlockSpec((tm, 2 * HEAD_DIM), lambda j, i: (i, 0))],
        out_specs=pl.BlockSpec((tm, tn), lambda j, i: (i, j)),
        out_shape=jax.ShapeDtypeStruct((M, n_col * tn), out_dtype),
        scratch_shapes=[pltpu.VMEM((K, tn), bf16)],
        compiler_params=_cparams(("parallel", "arbitrary")),
        name=f"proj_{mode}",
    )(x, w_t, cs)


def _proj_kv_kernel(x_ref, w_ref, cs_ref, st_ref, bf_ref, wbf, *, G):
    j = pl.program_id(0)
    _cast_weights_once([w_ref], [wbf], transpose=True)
    acc = _dot(x_ref[...], wbf[...])
    tm = acc.shape[0]

    def emit(rope):
        for g in range(G):
            y = acc[:, g * HEAD_DIM:(g + 1) * HEAD_DIM]
            if rope:
                y = _rope_tile(y, cs_ref[:, :HEAD_DIM], cs_ref[:, HEAD_DIM:])
            st_ref[pl.ds(g, tm, stride=G), :] = y
            bf_ref[:, g * HEAD_DIM:(g + 1) * HEAD_DIM] = y.astype(bf16)

    is_rope = functools.reduce(jnp.logical_or, [j == s for s in ROPE_KV_SEGS])
    pl.when(is_rope)(lambda: emit(True))
    pl.when(jnp.logical_not(is_rope))(lambda: emit(False))


def _proj_kv(x, w_t, cs, *, col0, n_seg, G, tm):
    M, K = x.shape
    N = G * HEAD_DIM
    return pl.pallas_call(
        functools.partial(_proj_kv_kernel, G=G),
        grid=(n_seg, M // tm),
        in_specs=[pl.BlockSpec((tm, K), lambda j, i: (i, 0)),
                  pl.BlockSpec((N, K), lambda j, i: (col0 + j, 0)),
                  pl.BlockSpec((tm, 2 * HEAD_DIM), lambda j, i: (i, 0))],
        out_specs=[pl.BlockSpec((None, tm * G, HEAD_DIM), lambda j, i: (j, i, 0)),
                   pl.BlockSpec((None, tm, N), lambda j, i: (j, i, 0))],
        out_shape=[jax.ShapeDtypeStruct((n_seg, M * G, HEAD_DIM), f32),
                   jax.ShapeDtypeStruct((n_seg, M, N), bf16)],
        scratch_shapes=[pltpu.VMEM((K, N), bf16)],
        compiler_params=_cparams(("parallel", "arbitrary")),
        name="proj_kv",
    )(x, w_t, cs)


def _gateup_kernel(x_ref, wg_ref, wu_ref, o_ref, wg_bf, wu_bf):
    _cast_weights_once([wg_ref, wu_ref], [wg_bf, wu_bf])
    x = x_ref[...]
    g = _dot(x, wg_bf[...])
    u = _dot(x, wu_bf[...])
    o_ref[...] = (g * _sigmoid(g) * u).astype(o_ref.dtype)


def _gateup(x, wg, wu, *, tm, tf):
    M, K = x.shape
    F = wg.shape[1]
    return pl.pallas_call(
        _gateup_kernel,
        grid=(F // tf, M // tm),
        in_specs=[pl.BlockSpec((tm, K), lambda j, i: (i, 0)),
                  pl.BlockSpec((K, tf), lambda j, i: (0, j)),
                  pl.BlockSpec((K, tf), lambda j, i: (0, j))],
        out_specs=pl.BlockSpec((tm, tf), lambda j, i: (i, j)),
        out_shape=jax.ShapeDtypeStruct((M, F), bf16),
        scratch_shapes=[pltpu.VMEM((K, tf), bf16)] * 2,
        compiler_params=_cparams(("parallel", "arbitrary")),
        name="ffn_gate_up",
    )(x, wg, wu)


def _resid_ln_kernel(*refs, n_pair, alpha, n_j, n_kseg, tn, emit_bf16):
    a_refs = refs[0:2 * n_pair:2]
    w_refs = refs[1:2 * n_pair:2]
    r_ref, g_ref, b_ref, o_ref = refs[2 * n_pair:2 * n_pair + 4]
    obf_ref = refs[2 * n_pair + 4] if emit_bf16 else None
    ybuf = refs[-1]
    k = pl.program_id(1)
    j = pl.program_id(2)

    def partial_product():
        acc = _dot(a_refs[0][...], w_refs[0][...])
        for a_ref, w_ref in zip(a_refs[1:], w_refs[1:]):
            acc = acc + _dot(a_ref[...], w_ref[...])
        return acc

    @pl.when(k == 0)
    def _():
        ybuf[j] = alpha * r_ref[...] + partial_product()

    if n_kseg > 1:
        @pl.when(k > 0)
        def _():
            ybuf[j] = ybuf[j] + partial_product()

    @pl.when((k == n_kseg - 1) & (j == n_j - 1))
    def _():
        n = n_j * tn
        tot = jnp.sum(ybuf[0], axis=-1, keepdims=True)
        for jj in range(1, n_j):
            tot = tot + jnp.sum(ybuf[jj], axis=-1, keepdims=True)
        mu = tot / n
        var = jnp.zeros_like(mu)
        for jj in range(n_j):
            d = ybuf[jj] - mu
            var = var + jnp.sum(d * d, axis=-1, keepdims=True)
        rstd = lax.rsqrt(var / n + LN_EPS)
        for jj in range(n_j):
            cols = slice(jj * tn, (jj + 1) * tn)
            out = (ybuf[jj] - mu) * rstd * g_ref[:, cols] + b_ref[:, cols]
            o_ref[:, cols] = out
            if emit_bf16:
                obf_ref[:, cols] = out.astype(bf16)


def _resid_ln(pairs, resid, gamma, beta, *, alpha, tm, tn, n_kseg, emit_bf16):
    M, N = resid.shape
    n_j = N // tn
    in_specs, args = [], []
    for a, w in pairs:
        ks = a.shape[1] // n_kseg
        assert ks * n_kseg == a.shape[1] and ks % LANES == 0
        in_specs += [pl.BlockSpec((tm, ks), lambda i, k, j: (i, k)),
                     pl.BlockSpec((ks, tn), lambda i, k, j: (k, j))]
        args += [a, w]
    in_specs += [pl.BlockSpec((tm, tn), lambda i, k, j: (i, jnp.where(k == 0, j, n_j - 1))),
                 pl.BlockSpec((1, N), lambda i, k, j: (0, 0)),
                 pl.BlockSpec((1, N), lambda i, k, j: (0, 0))]
    out_dtypes = [f32, bf16] if emit_bf16 else [f32]
    return pl.pallas_call(
        functools.partial(_resid_ln_kernel, n_pair=len(pairs), alpha=alpha, n_j=n_j, n_kseg=n_kseg, tn=tn,
                          emit_bf16=emit_bf16),
        grid=(M // tm, n_kseg, n_j),
        in_specs=in_specs,
        out_specs=[pl.BlockSpec((tm, N), lambda i, k, j: (i, 0), pipeline_mode=pl.Buffered(1))
                   for _ in out_dtypes],
        out_shape=[jax.ShapeDtypeStruct((M, N), dt) for dt in out_dtypes],
        scratch_shapes=[pltpu.VMEM((n_j, tm, tn), f32)],
        compiler_params=_cparams(("parallel", "arbitrary", "arbitrary")),
        name="matmul_resid_layernorm",
    )(*args, resid, gamma, beta)


def _pool_kernel(hist_ref, u_ref, wp_ref, sc_ref, o_ref, buf, *, tm, pos0):
    i = pl.program_id(1)
    H = POOL_MAX

    @pl.when(i == 0)
    def _():
        buf[0:H, :] = hist_ref[...]

    @pl.when(i > 0)
    def _():
        buf[0:H, :] = buf[tm:tm + H, :]

    buf[H:H + tm, :] = u_ref[...]
    pg = wp_ref.shape[1]
    pos = pos0 + i * tm + lax.broadcasted_iota(i32, (tm, 1), 0)
    for g, w in enumerate(POOL_WINDOWS):
        cols = slice(g * pg, (g + 1) * pg)
        cur = buf[H:H + tm, cols]
        tot = cur
        for k in range(1, w):
            tot = tot + buf[H - k:H - k + tm, cols]
        cnt = jnp.minimum(pos + 1, w).astype(f32)
        d = tot / cnt - cur
        y = _dot(d.astype(bf16), wp_ref[g]) * sc_ref[:, cols]
        o_ref[:, cols] = y.astype(o_ref.dtype)


def _pool_mix(hist, u, w_pool, scale, *, nb, n, pos0, tm, out_dtype):
    W = u.shape[1]
    nt = n // tm
    return pl.pallas_call(
        functools.partial(_pool_kernel, tm=tm, pos0=pos0),
        grid=(nb, nt),
        in_specs=[pl.BlockSpec((None, POOL_MAX, W), lambda b, i: (b, 0, 0)),
                  pl.BlockSpec((tm, W), lambda b, i: (b * nt + i, 0)),
                  pl.BlockSpec(w_pool.shape, lambda b, i: (0, 0, 0)),
                  pl.BlockSpec((1, W), lambda b, i: (0, 0))],
        out_specs=pl.BlockSpec((tm, W), lambda b, i: (b * nt + i, 0)),
        out_shape=jax.ShapeDtypeStruct((nb * n, W), out_dtype),
        scratch_shapes=[pltpu.VMEM((POOL_MAX + tm, W), f32)],
        compiler_params=_cparams(("parallel", "arbitrary")),
        name="pool_mix",
    )(hist, u, w_pool, scale)


def _cmp_part_kernel(pt_ref, *refs, n_pg, G):
    pages = refs[:n_pg]
    w_ref, o_ref, y_scr = refs[n_pg:]
    cpp = pages[0].shape[0]
    tok_per_tile = SUBLANES // G
    n_tile = n_pg * cpp
    rows = n_tile * SUBLANES
    width = w_ref.shape[2]
    row_in_tile = lax.broadcasted_iota(i32, (rows, HEAD_DIM), 0) & (SUBLANES - 1)
    acc = None
    for pp in range(CMP_STRIDE // tok_per_tile):
        x = jnp.concatenate([pages[pg][:, pp * SUBLANES:(pp + 1) * SUBLANES, :] for pg in range(n_pg)], axis=0)
        x = x.reshape(rows, HEAD_DIM)
        routed = [jnp.where((row_in_tile >= t * G) & (row_in_tile < (t + 1) * G), x, 0.0)
                  for t in range(tok_per_tile)]
        d = _dot(jnp.concatenate(routed, axis=1).astype(bf16), w_ref[pp])
        acc = d if acc is None else acc + d
    y = acc
    for t in range(1, tok_per_tile):
        y = y + pltpu.roll(acc, rows - t * G, axis=0)
    y_scr[...] = y.reshape(n_tile, SUBLANES, width)
    for g in range(G):
        o_ref[g] = y_scr[:, g, :]


def _cmp_part(pages, page_table, w1cat, *, G, n_pg):
    nb, n_pages = page_table.shape
    assert SUBLANES % G == 0
    chunk_rows = CMP_STRIDE * G
    cpp = pages.shape[1] // chunk_rows
    n_chunk = n_pages * cpp
    n_tile = n_pg * cpp
    width = w1cat.shape[1]
    tok_per_tile = SUBLANES // G
    pages = pages.reshape(pages.shape[0], cpp, chunk_rows, HEAD_DIM)
    w_tiles = w1cat.reshape(CMP_STRIDE // tok_per_tile, tok_per_tile * HEAD_DIM, width)

    def page_spec(pg):
        return pl.BlockSpec((None, cpp, chunk_rows, HEAD_DIM), lambda b, j, pt: (pt[b, j * n_pg + pg], 0, 0, 0))

    grid_spec = pltpu.PrefetchScalarGridSpec(
        num_scalar_prefetch=1,
        grid=(nb, n_pages // n_pg),
        in_specs=[page_spec(pg) for pg in range(n_pg)]
        + [pl.BlockSpec(w_tiles.shape, lambda b, j, pt: (0, 0, 0))],
        out_specs=pl.BlockSpec((None, G, n_tile, width), lambda b, j, pt: (b, 0, j, 0)),
        scratch_shapes=[pltpu.VMEM((n_tile, SUBLANES, width), f32)],
    )
    return pl.pallas_call(
        functools.partial(_cmp_part_kernel, n_pg=n_pg, G=G),
        grid_spec=grid_spec,
        out_shape=jax.ShapeDtypeStruct((nb, G, n_chunk, width), f32),
        compiler_params=_cparams(("parallel", "arbitrary")),
        name="cmp_part",
    )(page_table, *([pages] * n_pg), w_tiles)


def _gelu_tanh(x):
    return 0.5 * x * (1.0 + jnp.tanh(math.sqrt(2.0 / math.pi) * (x + 0.044715 * (x * x * x))))


def _cmp_finish_kernel(part_ref, pe_ref, w1_ref, w2_ref, cs_ref, o_ref, *, rope, transposed):
    G, n_chunk, width = part_ref.shape
    hid0 = _dot(pe_ref[...].astype(bf16), w1_ref[...])[0:1]
    part = part_ref[...].reshape(G * n_chunk, width)
    a = part[:, :HEAD_DIM]
    b = pltpu.roll(part[:, HEAD_DIM:], G * n_chunk - 1, axis=0)
    y = _dot(_gelu_tanh(hid0 + a + b).astype(bf16), w2_ref[...])
    for g in range(G):
        y_g = y[g * n_chunk:(g + 1) * n_chunk]
        if rope:
            y_g = _rope_tile(y_g, cs_ref[:, :HEAD_DIM], cs_ref[:, HEAD_DIM:])
        o_ref[g] = (y_g.T if transposed else y_g).astype(o_ref.dtype)


def _cmp_finish(part, pe, w1flat, w2, cs_cmp, *, rope, transposed):
    nb, G, n_chunk, _ = part.shape
    out_dims = (HEAD_DIM, n_chunk) if transposed else (n_chunk, HEAD_DIM)
    return pl.pallas_call(
        functools.partial(_cmp_finish_kernel, rope=rope, transposed=transposed),
        grid=(nb,),
        in_specs=[pl.BlockSpec((None, G, n_chunk, CMP_RATIO * HEAD_DIM), lambda b: (b, 0, 0, 0)),
                  pl.BlockSpec(pe.shape, lambda b: (0, 0)),
                  pl.BlockSpec(w1flat.shape, lambda b: (0, 0)),
                  pl.BlockSpec(w2.shape, lambda b: (0, 0)),
                  pl.BlockSpec(cs_cmp.shape, lambda b: (0, 0))],
        out_specs=pl.BlockSpec((None, G) + out_dims, lambda b: (b, 0, 0, 0)),
        out_shape=jax.ShapeDtypeStruct((nb, G) + out_dims, bf16),
        compiler_params=_cparams(("parallel",)),
        name="cmp_finish",
    )(part, pe, w1flat, w2, cs_cmp)


def _softmax_terms(s_biased, floor=None):
    m = jnp.max(s_biased, axis=-1, keepdims=True)
    if floor is not None:
        m = jnp.maximum(m, floor)
    e = jnp.exp(s_biased - m)
    return e, 1.0 / jnp.maximum(jnp.sum(e, axis=-1, keepdims=True), 1e-30)


def _dot_split3(p, w01):
    hi = p.astype(bf16)
    r1 = p - hi.astype(f32)
    mid = r1.astype(bf16)
    lo = (r1 - mid.astype(f32)).astype(bf16)
    return _dot(hi, w01) + _dot(mid, w01) + _dot(lo, w01)


def _overlap01(n_chunk, n_cmp, ns):
    n = lax.broadcasted_iota(i32, (n_chunk, ns), 0)
    s = lax.broadcasted_iota(i32, (n_chunk, ns), 1)
    ov = ((n * CMP_STRIDE < s * SLC_LEN + SLC_LEN) & (n * CMP_STRIDE + CMP_LEN - 1 >= s * SLC_LEN)
          & (n < n_cmp))
    return jnp.where(ov, 1.0, 0.0).astype(bf16)


def _cmp_bias(q_pos, n_chunk, n_cmp):
    n_io = lax.broadcasted_iota(i32, (q_pos.shape[0], n_chunk), 1)
    return jnp.where((n_io * CMP_STRIDE + CMP_LEN - 1 <= q_pos) & (n_io < n_cmp), 0.0, NEG)


def _selection_values(imp, s_io, cur):
    forced = (s_io == 0) | (s_io == cur) | (s_io == cur - 1)
    return jnp.where(forced, jnp.inf, jnp.where(s_io <= cur, imp, -jnp.inf))


def _ahead(other, val, other_first):
    return (other > val) | ((other == val) & other_first)


def _online_step(s, v, m_old, l_old, acc_old):
    m_new = jnp.maximum(m_old, jnp.max(s, axis=-1, keepdims=True))
    alpha = jnp.exp(m_old - m_new)
    e = jnp.exp(s - m_new)
    l_new = alpha * l_old + jnp.sum(e, axis=-1, keepdims=True)
    acc_new = alpha * acc_old + _dot(e.astype(bf16), v)
    return m_new, l_new, acc_new


def _prompt_attn_kernel(q_ref, gt_ref, kcb_ref, vcbt_ref, ks_ref, vs_ref, kw_ref, vw_ref, oh_ref, o_ref,
                        kaug, vst_ref, vwt_ref, qaug, val_scr, cnt_scr, acc_scr, *, R, n_cmp, n_slc, kc):
    c = pl.program_id(2)
    tq = Q_TILE
    nq = R * tq
    n_chunk = kcb_ref.shape[0]
    ns = oh_ref.shape[1]
    nr = val_scr.shape[0]
    win_keys = WINDOW + tq

    def transposed(tile):
        return tile.astype(f32).T.astype(bf16)

    @pl.when(c == 0)
    def _():
        kaug[:, :HEAD_DIM] = ks_ref[...]
        kaug[:, HEAD_DIM:] = oh_ref[...]
        for t in range(vst_ref.shape[0]):
            vst_ref[t] = transposed(vs_ref[t * tq:(t + 1) * tq, :])
            vwt_ref[t] = transposed(vw_ref[t * tq:(t + 1) * tq, :])

    gtt = gt_ref[...].T

    def all_heads(a):
        return jnp.concatenate([a] * R, axis=1)

    def weighted_values(v_t, e):
        return _dot(v_t, e.astype(bf16)), 1.0 / jnp.maximum(jnp.sum(e, axis=0, keepdims=True), 1e-30)

    for r in range(R):
        qaug[0:HEAD_DIM, r * tq:(r + 1) * tq] = transposed(q_ref[:, r * HEAD_DIM:(r + 1) * HEAD_DIM])
    q_t = qaug[0:HEAD_DIM, :]
    q_pos = c * tq + lax.broadcasted_iota(i32, (1, tq), 1)

    n_io = lax.broadcasted_iota(i32, (n_chunk, tq), 0)
    bias_c = jnp.where((n_io * CMP_STRIDE + CMP_LEN - 1 <= q_pos) & (n_io < n_cmp), 0.0, NEG)
    s_c = _dot(kcb_ref[...], q_t) + all_heads(bias_c)
    e_c = jnp.exp(s_c - jnp.maximum(jnp.max(s_c, axis=0, keepdims=True), 0.5 * NEG))
    o_c, inv_c = weighted_values(vcbt_ref[...], e_c)
    o_c = o_c * inv_c
    p_c = e_c * inv_c
    p_sum = p_c[:, 0:tq]
    for r in range(1, R):
        p_sum = p_sum + p_c[:, r * tq:(r + 1) * tq]

    blk = lax.broadcasted_iota(i32, (nr, n_chunk), 0)
    n_col = lax.broadcasted_iota(i32, (nr, n_chunk), 1)
    ov = jnp.where((n_col * CMP_STRIDE < blk * SLC_LEN + SLC_LEN)
                   & (n_col * CMP_STRIDE + CMP_LEN - 1 >= blk * SLC_LEN) & (n_col < n_cmp), 1.0, 0.0).astype(bf16)
    hi = p_sum.astype(bf16)
    r1 = p_sum - hi.astype(f32)
    mid = r1.astype(bf16)
    lo = (r1 - mid.astype(f32)).astype(bf16)
    imp_t = _dot(ov, hi) + _dot(ov, mid) + _dot(ov, lo)
    s_io = lax.broadcasted_iota(i32, (nr, tq), 0)
    cur = _div_pow2(c * tq + lax.broadcasted_iota(i32, (nr, tq), 1), SLC_LEN)
    val_scr[...] = _selection_values(imp_t, s_io, cur)
    cnt_scr[...] = jnp.zeros(cnt_scr.shape, f32)
    n_live = _div_pow2(c * tq + tq - 1, SLC_LEN) + 1
    for grp in range(nr // SUBLANES):
        @pl.when(grp * SUBLANES < n_live)
        def _():
            val = val_scr[...]
            cnt = cnt_scr[...]
            for sp in range(grp * SUBLANES, min((grp + 1) * SUBLANES, n_slc)):
                cnt = cnt + jnp.where(_ahead(val_scr[sp:sp + 1, :], val, s_io > sp), 1.0, 0.0)
            cnt_scr[...] = cnt

    first_blk = _div_pow2(c * tq, SLC_LEN)
    keep = (cnt_scr[...] < N_SEL) & (val_scr[...] > -jnp.inf) & (s_io < first_blk)
    mask_t = jnp.where(keep, 0.0, NEG)
    if nr < ns:
        mask_t = jnp.concatenate([mask_t, jnp.full((ns - nr, tq), NEG, f32)], axis=0)
    qaug[HEAD_DIM:, :] = all_heads(mask_t.astype(bf16))

    d0 = pl.multiple_of(c * tq, tq)
    tri = jnp.where(lax.broadcasted_iota(i32, (tq, tq), 0) <= lax.broadcasted_iota(i32, (tq, tq), 1), 0.0, NEG)
    s_d = _dot(ks_ref[pl.ds(d0, tq), :], q_t) + all_heads(tri)
    m_d = jnp.max(s_d, axis=0, keepdims=True)
    e_d = jnp.exp(s_d - m_d)
    l_d = jnp.sum(e_d, axis=0, keepdims=True)
    acc_scr[...] = _dot(vst_ref[c], e_d.astype(bf16))

    tiles_per_chunk = kc // tq

    def sweep(j, carry):
        m_old, l_old = carry
        k0 = pl.multiple_of(j * kc, kc)
        s = _dot(kaug[pl.ds(k0, kc), :], qaug[...])
        m_new = jnp.maximum(m_old, jnp.max(s, axis=0, keepdims=True))
        alpha = jnp.exp(m_old - m_new)
        e = jnp.exp(s - m_new)
        v_t = jnp.concatenate([vst_ref[j * tiles_per_chunk + i] for i in range(tiles_per_chunk)], axis=1)
        acc_scr[...] = alpha * acc_scr[...] + _dot(v_t, e.astype(bf16))
        return m_new, alpha * l_old + jnp.sum(e, axis=0, keepdims=True)

    _, l_s = lax.fori_loop(0, (c * tq + kc - 1) // kc, sweep, (m_d, l_d))
    o_s = acc_scr[...] * (1.0 / jnp.maximum(l_s, 1e-30))

    w_tile = jnp.maximum(c - WINDOW // tq, 0)
    w0 = pl.multiple_of(w_tile * tq, tq)
    key_w = w0 + lax.broadcasted_iota(i32, (win_keys, tq), 0)
    bias_w = jnp.where((key_w <= q_pos) & (q_pos - key_w < WINDOW), 0.0, NEG)
    s_w = _dot(kw_ref[pl.ds(w0, win_keys), :], q_t) + all_heads(bias_w)
    vw_t = jnp.concatenate([vwt_ref[w_tile + i] for i in range(win_keys // tq)], axis=1)
    o_w, inv_w = weighted_values(vw_t, jnp.exp(s_w - jnp.max(s_w, axis=0, keepdims=True)))
    o_w = o_w * inv_w

    for r in range(R):
        cols = slice(r * tq, (r + 1) * tq)
        g = [gtt[r * N_BRANCH + br:r * N_BRANCH + br + 1, :] for br in range(N_BRANCH)]
        o_t = g[0] * o_c[:, cols] + g[1] * o_s[:, cols] + g[2] * o_w[:, cols]
        o_ref[:, r * HEAD_DIM:(r + 1) * HEAD_DIM] = o_t.T.astype(o_ref.dtype)


def _prompt_attn(q, gates, kcb, vcb_t, ks, vs, kw, vw, onehot, *, B, T, G, R):
    n_chunk = kcb.shape[2]
    n_cmp = n_chunk - CMP_RATIO + 1
    n_slc = pl.cdiv(T, SLC_LEN)
    ns = onehot.shape[1]
    nr = _round_up(n_slc, SUBLANES)
    kc = _pick_tile(T, 1024, Q_TILE)
    nt = T // Q_TILE
    assert T % Q_TILE == 0 and T >= WINDOW + Q_TILE and Q_TILE % SLC_LEN == 0 and Q_TILE <= 2 * SLC_LEN
    seq_spec = pl.BlockSpec((T, HEAD_DIM), lambda b, g, c: (b, g))
    return pl.pallas_call(
        functools.partial(_prompt_attn_kernel, R=R, n_cmp=n_cmp, n_slc=n_slc, kc=kc),
        grid=(B, G, nt),
        in_specs=[pl.BlockSpec((Q_TILE, R * HEAD_DIM), lambda b, g, c: (b * nt + c, g)),
                  pl.BlockSpec((Q_TILE, LANES), lambda b, g, c: (b * nt + c, g)),
                  pl.BlockSpec((None, None, n_chunk, HEAD_DIM), lambda b, g, c: (b, g, 0, 0)),
                  pl.BlockSpec((None, None, HEAD_DIM, n_chunk), lambda b, g, c: (b, g, 0, 0)),
                  seq_spec, seq_spec, seq_spec, seq_spec,
                  pl.BlockSpec((T, ns), lambda b, g, c: (0, 0))],
        out_specs=pl.BlockSpec((Q_TILE, R * HEAD_DIM), lambda b, g, c: (b * nt + c, g)),
        out_shape=jax.ShapeDtypeStruct((B * T, G * R * HEAD_DIM), bf16),
        scratch_shapes=[pltpu.VMEM((T, HEAD_DIM + ns), bf16),
                        pltpu.VMEM((nt, HEAD_DIM, Q_TILE), bf16), pltpu.VMEM((nt, HEAD_DIM, Q_TILE), bf16),
                        pltpu.VMEM((HEAD_DIM + ns, R * Q_TILE), bf16),
                        pltpu.VMEM((nr, Q_TILE), f32), pltpu.VMEM((nr, Q_TILE), f32),
                        pltpu.VMEM((HEAD_DIM, R * Q_TILE), f32)],
        compiler_params=_cparams(("parallel", "parallel", "arbitrary")),
        name="nsa_prompt",
    )(q, gates, kcb, vcb_t, ks, vs, kw, vw, onehot)


def _sample_cmp_win_kernel(q_ref, gt_ref, kcb_ref, vcb_ref, kwin_ref, vwin_ref, kwn_ref, vwn_ref,
                           part_ref, qaug_ref, kbuf, vbuf, *, G, R, n_cmp, n_slc, past):
    n_chunk = kcb_ref.shape[1]
    ns = qaug_ref.shape[1] - G * HEAD_DIM
    wb = kwin_ref.shape[0] // G
    sp = SUBLANES
    rows = R * sp
    nbuf = kbuf.shape[0]
    q_pos = past + (lax.broadcasted_iota(i32, (rows, 1), 0) & (sp - 1))
    bias_c = _cmp_bias(q_pos, n_chunk, n_cmp)
    key_w = past - wb + lax.broadcasted_iota(i32, (rows, nbuf), 1)
    bias_w = jnp.where((key_w <= q_pos) & (q_pos - key_w < WINDOW) & (key_w >= 0), 0.0, NEG)
    ov = _overlap01(n_chunk, n_cmp, ns)
    s_io = lax.broadcasted_iota(i32, (sp, ns), 1)
    cur = _div_pow2(q_pos[0:sp], SLC_LEN)
    qaug_ref[:, :G * HEAD_DIM] = jnp.zeros((G * rows, G * HEAD_DIM), bf16)

    for g in range(G):
        q = q_ref[g]
        cols = slice(g * HEAD_DIM, (g + 1) * HEAD_DIM)

        e, inv = _softmax_terms(_dot_nt(q, kcb_ref[g]) + bias_c, floor=0.5 * NEG)
        p_c = e * inv
        o_c = _dot(p_c.astype(bf16), vcb_ref[g])
        p_sum = p_c[0:sp]
        for r in range(1, R):
            p_sum = p_sum + p_c[r * sp:(r + 1) * sp]
        val = _selection_values(_dot_split3(p_sum, ov), s_io, cur)
        cnt = jnp.zeros((sp, ns), f32)
        for blk in range(n_slc):
            cnt = cnt + jnp.where(_ahead(val[:, blk:blk + 1], val, s_io > blk), 1.0, 0.0)
        mask = jnp.where((cnt < N_SEL) & (val > -jnp.inf), 0.0, NEG).astype(bf16)

        qaug_ref[g * rows:(g + 1) * rows, cols] = q
        qaug_ref[g * rows:(g + 1) * rows, G * HEAD_DIM:] = jnp.concatenate([mask] * R, axis=0)

        kbuf[0:wb, :] = kwin_ref[pl.ds(g, wb, stride=G), :].astype(bf16)
        kbuf[wb:nbuf, :] = kwn_ref[:, cols]
        vbuf[0:wb, :] = vwin_ref[pl.ds(g, wb, stride=G), :].astype(bf16)
        vbuf[wb:nbuf, :] = vwn_ref[:, cols]
        e, inv = _softmax_terms(_dot_nt(q, kbuf[...]) + bias_w)
        o_w = _dot(e.astype(bf16), vbuf[...]) * inv
        part_ref[g] = gt_ref[g][:, 0:1] * o_c + gt_ref[g][:, 2:3] * o_w


def _sample_cmp_win(q, gates, kcb, vcb, kwin, vwin, kw_new, vw_new, *, past, n_slc, ns):
    DB, G, rows, _ = q.shape
    R = rows // SUBLANES
    n_chunk = kcb.shape[2]
    n_cmp = n_chunk - CMP_RATIO + 1
    wbg = kwin.shape[1]
    npad = kw_new.shape[1]
    qspec = pl.BlockSpec((None, G, rows, HEAD_DIM), lambda b: (b, 0, 0, 0))
    cspec = pl.BlockSpec((None, G, n_chunk, HEAD_DIM), lambda b: (b, 0, 0, 0))
    wspec = pl.BlockSpec((None, wbg, HEAD_DIM), lambda b: (b, 0, 0))
    nspec = pl.BlockSpec((None, npad, G * HEAD_DIM), lambda b: (b, 0, 0))
    aug_w = G * HEAD_DIM + ns
    return pl.pallas_call(
        functools.partial(_sample_cmp_win_kernel, G=G, R=R, n_cmp=n_cmp, n_slc=n_slc, past=past),
        grid=(DB,),
        in_specs=[qspec, qspec, cspec, cspec, wspec, wspec, nspec, nspec],
        out_specs=[qspec, pl.BlockSpec((None, G * rows, aug_w), lambda b: (b, 0, 0))],
        out_shape=[jax.ShapeDtypeStruct((DB, G, rows, HEAD_DIM), f32),
                   jax.ShapeDtypeStruct((DB, G * rows, aug_w), bf16)],
        scratch_shapes=[pltpu.VMEM((wbg // G + npad, HEAD_DIM), bf16)] * 2,
        compiler_params=_cparams(("parallel",)),
        name="nsa_sample_cmp_win",
    )(q, gates, kcb, vcb, kwin, vwin, kw_new, vw_new)


def _sample_sel_kernel(pt_ref, *refs, n_pg, G, R, page):
    kpages = refs[:n_pg]
    vpages = refs[n_pg:2 * n_pg]
    (oh_ref, ohn_ref, qaug_ref, gt_ref, part_ref, ksn_ref, vsn_ref, o_ref,
     kaug, vall, m_scr, l_scr, acc_scr) = refs[2 * n_pg:]
    j = pl.program_id(1)
    sp = SUBLANES
    rows = R * sp
    kvw = G * HEAD_DIM

    @pl.when(j == 0)
    def _():
        m_scr[...] = jnp.full(m_scr.shape, NEG, f32)
        l_scr[...] = jnp.zeros(l_scr.shape, f32)
        acc_scr[...] = jnp.zeros(acc_scr.shape, f32)

    def update(n_keys, bias):
        s = _dot_nt(qaug_ref[...], kaug[0:n_keys, :])
        if bias is not None:
            s = s + bias
        m, l, acc = _online_step(s, vall[0:n_keys, :], m_scr[...], l_scr[...], acc_scr[...])
        m_scr[...] = m
        l_scr[...] = l
        acc_scr[...] = acc

    for pg in range(n_pg):
        for g in range(G):
            kaug[pg * page:(pg + 1) * page, g * HEAD_DIM:(g + 1) * HEAD_DIM] = (
                kpages[pg][pl.ds(g, page, stride=G), :].astype(bf16))
            vall[pg * page:(pg + 1) * page, g * HEAD_DIM:(g + 1) * HEAD_DIM] = (
                vpages[pg][pl.ds(g, page, stride=G), :].astype(bf16))
    kaug[:, kvw:] = oh_ref[...]
    update(n_pg * page, None)

    @pl.when(j == pl.num_programs(1) - 1)
    def _():
        kaug[0:page, :kvw] = ksn_ref[...]
        kaug[0:page, kvw:] = ohn_ref[...]
        vall[0:page, :] = vsn_ref[...]
        step = lax.broadcasted_iota(i32, (G * rows, page), 0) & (sp - 1)
        causal = jnp.where(lax.broadcasted_iota(i32, (G * rows, page), 1) <= step, 0.0, NEG)
        update(page, causal)
        inv = 1.0 / jnp.maximum(l_scr[...], 1e-30)
        for g in range(G):
            rs = slice(g * rows, (g + 1) * rows)
            o_s = acc_scr[rs, g * HEAD_DIM:(g + 1) * HEAD_DIM] * inv[rs]
            o_ref[g] = part_ref[g] + gt_ref[g][:, 1:2] * o_s


def _sample_sel(kcache, vcache, page_table, onehot, qaug, gates, part, ks_new, vs_new, *, G, n_pg):
    DB, _, rows, _ = part.shape
    R = rows // SUBLANES
    n_pages = page_table.shape[1]
    page = kcache.shape[1] // G
    ns = onehot.shape[1]
    aug_w = G * HEAD_DIM + ns
    n_keys = n_pg * page

    def page_spec(pg):
        return pl.BlockSpec((None, page * G, HEAD_DIM), lambda b, j, pt: (pt[b, j * n_pg + pg], 0, 0))

    pspec = pl.BlockSpec((None, G, rows, HEAD_DIM), lambda b, j, pt: (b, 0, 0, 0))
    nspec = pl.BlockSpec((None, page, G * HEAD_DIM), lambda b, j, pt: (b, 0, 0))
    grid_spec = pltpu.PrefetchScalarGridSpec(
        num_scalar_prefetch=1,
        grid=(DB, n_pages // n_pg),
        in_specs=[page_spec(pg) for pg in range(n_pg)] * 2
        + [pl.BlockSpec((n_keys, ns), lambda b, j, pt: (j, 0)),
           pl.BlockSpec((page, ns), lambda b, j, pt: (n_pages, 0)),
           pl.BlockSpec((None, G * rows, aug_w), lambda b, j, pt: (b, 0, 0)),
           pspec, pspec, nspec, nspec],
        out_specs=pspec,
        scratch_shapes=[pltpu.VMEM((n_keys, aug_w), bf16), pltpu.VMEM((n_keys, G * HEAD_DIM), bf16),
                        pltpu.VMEM((G * rows, 1), f32), pltpu.VMEM((G * rows, 1), f32),
                        pltpu.VMEM((G * rows, G * HEAD_DIM), f32)],
    )
    return pl.pallas_call(
        functools.partial(_sample_sel_kernel, n_pg=n_pg, G=G, R=R, page=page),
        grid_spec=grid_spec,
        out_shape=jax.ShapeDtypeStruct(part.shape, f32),
        compiler_params=_cparams(("parallel", "arbitrary")),
        name="nsa_sample_sel",
    )(page_table, *([kcache] * n_pg), *([vcache] * n_pg), onehot, onehot, qaug, gates, part, ks_new, vs_new)


def _rope_table(pos):
    half = HEAD_DIM // 2
    inv = ROPE_THETA ** (-jnp.arange(half, dtype=f32) / half)
    ang = pos.astype(f32)[:, None] * inv[None, :]
    cos, sin = jnp.cos(ang), jnp.sin(ang)
    return jnp.concatenate([cos, cos, -sin, sin], axis=-1)


def _block_onehot(n_keys, ns):
    blk = jnp.arange(n_keys, dtype=i32)[:, None] // SLC_LEN
    return (blk == jnp.arange(ns, dtype=i32)[None, :]).astype(bf16)


def kernel(x_prompt, x_sample, cache_k_cmp, cache_v_cmp, cache_k_slc, cache_v_slc, state_k_win, state_v_win, state_pool, page_table, w_in, w_cmp1_k, pe_cmp_k, w_cmp2_k, w_cmp1_v, pe_cmp_v, w_cmp2_v, w_pool, pool_scale, w_o, ln1_g, ln1_b, w_gate, w_up, w_down, ln2_g, ln2_b):
    B, T, D = x_prompt.shape
    DB, S, _ = x_sample.shape
    depth, n_phys, page, G, _ = cache_k_cmp.shape
    assert depth == 1 and S <= SUBLANES
    n_pages = page_table.shape[1]
    past = n_pages * page
    PW = state_pool.shape[3]
    NW = D - PW
    NH = NW // HEAD_DIM
    R = NH // G
    KVW = G * HEAD_DIM
    F = w_gate.shape[2]
    wb = state_k_win.shape[2]
    wbp = min(WINDOW, T)
    alpha = (2 * depth) ** 0.25
    MP = B * T
    MS = DB * S
    M = MP + MS
    assert T % page == 0 and page % CMP_STRIDE == 0 and page % SLC_LEN == 0 and past % SLC_LEN == 0

    x_all = jnp.concatenate([x_prompt.reshape(MP, D), x_sample.reshape(MS, D)], axis=0)
    x_bf = x_all.astype(bf16)
    w_in_t = w_in[0].T
    o_kv = PW + NW
    o_g = o_kv + N_KV_SEG * KVW
    w_g_t = w_in_t[o_g:o_g + N_BRANCH * NH].reshape(G, R * N_BRANCH, D)
    w_g_t = jnp.pad(w_g_t, ((0, 0), (0, LANES - R * N_BRANCH), (0, 0))).reshape(G * LANES, D)

    pos_all = jnp.concatenate([jnp.tile(jnp.arange(T, dtype=i32), B),
                               jnp.tile(past + jnp.arange(S, dtype=i32), DB)])
    cs_all = _rope_table(pos_all)

    tm = _pick_tile(M, 640, SUBLANES)
    assert PW % KVW == 0 and NW % KVW == 0
    u = _proj(x_bf, w_in_t, cs_all, col0=0, n_col=PW // KVW, mode="none", tm=tm, tn=KVW)
    q = _proj(x_bf, w_in_t, cs_all, col0=PW // KVW, n_col=NW // KVW, mode="rope", scale=HEAD_DIM ** -0.5,
              out_dtype=bf16, tm=tm, tn=KVW)
    gates = _proj(x_bf, w_g_t, cs_all, col0=0, n_col=1, mode="sigmoid", tm=tm, tn=G * LANES)
    kv_st, kv_bf = _proj_kv(x_bf, w_in_t, cs_all, col0=o_kv // KVW, n_seg=N_KV_SEG, G=G, tm=tm)
    kc_st, vc_st = kv_st[0], kv_st[1]
    ks_bf, vs_bf, kw_bf, vw_bf = kv_bf[2], kv_bf[3], kv_bf[4], kv_bf[5]

    wp_bf = w_pool[0].astype(bf16)
    pscale = pool_scale[0].reshape(1, PW)
    pool_p = _pool_mix(jnp.zeros((B, POOL_MAX, PW), f32), u, wp_bf, pscale, nb=B, n=T, pos0=0,
                       tm=_pick_tile(T, 512, SUBLANES), out_dtype=bf16)
    u_s = u[MP:].reshape(DB, S, PW)
    hist_s = jnp.pad(state_pool[0], ((0, 0), (POOL_MAX - state_pool.shape[2], 0), (0, 0)))
    u_s_pad = jnp.pad(u_s, ((0, 0), (0, SUBLANES - S), (0, 0))).reshape(DB * SUBLANES, PW)
    pool_s = _pool_mix(hist_s, u_s_pad, wp_bf, pscale, nb=DB, n=SUBLANES, pos0=past, tm=SUBLANES,
                       out_dtype=f32).reshape(DB, SUBLANES, PW)[:, :S]

    def cmp_weights(w1, pe, w2):
        w1cat = w1[0].reshape(CMP_RATIO, CMP_STRIDE, HEAD_DIM, HEAD_DIM).transpose(1, 2, 0, 3)
        w1cat = w1cat.reshape(CMP_STRIDE * HEAD_DIM, CMP_RATIO * HEAD_DIM).astype(bf16)
        pe8 = jnp.broadcast_to(pe[0].reshape(1, CMP_LEN * HEAD_DIM), (SUBLANES, CMP_LEN * HEAD_DIM))
        return w1cat, pe8, w1[0].reshape(CMP_LEN * HEAD_DIM, HEAD_DIM).astype(bf16), w2[0].astype(bf16)

    cw_k = cmp_weights(w_cmp1_k, pe_cmp_k, w_cmp2_k)
    cw_v = cmp_weights(w_cmp1_v, pe_cmp_v, w_cmp2_v)

    def compress(pages, table, cw, rope, transposed=False):
        n_chunk = table.shape[1] * (page // CMP_STRIDE)
        cs_cmp = _rope_table(jnp.arange(n_chunk, dtype=i32) * CMP_STRIDE + CMP_LEN - 1)
        part = _cmp_part(pages, table, cw[0], G=G, n_pg=_pick_tile(table.shape[1], 16, 1))
        return _cmp_finish(part, cw[1], cw[2], cw[3], cs_cmp, rope=rope, transposed=transposed)

    def as_pages(a):
        return a.reshape(-1, page * G, HEAD_DIM)

    table_p = jnp.arange(MP // page, dtype=i32).reshape(B, T // page)
    kcb_p = compress(as_pages(kc_st[:MP * G]), table_p, cw_k, True)
    vcb_p_t = compress(as_pages(vc_st[:MP * G]), table_p, cw_v, False, transposed=True)
    kcb_s = compress(as_pages(cache_k_cmp.reshape(-1, HEAD_DIM)), page_table, cw_k, True)
    vcb_s = compress(as_pages(cache_v_cmp.reshape(-1, HEAD_DIM)), page_table, cw_v, False)

    ns_p = _round_up(pl.cdiv(T, SLC_LEN), LANES)
    nsa_p = _prompt_attn(q, gates, kcb_p, vcb_p_t, ks_bf, vs_bf, kw_bf, vw_bf, _block_onehot(T, ns_p),
                         B=B, T=T, G=G, R=R)

    def srow(a):
        return a[MP:].reshape(DB, S, a.shape[1])

    def to_rows(a):
        a = jnp.pad(a.transpose(0, 2, 3, 1, 4), ((0, 0), (0, 0), (0, 0), (0, SUBLANES - S), (0, 0)))
        return a.reshape(DB, G, R * SUBLANES, a.shape[-1])

    q_s = to_rows(srow(q).reshape(DB, S, G, R, HEAD_DIM))
    g_s = srow(gates).reshape(DB, S, G, LANES)[..., :R * N_BRANCH].reshape(DB, S, G, R, N_BRANCH)
    g_s = to_rows(jnp.pad(g_s, ((0, 0),) * 4 + ((0, LANES - N_BRANCH),)))

    def new_rows(a):
        return jnp.pad(srow(a), ((0, 0), (0, page - S), (0, 0)))

    n_slc_s = pl.cdiv(past + S, SLC_LEN)
    ns_s = _round_up(n_slc_s, LANES)
    part_s, qaug_s = _sample_cmp_win(q_s, g_s, kcb_s, vcb_s,
                                     state_k_win.reshape(DB, wb * G, HEAD_DIM),
                                     state_v_win.reshape(DB, wb * G, HEAD_DIM),
                                     new_rows(kw_bf), new_rows(vw_bf), past=past, n_slc=n_slc_s, ns=ns_s)
    o_s = _sample_sel(as_pages(cache_k_slc.reshape(-1, HEAD_DIM)), as_pages(cache_v_slc.reshape(-1, HEAD_DIM)),
                      page_table, _block_onehot(past + page, ns_s), qaug_s, g_s, part_s,
                      new_rows(ks_bf), new_rows(vs_bf), G=G, n_pg=_pick_tile(n_pages, 16, 1))
    nsa_s = o_s.reshape(DB, G, R, SUBLANES, HEAD_DIM)[:, :, :, :S].transpose(0, 3, 1, 2, 4)
    nsa_s = nsa_s.reshape(MS, NW).astype(bf16)

    pool_all = jnp.concatenate([pool_p, pool_s.reshape(MS, PW).astype(bf16)], axis=0)
    nsa_all = jnp.concatenate([nsa_p, nsa_s], axis=0)
    wo_bf = w_o[0].astype(bf16)
    h, h_bf = _resid_ln([(pool_all, wo_bf[:PW]), (nsa_all, wo_bf[PW:])], x_all,
                        ln1_g[0].reshape(1, D), ln1_b[0].reshape(1, D),
                        alpha=alpha, tm=tm, tn=_pick_tile(D, 512, LANES), n_kseg=1, emit_bf16=True)
    ff = _gateup(h_bf, w_gate[0], w_up[0], tm=tm, tf=_pick_tile(F, 512, LANES))
    n_kseg = 2 if F % (2 * LANES) == 0 else 1
    y, = _resid_ln([(ff, w_down[0].astype(bf16))], h, ln2_g[0].reshape(1, D), ln2_b[0].reshape(1, D),
                   alpha=alpha, tm=tm, tn=_pick_tile(D, 256, LANES), n_kseg=n_kseg, emit_bf16=False)

    def pstate(i):
        return kv_st[i, :MP * G].reshape(1, B, T, G, HEAD_DIM)

    def sstate(i):
        return kv_st[i, MP * G:].reshape(1, DB, S, G, HEAD_DIM)

    kw_p, vw_p = pstate(4)[:, :, T - wbp:], pstate(5)[:, :, T - wbp:]
    pool_state_p = u[:MP].reshape(1, B, T, PW)[:, :, T - (POOL_MAX - 1):]
    kw_s = jnp.concatenate([state_k_win, sstate(4)], axis=2)[:, :, S:]
    vw_s = jnp.concatenate([state_v_win, sstate(5)], axis=2)[:, :, S:]
    pool_state_s = jnp.concatenate([state_pool, u_s[None]], axis=2)[:, :, S:]
    return (y[:MP].reshape(B, T, D), y[MP:].reshape(DB, S, D),
            pstate(0), pstate(1), pstate(2), pstate(3), kw_p, vw_p, pool_state_p,
            sstate(0), sstate(1), sstate(2), sstate(3), kw_s, vw_s, pool_state_s)
```

```python
import functools
import math

import jax
import jax.numpy as jnp
from jax import lax
from jax.experimental import pallas as pl
from jax.experimental.pallas import tpu as pltpu

HEAD_DIM = 128
POOL_WINDOWS = (2, 4, 8, 16)
POOL_MAX = max(POOL_WINDOWS)
N_BRANCH = 3
N_KV_SEG = 6
ROPE_KV_SEGS = (2, 4)
CMP_LEN = 32
CMP_STRIDE = 16
CMP_RATIO = CMP_LEN // CMP_STRIDE
SLC_LEN = 64
N_SEL = 16
WINDOW = 512
Q_TILE = 128
ROPE_THETA = 10000.0
LN_EPS = 1e-5
LANES = 128
SUBLANES = 8
VMEM_LIMIT = 56 * 1024 * 1024
NEG = -1e30

f32 = jnp.float32
bf16 = jnp.bfloat16
i32 = jnp.int32


def _cparams(sem):
    return pltpu.CompilerParams(dimension_semantics=sem, vmem_limit_bytes=VMEM_LIMIT)


def _pick_tile(n, target, align):
    best = None
    for t in range(align, min(n, target) + 1, align):
        if n % t == 0:
            best = t
    if best is None:
        raise ValueError(f"no tile for {n} (target {target}, align {align})")
    return best


def _round_up(n, m):
    return m * pl.cdiv(n, m)


def _dot(a, b):
    return jnp.dot(a, b, preferred_element_type=f32)


def _dot_nt(a, b):
    return lax.dot_general(a, b, (((1,), (1,)), ((), ())), preferred_element_type=f32)


def _rope_tile(a, cosf, sinf):
    return a * cosf + pltpu.roll(a, HEAD_DIM // 2, axis=1) * sinf


def _div_pow2(x, d):
    assert d & (d - 1) == 0
    return lax.shift_right_arithmetic(x, d.bit_length() - 1)


def _sigmoid(x):
    return 1.0 / (1.0 + jnp.exp(-x))


def _cast_weights_once(w_refs, wbf_refs, transpose=False):
    @pl.when(pl.program_id(1) == 0)
    def _():
        for w_ref, wbf in zip(w_refs, wbf_refs):
            w = w_ref[...]
            wbf[...] = (w.T if transpose else w).astype(bf16)


def _proj_kernel(x_ref, w_ref, cs_ref, o_ref, wbf, *, mode, scale):
    _cast_weights_once([w_ref], [wbf], transpose=True)
    acc = _dot(x_ref[...], wbf[...])
    if mode == "rope":
        cosf = cs_ref[:, :HEAD_DIM]
        sinf = cs_ref[:, HEAD_DIM:]
        for h in range(acc.shape[1] // HEAD_DIM):
            sl = slice(h * HEAD_DIM, (h + 1) * HEAD_DIM)
            o_ref[:, sl] = (_rope_tile(acc[:, sl], cosf, sinf) * scale).astype(o_ref.dtype)
    elif mode == "sigmoid":
        o_ref[...] = _sigmoid(acc).astype(o_ref.dtype)
    else:
        o_ref[...] = acc.astype(o_ref.dtype)


def _proj(x, w_t, cs, *, col0, n_col, mode, scale=1.0, out_dtype=f32, tm, tn):
    M, K = x.shape
    return pl.pallas_call(
        functools.partial(_proj_kernel, mode=mode, scale=scale),
        grid=(n_col, M // tm),
        in_specs=[pl.BlockSpec((tm, K), lambda j, i: (i, 0)),
                  pl.BlockSpec((tn, K), lambda j, i: (col0 + j, 0)),
                  pl.BlockSpec((tm, 2 * HEAD_DIM), lambda j, i: (i, 0))],
        out_specs=pl.BlockSpec((tm, tn), lambda j, i: (i, j)),
        out_shape=jax.ShapeDtypeStruct((M, n_col * tn), out_dtype),
        scratch_shapes=[pltpu.VMEM((K, tn), bf16)],
        compiler_params=_cparams(("parallel", "arbitrary")),
        name=f"proj_{mode}",
    )(x, w_t, cs)


def _proj_kv_kernel(x_ref, w_ref, cs_ref, st_ref, bf_ref, wbf, *, G):
    j = pl.program_id(0)
    _cast_weights_once([w_ref], [wbf], transpose=True)
    acc = _dot(x_ref[...], wbf[...])
    tm = acc.shape[0]

    def emit(rope):
        for g in range(G):
            y = acc[:, g * HEAD_DIM:(g + 1) * HEAD_DIM]
            if rope:
                y = _rope_tile(y, cs_ref[:, :HEAD_DIM], cs_ref[:, HEAD_DIM:])
            st_ref[pl.ds(g, tm, stride=G), :] = y
            bf_ref[:, g * HEAD_DIM:(g + 1) * HEAD_DIM] = y.astype(bf16)

    is_rope = functools.reduce(jnp.logical_or, [j == s for s in ROPE_KV_SEGS])
    pl.when(is_rope)(lambda: emit(True))
    pl.when(jnp.logical_not(is_rope))(lambda: emit(False))


def _proj_kv(x, w_t, cs, *, col0, n_seg, G, tm):
    M, K = x.shape
    N = G * HEAD_DIM
    return pl.pallas_call(
        functools.partial(_proj_kv_kernel, G=G),
        grid=(n_seg, M // tm),
        in_specs=[pl.BlockSpec((tm, K), lambda j, i: (i, 0)),
                  pl.BlockSpec((N, K), lambda j, i: (col0 + j, 0)),
                  pl.BlockSpec((tm, 2 * HEAD_DIM), lambda j, i: (i, 0))],
        out_specs=[pl.BlockSpec((None, tm * G, HEAD_DIM), lambda j, i: (j, i, 0)),
                   pl.BlockSpec((None, tm, N), lambda j, i: (j, i, 0))],
        out_shape=[jax.ShapeDtypeStruct((n_seg, M * G, HEAD_DIM), f32),
                   jax.ShapeDtypeStruct((n_seg, M, N), bf16)],
        scratch_shapes=[pltpu.VMEM((K, N), bf16)],
        compiler_params=_cparams(("parallel", "arbitrary")),
        name="proj_kv",
    )(x, w_t, cs)


def _gateup_kernel(x_ref, wg_ref, wu_ref, o_ref, wg_bf, wu_bf):
    _cast_weights_once([wg_ref, wu_ref], [wg_bf, wu_bf])
    x = x_ref[...]
    g = _dot(x, wg_bf[...])
    u = _dot(x, wu_bf[...])
    o_ref[...] = (g * _sigmoid(g) * u).astype(o_ref.dtype)


def _gateup(x, wg, wu, *, tm, tf):
    M, K = x.shape
    F = wg.shape[1]
    return pl.pallas_call(
        _gateup_kernel,
        grid=(F // tf, M // tm),
        in_specs=[pl.BlockSpec((tm, K), lambda j, i: (i, 0)),
                  pl.BlockSpec((K, tf), lambda j, i: (0, j)),
                  pl.BlockSpec((K, tf), lambda j, i: (0, j))],
        out_specs=pl.BlockSpec((tm, tf), lambda j, i: (i, j)),
        out_shape=jax.ShapeDtypeStruct((M, F), bf16),
        scratch_shapes=[pltpu.VMEM((K, tf), bf16)] * 2,
        compiler_params=_cparams(("parallel", "arbitrary")),
        name="ffn_gate_up",
    )(x, wg, wu)


def _resid_ln_kernel(*refs, n_pair, alpha, n_j, n_kseg, tn, emit_bf16):
    a_refs = refs[0:2 * n_pair:2]
    w_refs = refs[1:2 * n_pair:2]
    r_ref, g_ref, b_ref, o_ref = refs[2 * n_pair:2 * n_pair + 4]
    obf_ref = refs[2 * n_pair + 4] if emit_bf16 else None
    ybuf = refs[-1]
    k = pl.program_id(1)
    j = pl.program_id(2)

    def partial_product():
        acc = _dot(a_refs[0][...], w_refs[0][...])
        for a_ref, w_ref in zip(a_refs[1:], w_refs[1:]):
            acc = acc + _dot(a_ref[...], w_ref[...])
        return acc

    @pl.when(k == 0)
    def _():
        ybuf[j] = alpha * r_ref[...] + partial_product()

    if n_kseg > 1:
        @pl.when(k > 0)
        def _():
            ybuf[j] = ybuf[j] + partial_product()

    @pl.when((k == n_kseg - 1) & (j == n_j - 1))
    def _():
        n = n_j * tn
        tot = jnp.sum(ybuf[0], axis=-1, keepdims=True)
        for jj in range(1, n_j):
            tot = tot + jnp.sum(ybuf[jj], axis=-1, keepdims=True)
        mu = tot / n
        var = jnp.zeros_like(mu)
        for jj in range(n_j):
            d = ybuf[jj] - mu
            var = var + jnp.sum(d * d, axis=-1, keepdims=True)
        rstd = lax.rsqrt(var / n + LN_EPS)
        for jj in range(n_j):
            cols = slice(jj * tn, (jj + 1) * tn)
            out = (ybuf[jj] - mu) * rstd * g_ref[:, cols] + b_ref[:, cols]
            o_ref[:, cols] = out
            if emit_bf16:
                obf_ref[:, cols] = out.astype(bf16)


def _resid_ln(pairs, resid, gamma, beta, *, alpha, tm, tn, n_kseg, emit_bf16):
    M, N = resid.shape
    n_j = N // tn
    in_specs, args = [], []
    for a, w in pairs:
        ks = a.shape[1] // n_kseg
        assert ks * n_kseg == a.shape[1] and ks % LANES == 0
        in_specs += [pl.BlockSpec((tm, ks), lambda i, k, j: (i, k)),
                     pl.BlockSpec((ks, tn), lambda i, k, j: (k, j))]
        args += [a, w]
    in_specs += [pl.BlockSpec((tm, tn), lambda i, k, j: (i, jnp.where(k == 0, j, n_j - 1))),
                 pl.BlockSpec((1, N), lambda i, k, j: (0, 0)),
                 pl.BlockSpec((1, N), lambda i, k, j: (0, 0))]
    out_dtypes = [f32, bf16] if emit_bf16 else [f32]
    return pl.pallas_call(
        functools.partial(_resid_ln_kernel, n_pair=len(pairs), alpha=alpha, n_j=n_j, n_kseg=n_kseg, tn=tn,
                          emit_bf16=emit_bf16),
        grid=(M // tm, n_kseg, n_j),
        in_specs=in_specs,
        out_specs=[pl.BlockSpec((tm, N), lambda i, k, j: (i, 0), pipeline_mode=pl.Buffered(1))
                   for _ in out_dtypes],
        out_shape=[jax.ShapeDtypeStruct((M, N), dt) for dt in out_dtypes],
        scratch_shapes=[pltpu.VMEM((n_j, tm, tn), f32)],
        compiler_params=_cparams(("parallel", "arbitrary", "arbitrary")),
        name="matmul_resid_layernorm",
    )(*args, resid, gamma, beta)


def _pool_kernel(hist_ref, u_ref, wp_ref, sc_ref, o_ref, buf, *, tm, pos0):
    i = pl.program_id(1)
    H = POOL_MAX

    @pl.when(i == 0)
    def _():
        buf[0:H, :] = hist_ref[...]

    @pl.when(i > 0)
    def _():
        buf[0:H, :] = buf[tm:tm + H, :]

    buf[H:H + tm, :] = u_ref[...]
    pg = wp_ref.shape[1]
    pos = pos0 + i * tm + lax.broadcasted_iota(i32, (tm, 1), 0)
    for g, w in enumerate(POOL_WINDOWS):
        cols = slice(g * pg, (g + 1) * pg)
        cur = buf[H:H + tm, cols]
        tot = cur
        for k in range(1, w):
            tot = tot + buf[H - k:H - k + tm, cols]
        cnt = jnp.minimum(pos + 1, w).astype(f32)
        d = tot / cnt - cur
        y = _dot(d.astype(bf16), wp_ref[g]) * sc_ref[:, cols]
        o_ref[:, cols] = y.astype(o_ref.dtype)


def _pool_mix(hist, u, w_pool, scale, *, nb, n, pos0, tm, out_dtype):
    W = u.shape[1]
    nt = n // tm
    return pl.pallas_call(
        functools.partial(_pool_kernel, tm=tm, pos0=pos0),
        grid=(nb, nt),
        in_specs=[pl.BlockSpec((None, POOL_MAX, W), lambda b, i: (b, 0, 0)),
                  pl.BlockSpec((tm, W), lambda b, i: (b * nt + i, 0)),
                  pl.BlockSpec(w_pool.shape, lambda b, i: (0, 0, 0)),
                  pl.BlockSpec((1, W), lambda b, i: (0, 0))],
        out_specs=pl.BlockSpec((tm, W), lambda b, i: (b * nt + i, 0)),
        out_shape=jax.ShapeDtypeStruct((nb * n, W), out_dtype),
        scratch_shapes=[pltpu.VMEM((POOL_MAX + tm, W), f32)],
        compiler_params=_cparams(("parallel", "arbitrary")),
        name="pool_mix",
    )(hist, u, w_pool, scale)


def _cmp_part_kernel(pt_ref, *refs, n_pg, G):
    pages = refs[:n_pg]
    w_ref, o_ref, y_scr = refs[n_pg:]
    cpp = pages[0].shape[0]
    tok_per_tile = SUBLANES // G
    n_tile = n_pg * cpp
    rows = n_tile * SUBLANES
    width = w_ref.shape[2]
    row_in_tile = lax.broadcasted_iota(i32, (rows, HEAD_DIM), 0) & (SUBLANES - 1)
    acc = None
    for pp in range(CMP_STRIDE // tok_per_tile):
        x = jnp.concatenate([pages[pg][:, pp * SUBLANES:(pp + 1) * SUBLANES, :] for pg in range(n_pg)], axis=0)
        x = x.reshape(rows, HEAD_DIM)
        routed = [jnp.where((row_in_tile >= t * G) & (row_in_tile < (t + 1) * G), x, 0.0)
                  for t in range(tok_per_tile)]
        d = _dot(jnp.concatenate(routed, axis=1).astype(bf16), w_ref[pp])
        acc = d if acc is None else acc + d
    y = acc
    for t in range(1, tok_per_tile):
        y = y + pltpu.roll(acc, rows - t * G, axis=0)
    y_scr[...] = y.reshape(n_tile, SUBLANES, width)
    for g in range(G):
        o_ref[g] = y_scr[:, g, :]


def _cmp_part(pages, page_table, w1cat, *, G, n_pg):
    nb, n_pages = page_table.shape
    assert SUBLANES % G == 0
    chunk_rows = CMP_STRIDE * G
    cpp = pages.shape[1] // chunk_rows
    n_chunk = n_pages * cpp
    n_tile = n_pg * cpp
    width = w1cat.shape[1]
    tok_per_tile = SUBLANES // G
    pages = pages.reshape(pages.shape[0], cpp, chunk_rows, HEAD_DIM)
    w_tiles = w1cat.reshape(CMP_STRIDE // tok_per_tile, tok_per_tile * HEAD_DIM, width)

    def page_spec(pg):
        return pl.BlockSpec((None, cpp, chunk_rows, HEAD_DIM), lambda b, j, pt: (pt[b, j * n_pg + pg], 0, 0, 0))

    grid_spec = pltpu.PrefetchScalarGridSpec(
        num_scalar_prefetch=1,
        grid=(nb, n_pages // n_pg),
        in_specs=[page_spec(pg) for pg in range(n_pg)]
        + [pl.BlockSpec(w_tiles.shape, lambda b, j, pt: (0, 0, 0))],
        out_specs=pl.BlockSpec((None, G, n_tile, width), lambda b, j, pt: (b, 0, j, 0)),
        scratch_shapes=[pltpu.VMEM((n_tile, SUBLANES, width), f32)],
    )
    return pl.pallas_call(
        functools.partial(_cmp_part_kernel, n_pg=n_pg, G=G),
        grid_spec=grid_spec,
        out_shape=jax.ShapeDtypeStruct((nb, G, n_chunk, width), f32),
        compiler_params=_cparams(("parallel", "arbitrary")),
        name="cmp_part",
    )(page_table, *([pages] * n_pg), w_tiles)


def _gelu_tanh(x):
    return 0.5 * x * (1.0 + jnp.tanh(math.sqrt(2.0 / math.pi) * (x + 0.044715 * (x * x * x))))


def _cmp_finish_kernel(part_ref, pe_ref, w1_ref, w2_ref, cs_ref, o_ref, *, rope, transposed):
    G, n_chunk, width = part_ref.shape
    hid0 = _dot(pe_ref[...].astype(bf16), w1_ref[...])[0:1]
    part = part_ref[...].reshape(G * n_chunk, width)
    a = part[:, :HEAD_DIM]
    b = pltpu.roll(part[:, HEAD_DIM:], G * n_chunk - 1, axis=0)
    y = _dot(_gelu_tanh(hid0 + a + b).astype(bf16), w2_ref[...])
    for g in range(G):
        y_g = y[g * n_chunk:(g + 1) * n_chunk]
        if rope:
            y_g = _rope_tile(y_g, cs_ref[:, :HEAD_DIM], cs_ref[:, HEAD_DIM:])
        o_ref[g] = (y_g.T if transposed else y_g).astype(o_ref.dtype)


def _cmp_finish(part, pe, w1flat, w2, cs_cmp, *, rope, transposed):
    nb, G, n_chunk, _ = part.shape
    out_dims = (HEAD_DIM, n_chunk) if transposed else (n_chunk, HEAD_DIM)
    return pl.pallas_call(
        functools.partial(_cmp_finish_kernel, rope=rope, transposed=transposed),
        grid=(nb,),
        in_specs=[pl.BlockSpec((None, G, n_chunk, CMP_RATIO * HEAD_DIM), lambda b: (b, 0, 0, 0)),
                  pl.BlockSpec(pe.shape, lambda b: (0, 0)),
                  pl.BlockSpec(w1flat.shape, lambda b: (0, 0)),
                  pl.BlockSpec(w2.shape, lambda b: (0, 0)),
                  pl.BlockSpec(cs_cmp.shape, lambda b: (0, 0))],
        out_specs=pl.BlockSpec((None, G) + out_dims, lambda b: (b, 0, 0, 0)),
        out_shape=jax.ShapeDtypeStruct((nb, G) + out_dims, bf16),
        compiler_params=_cparams(("parallel",)),
        name="cmp_finish",
    )(part, pe, w1flat, w2, cs_cmp)


def _softmax_terms(s_biased, floor=None):
    m = jnp.max(s_biased, axis=-1, keepdims=True)
    if floor is not None:
        m = jnp.maximum(m, floor)
    e = jnp.exp(s_biased - m)
    return e, 1.0 / jnp.maximum(jnp.sum(e, axis=-1, keepdims=True), 1e-30)


def _dot_split3(p, w01):
    hi = p.astype(bf16)
    r1 = p - hi.astype(f32)
    mid = r1.astype(bf16)
    lo = (r1 - mid.astype(f32)).astype(bf16)
    return _dot(hi, w01) + _dot(mid, w01) + _dot(lo, w01)


def _overlap01(n_chunk, n_cmp, ns):
    n = lax.broadcasted_iota(i32, (n_chunk, ns), 0)
    s = lax.broadcasted_iota(i32, (n_chunk, ns), 1)
    ov = ((n * CMP_STRIDE < s * SLC_LEN + SLC_LEN) & (n * CMP_STRIDE + CMP_LEN - 1 >= s * SLC_LEN)
          & (n < n_cmp))
    return jnp.where(ov, 1.0, 0.0).astype(bf16)


def _cmp_bias(q_pos, n_chunk, n_cmp):
    n_io = lax.broadcasted_iota(i32, (q_pos.shape[0], n_chunk), 1)
    return jnp.where((n_io * CMP_STRIDE + CMP_LEN - 1 <= q_pos) & (n_io < n_cmp), 0.0, NEG)


def _selection_values(imp, s_io, cur):
    forced = (s_io == 0) | (s_io == cur) | (s_io == cur - 1)
    return jnp.where(forced, jnp.inf, jnp.where(s_io <= cur, imp, -jnp.inf))


def _ahead(other, val, other_first):
    return (other > val) | ((other == val) & other_first)


def _online_step(s, v, m_old, l_old, acc_old):
    m_new = jnp.maximum(m_old, jnp.max(s, axis=-1, keepdims=True))
    alpha = jnp.exp(m_old - m_new)
    e = jnp.exp(s - m_new)
    l_new = alpha * l_old + jnp.sum(e, axis=-1, keepdims=True)
    acc_new = alpha * acc_old + _dot(e.astype(bf16), v)
    return m_new, l_new, acc_new


def _prompt_attn_kernel(q_ref, gt_ref, kcb_ref, vcbt_ref, ks_ref, vs_ref, kw_ref, vw_ref, oh_ref, o_ref,
                        kaug, vst_ref, vwt_ref, qaug, val_scr, cnt_scr, acc_scr, *, R, n_cmp, n_slc, kc):
    c = pl.program_id(2)
    tq = Q_TILE
    nq = R * tq
    n_chunk = kcb_ref.shape[0]
    ns = oh_ref.shape[1]
    nr = val_scr.shape[0]
    win_keys = WINDOW + tq

    def transposed(tile):
        return tile.astype(f32).T.astype(bf16)

    @pl.when(c == 0)
    def _():
        kaug[:, :HEAD_DIM] = ks_ref[...]
        kaug[:, HEAD_DIM:] = oh_ref[...]
        for t in range(vst_ref.shape[0]):
            vst_ref[t] = transposed(vs_ref[t * tq:(t + 1) * tq, :])
            vwt_ref[t] = transposed(vw_ref[t * tq:(t + 1) * tq, :])

    gtt = gt_ref[...].T

    def all_heads(a):
        return jnp.concatenate([a] * R, axis=1)

    def weighted_values(v_t, e):
        return _dot(v_t, e.astype(bf16)), 1.0 / jnp.maximum(jnp.sum(e, axis=0, keepdims=True), 1e-30)

    for r in range(R):
        qaug[0:HEAD_DIM, r * tq:(r + 1) * tq] = transposed(q_ref[:, r * HEAD_DIM:(r + 1) * HEAD_DIM])
    q_t = qaug[0:HEAD_DIM, :]
    q_pos = c * tq + lax.broadcasted_iota(i32, (1, tq), 1)

    n_io = lax.broadcasted_iota(i32, (n_chunk, tq), 0)
    bias_c = jnp.where((n_io * CMP_STRIDE + CMP_LEN - 1 <= q_pos) & (n_io < n_cmp), 0.0, NEG)
    s_c = _dot(kcb_ref[...], q_t) + all_heads(bias_c)
    e_c = jnp.exp(s_c - jnp.maximum(jnp.max(s_c, axis=0, keepdims=True), 0.5 * NEG))
    o_c, inv_c = weighted_values(vcbt_ref[...], e_c)
    o_c = o_c * inv_c
    p_c = e_c * inv_c
    p_sum = p_c[:, 0:tq]
    for r in range(1, R):
        p_sum = p_sum + p_c[:, r * tq:(r + 1) * tq]

    blk = lax.broadcasted_iota(i32, (nr, n_chunk), 0)
    n_col = lax.broadcasted_iota(i32, (nr, n_chunk), 1)
    ov = jnp.where((n_col * CMP_STRIDE < blk * SLC_LEN + SLC_LEN)
                   & (n_col * CMP_STRIDE + CMP_LEN - 1 >= blk * SLC_LEN) & (n_col < n_cmp), 1.0, 0.0).astype(bf16)
    hi = p_sum.astype(bf16)
    r1 = p_sum - hi.astype(f32)
    mid = r1.astype(bf16)
    lo = (r1 - mid.astype(f32)).astype(bf16)
    imp_t = _dot(ov, hi) + _dot(ov, mid) + _dot(ov, lo)
    s_io = lax.broadcasted_iota(i32, (nr, tq), 0)
    cur = _div_pow2(c * tq + lax.broadcasted_iota(i32, (nr, tq), 1), SLC_LEN)
    val_scr[...] = _selection_values(imp_t, s_io, cur)
    cnt_scr[...] = jnp.zeros(cnt_scr.shape, f32)
    n_live = _div_pow2(c * tq + tq - 1, SLC_LEN) + 1
    for grp in range(nr // SUBLANES):
        @pl.when(grp * SUBLANES < n_live)
        def _():
            val = val_scr[...]
            cnt = cnt_scr[...]
            for sp in range(grp * SUBLANES, min((grp + 1) * SUBLANES, n_slc)):
                cnt = cnt + jnp.where(_ahead(val_scr[sp:sp + 1, :], val, s_io > sp), 1.0, 0.0)
            cnt_scr[...] = cnt

    first_blk = _div_pow2(c * tq, SLC_LEN)
    keep = (cnt_scr[...] < N_SEL) & (val_scr[...] > -jnp.inf) & (s_io < first_blk)
    mask_t = jnp.where(keep, 0.0, NEG)
    if nr < ns:
        mask_t = jnp.concatenate([mask_t, jnp.full((ns - nr, tq), NEG, f32)], axis=0)
    qaug[HEAD_DIM:, :] = all_heads(mask_t.astype(bf16))

    d0 = pl.multiple_of(c * tq, tq)
    tri = jnp.where(lax.broadcasted_iota(i32, (tq, tq), 0) <= lax.broadcasted_iota(i32, (tq, tq), 1), 0.0, NEG)
    s_d = _dot(ks_ref[pl.ds(d0, tq), :], q_t) + all_heads(tri)
    m_d = jnp.max(s_d, axis=0, keepdims=True)
    e_d = jnp.exp(s_d - m_d)
    l_d = jnp.sum(e_d, axis=0, keepdims=True)
    acc_scr[...] = _dot(vst_ref[c], e_d.astype(bf16))

    tiles_per_chunk = kc // tq

    def sweep(j, carry):
        m_old, l_old = carry
        k0 = pl.multiple_of(j * kc, kc)
        s = _dot(kaug[pl.ds(k0, kc), :], qaug[...])
        m_new = jnp.maximum(m_old, jnp.max(s, axis=0, keepdims=True))
        alpha = jnp.exp(m_old - m_new)
        e = jnp.exp(s - m_new)
        v_t = jnp.concatenate([vst_ref[j * tiles_per_chunk + i] for i in range(tiles_per_chunk)], axis=1)
        acc_scr[...] = alpha * acc_scr[...] + _dot(v_t, e.astype(bf16))
        return m_new, alpha * l_old + jnp.sum(e, axis=0, keepdims=True)

    _, l_s = lax.fori_loop(0, (c * tq + kc - 1) // kc, sweep, (m_d, l_d))
    o_s = acc_scr[...] * (1.0 / jnp.maximum(l_s, 1e-30))

    w_tile = jnp.maximum(c - WINDOW // tq, 0)
    w0 = pl.multiple_of(w_tile * tq, tq)
    key_w = w0 + lax.broadcasted_iota(i32, (win_keys, tq), 0)
    bias_w = jnp.where((key_w <= q_pos) & (q_pos - key_w < WINDOW), 0.0, NEG)
    s_w = _dot(kw_ref[pl.ds(w0, win_keys), :], q_t) + all_heads(bias_w)
    vw_t = jnp.concatenate([vwt_ref[w_tile + i] for i in range(win_keys // tq)], axis=1)
    o_w, inv_w = weighted_values(vw_t, jnp.exp(s_w - jnp.max(s_w, axis=0, keepdims=True)))
    o_w = o_w * inv_w

    for r in range(R):
        cols = slice(r * tq, (r + 1) * tq)
        g = [gtt[r * N_BRANCH + br:r * N_BRANCH + br + 1, :] for br in range(N_BRANCH)]
        o_t = g[0] * o_c[:, cols] + g[1] * o_s[:, cols] + g[2] * o_w[:, cols]
        o_ref[:, r * HEAD_DIM:(r + 1) * HEAD_DIM] = o_t.T.astype(o_ref.dtype)


def _prompt_attn(q, gates, kcb, vcb_t, ks, vs, kw, vw, onehot, *, B, T, G, R):
    n_chunk = kcb.shape[2]
    n_cmp = n_chunk - CMP_RATIO + 1
    n_slc = pl.cdiv(T, SLC_LEN)
    ns = onehot.shape[1]
    nr = _round_up(n_slc, SUBLANES)
    kc = _pick_tile(T, 1024, Q_TILE)
    nt = T // Q_TILE
    assert T % Q_TILE == 0 and T >= WINDOW + Q_TILE and Q_TILE % SLC_LEN == 0 and Q_TILE <= 2 * SLC_LEN
    seq_spec = pl.BlockSpec((T, HEAD_DIM), lambda b, g, c: (b, g))
    return pl.pallas_call(
        functools.partial(_prompt_attn_kernel, R=R, n_cmp=n_cmp, n_slc=n_slc, kc=kc),
        grid=(B, G, nt),
        in_specs=[pl.BlockSpec((Q_TILE, R * HEAD_DIM), lambda b, g, c: (b * nt + c, g)),
                  pl.BlockSpec((Q_TILE, LANES), lambda b, g, c: (b * nt + c, g)),
                  pl.BlockSpec((None, None, n_chunk, HEAD_DIM), lambda b, g, c: (b, g, 0, 0)),
                  pl.BlockSpec((None, None, HEAD_DIM, n_chunk), lambda b, g, c: (b, g, 0, 0)),
                  seq_spec, seq_spec, seq_spec, seq_spec,
                  pl.BlockSpec((T, ns), lambda b, g, c: (0, 0))],
        out_specs=pl.BlockSpec((Q_TILE, R * HEAD_DIM), lambda b, g, c: (b * nt + c, g)),
        out_shape=jax.ShapeDtypeStruct((B * T, G * R * HEAD_DIM), bf16),
        scratch_shapes=[pltpu.VMEM((T, HEAD_DIM + ns), bf16),
                        pltpu.VMEM((nt, HEAD_DIM, Q_TILE), bf16), pltpu.VMEM((nt, HEAD_DIM, Q_TILE), bf16),
                        pltpu.VMEM((HEAD_DIM + ns, R * Q_TILE), bf16),
                        pltpu.VMEM((nr, Q_TILE), f32), pltpu.VMEM((nr, Q_TILE), f32),
                        pltpu.VMEM((HEAD_DIM, R * Q_TILE), f32)],
        compiler_params=_cparams(("parallel", "parallel", "arbitrary")),
        name="nsa_prompt",
    )(q, gates, kcb, vcb_t, ks, vs, kw, vw, onehot)


def _sample_cmp_win_kernel(q_ref, gt_ref, kcb_ref, vcb_ref, kwin_ref, vwin_ref, kwn_ref, vwn_ref,
                           part_ref, qaug_ref, kbuf, vbuf, *, G, R, n_cmp, n_slc, past):
    n_chunk = kcb_ref.shape[1]
    ns = qaug_ref.shape[1] - G * HEAD_DIM
    wb = kwin_ref.shape[0] // G
    sp = SUBLANES
    rows = R * sp
    nbuf = kbuf.shape[0]
    q_pos = past + (lax.broadcasted_iota(i32, (rows, 1), 0) & (sp - 1))
    bias_c = _cmp_bias(q_pos, n_chunk, n_cmp)
    key_w = past - wb + lax.broadcasted_iota(i32, (rows, nbuf), 1)
    bias_w = jnp.where((key_w <= q_pos) & (q_pos - key_w < WINDOW) & (key_w >= 0), 0.0, NEG)
    ov = _overlap01(n_chunk, n_cmp, ns)
    s_io = lax.broadcasted_iota(i32, (sp, ns), 1)
    cur = _div_pow2(q_pos[0:sp], SLC_LEN)
    qaug_ref[:, :G * HEAD_DIM] = jnp.zeros((G * rows, G * HEAD_DIM), bf16)

    for g in range(G):
        q = q_ref[g]
        cols = slice(g * HEAD_DIM, (g + 1) * HEAD_DIM)

        e, inv = _softmax_terms(_dot_nt(q, kcb_ref[g]) + bias_c, floor=0.5 * NEG)
        p_c = e * inv
        o_c = _dot(p_c.astype(bf16), vcb_ref[g])
        p_sum = p_c[0:sp]
        for r in range(1, R):
            p_sum = p_sum + p_c[r * sp:(r + 1) * sp]
        val = _selection_values(_dot_split3(p_sum, ov), s_io, cur)
        cnt = jnp.zeros((sp, ns), f32)
        for blk in range(n_slc):
            cnt = cnt + jnp.where(_ahead(val[:, blk:blk + 1], val, s_io > blk), 1.0, 0.0)
        mask = jnp.where((cnt < N_SEL) & (val > -jnp.inf), 0.0, NEG).astype(bf16)

        qaug_ref[g * rows:(g + 1) * rows, cols] = q
        qaug_ref[g * rows:(g + 1) * rows, G * HEAD_DIM:] = jnp.concatenate([mask] * R, axis=0)

        kbuf[0:wb, :] = kwin_ref[pl.ds(g, wb, stride=G), :].astype(bf16)
        kbuf[wb:nbuf, :] = kwn_ref[:, cols]
        vbuf[0:wb, :] = vwin_ref[pl.ds(g, wb, stride=G), :].astype(bf16)
        vbuf[wb:nbuf, :] = vwn_ref[:, cols]
        e, inv = _softmax_terms(_dot_nt(q, kbuf[...]) + bias_w)
        o_w = _dot(e.astype(bf16), vbuf[...]) * inv
        part_ref[g] = gt_ref[g][:, 0:1] * o_c + gt_ref[g][:, 2:3] * o_w


def _sample_cmp_win(q, gates, kcb, vcb, kwin, vwin, kw_new, vw_new, *, past, n_slc, ns):
    DB, G, rows, _ = q.shape
    R = rows // SUBLANES
    n_chunk = kcb.shape[2]
    n_cmp = n_chunk - CMP_RATIO + 1
    wbg = kwin.shape[1]
    npad = kw_new.shape[1]
    qspec = pl.BlockSpec((None, G, rows, HEAD_DIM), lambda b: (b, 0, 0, 0))
    cspec = pl.BlockSpec((None, G, n_chunk, HEAD_DIM), lambda b: (b, 0, 0, 0))
    wspec = pl.BlockSpec((None, wbg, HEAD_DIM), lambda b: (b, 0, 0))
    nspec = pl.BlockSpec((None, npad, G * HEAD_DIM), lambda b: (b, 0, 0))
    aug_w = G * HEAD_DIM + ns
    return pl.pallas_call(
        functools.partial(_sample_cmp_win_kernel, G=G, R=R, n_cmp=n_cmp, n_slc=n_slc, past=past),
        grid=(DB,),
        in_specs=[qspec, qspec, cspec, cspec, wspec, wspec, nspec, nspec],
        out_specs=[qspec, pl.BlockSpec((None, G * rows, aug_w), lambda b: (b, 0, 0))],
        out_shape=[jax.ShapeDtypeStruct((DB, G, rows, HEAD_DIM), f32),
                   jax.ShapeDtypeStruct((DB, G * rows, aug_w), bf16)],
        scratch_shapes=[pltpu.VMEM((wbg // G + npad, HEAD_DIM), bf16)] * 2,
        compiler_params=_cparams(("parallel",)),
        name="nsa_sample_cmp_win",
    )(q, gates, kcb, vcb, kwin, vwin, kw_new, vw_new)


def _sample_sel_kernel(pt_ref, *refs, n_pg, G, R, page):
    kpages = refs[:n_pg]
    vpages = refs[n_pg:2 * n_pg]
    (oh_ref, ohn_ref, qaug_ref, gt_ref, part_ref, ksn_ref, vsn_ref, o_ref,
     kaug, vall, m_scr, l_scr, acc_scr) = refs[2 * n_pg:]
    j = pl.program_id(1)
    sp = SUBLANES
    rows = R * sp
    kvw = G * HEAD_DIM

    @pl.when(j == 0)
    def _():
        m_scr[...] = jnp.full(m_scr.shape, NEG, f32)
        l_scr[...] = jnp.zeros(l_scr.shape, f32)
        acc_scr[...] = jnp.zeros(acc_scr.shape, f32)

    def update(n_keys, bias):
        s = _dot_nt(qaug_ref[...], kaug[0:n_keys, :])
        if bias is not None:
            s = s + bias
        m, l, acc = _online_step(s, vall[0:n_keys, :], m_scr[...], l_scr[...], acc_scr[...])
        m_scr[...] = m
        l_scr[...] = l
        acc_scr[...] = acc

    for pg in range(n_pg):
        for g in range(G):
            kaug[pg * page:(pg + 1) * page, g * HEAD_DIM:(g + 1) * HEAD_DIM] = (
                kpages[pg][pl.ds(g, page, stride=G), :].astype(bf16))
            vall[pg * page:(pg + 1) * page, g * HEAD_DIM:(g + 1) * HEAD_DIM] = (
                vpages[pg][pl.ds(g, page, stride=G), :].astype(bf16))
    kaug[:, kvw:] = oh_ref[...]
    update(n_pg * page, None)

    @pl.when(j == pl.num_programs(1) - 1)
    def _():
        kaug[0:page, :kvw] = ksn_ref[...]
        kaug[0:page, kvw:] = ohn_ref[...]
        vall[0:page, :] = vsn_ref[...]
        step = lax.broadcasted_iota(i32, (G * rows, page), 0) & (sp - 1)
        causal = jnp.where(lax.broadcasted_iota(i32, (G * rows, page), 1) <= step, 0.0, NEG)
        update(page, causal)
        inv = 1.0 / jnp.maximum(l_scr[...], 1e-30)
        for g in range(G):
            rs = slice(g * rows, (g + 1) * rows)
            o_s = acc_scr[rs, g * HEAD_DIM:(g + 1) * HEAD_DIM] * inv[rs]
            o_ref[g] = part_ref[g] + gt_ref[g][:, 1:2] * o_s


def _sample_sel(kcache, vcache, page_table, onehot, qaug, gates, part, ks_new, vs_new, *, G, n_pg):
    DB, _, rows, _ = part.shape
    R = rows // SUBLANES
    n_pages = page_table.shape[1]
    page = kcache.shape[1] // G
    ns = onehot.shape[1]
    aug_w = G * HEAD_DIM + ns
    n_keys = n_pg * page

    def page_spec(pg):
        return pl.BlockSpec((None, page * G, HEAD_DIM), lambda b, j, pt: (pt[b, j * n_pg + pg], 0, 0))

    pspec = pl.BlockSpec((None, G, rows, HEAD_DIM), lambda b, j, pt: (b, 0, 0, 0))
    nspec = pl.BlockSpec((None, page, G * HEAD_DIM), lambda b, j, pt: (b, 0, 0))
    grid_spec = pltpu.PrefetchScalarGridSpec(
        num_scalar_prefetch=1,
        grid=(DB, n_pages // n_pg),
        in_specs=[page_spec(pg) for pg in range(n_pg)] * 2
        + [pl.BlockSpec((n_keys, ns), lambda b, j, pt: (j, 0)),
           pl.BlockSpec((page, ns), lambda b, j, pt: (n_pages, 0)),
           pl.BlockSpec((None, G * rows, aug_w), lambda b, j, pt: (b, 0, 0)),
           pspec, pspec, nspec, nspec],
        out_specs=pspec,
        scratch_shapes=[pltpu.VMEM((n_keys, aug_w), bf16), pltpu.VMEM((n_keys, G * HEAD_DIM), bf16),
                        pltpu.VMEM((G * rows, 1), f32), pltpu.VMEM((G * rows, 1), f32),
                        pltpu.VMEM((G * rows, G * HEAD_DIM), f32)],
    )
    return pl.pallas_call(
        functools.partial(_sample_sel_kernel, n_pg=n_pg, G=G, R=R, page=page),
        grid_spec=grid_spec,
        out_shape=jax.ShapeDtypeStruct(part.shape, f32),
        compiler_params=_cparams(("parallel", "arbitrary")),
        name="nsa_sample_sel",
    )(page_table, *([kcache] * n_pg), *([vcache] * n_pg), onehot, onehot, qaug, gates, part, ks_new, vs_new)


def _rope_table(pos):
    half = HEAD_DIM // 2
    inv = ROPE_THETA ** (-jnp.arange(half, dtype=f32) / half)
    ang = pos.astype(f32)[:, None] * inv[None, :]
    cos, sin = jnp.cos(ang), jnp.sin(ang)
    return jnp.concatenate([cos, cos, -sin, sin], axis=-1)


def _block_onehot(n_keys, ns):
    blk = jnp.arange(n_keys, dtype=i32)[:, None] // SLC_LEN
    return (blk == jnp.arange(ns, dtype=i32)[None, :]).astype(bf16)


def kernel(x_prompt, x_sample, cache_k_cmp, cache_v_cmp, cache_k_slc, cache_v_slc, state_k_win, state_v_win, state_pool, page_table, w_in, w_cmp1_k, pe_cmp_k, w_cmp2_k, w_cmp1_v, pe_cmp_v, w_cmp2_v, w_pool, pool_scale, w_o, ln1_g, ln1_b, w_gate, w_up, w_down, ln2_g, ln2_b):
    B, T, D = x_prompt.shape
    DB, S, _ = x_sample.shape
    depth, n_phys, page, G, _ = cache_k_cmp.shape
    assert depth == 1 and S <= SUBLANES
    n_pages = page_table.shape[1]
    past = n_pages * page
    PW = state_pool.shape[3]
    NW = D - PW
    NH = NW // HEAD_DIM
    R = NH // G
    KVW = G * HEAD_DIM
    F = w_gate.shape[2]
    wb = state_k_win.shape[2]
    wbp = min(WINDOW, T)
    alpha = (2 * depth) ** 0.25
    MP = B * T
    MS = DB * S
    M = MP + MS
    assert T % page == 0 and page % CMP_STRIDE == 0 and page % SLC_LEN == 0 and past % SLC_LEN == 0

    x_all = jnp.concatenate([x_prompt.reshape(MP, D), x_sample.reshape(MS, D)], axis=0)
    x_bf = x_all.astype(bf16)
    w_in_t = w_in[0].T
    o_kv = PW + NW
    o_g = o_kv + N_KV_SEG * KVW
    w_g_t = w_in_t[o_g:o_g + N_BRANCH * NH].reshape(G, R * N_BRANCH, D)
    w_g_t = jnp.pad(w_g_t, ((0, 0), (0, LANES - R * N_BRANCH), (0, 0))).reshape(G * LANES, D)

    pos_all = jnp.concatenate([jnp.tile(jnp.arange(T, dtype=i32), B),
                               jnp.tile(past + jnp.arange(S, dtype=i32), DB)])
    cs_all = _rope_table(pos_all)

    tm = _pick_tile(M, 640, SUBLANES)
    tm_proj = _pick_tile(M, 832, SUBLANES)
    assert PW % KVW == 0 and NW % KVW == 0
    u = _proj(x_bf, w_in_t, cs_all, col0=0, n_col=PW // KVW, mode="none", tm=tm_proj, tn=KVW)
    q = _proj(x_bf, w_in_t, cs_all, col0=PW // KVW, n_col=NW // KVW, mode="rope", scale=HEAD_DIM ** -0.5,
              out_dtype=bf16, tm=tm_proj, tn=KVW)
    gates = _proj(x_bf, w_g_t, cs_all, col0=0, n_col=1, mode="sigmoid", tm=tm_proj, tn=G * LANES)
    kv_st, kv_bf = _proj_kv(x_bf, w_in_t, cs_all, col0=o_kv // KVW, n_seg=N_KV_SEG, G=G, tm=tm_proj)
    kc_st, vc_st = kv_st[0], kv_st[1]
    ks_bf, vs_bf, kw_bf, vw_bf = kv_bf[2], kv_bf[3], kv_bf[4], kv_bf[5]

    wp_bf = w_pool[0].astype(bf16)
    pscale = pool_scale[0].reshape(1, PW)
    pool_p = _pool_mix(jnp.zeros((B, POOL_MAX, PW), f32), u, wp_bf, pscale, nb=B, n=T, pos0=0,
                       tm=_pick_tile(T, 512, SUBLANES), out_dtype=bf16)
    u_s = u[MP:].reshape(DB, S, PW)
    hist_s = jnp.pad(state_pool[0], ((0, 0), (POOL_MAX - state_pool.shape[2], 0), (0, 0)))
    u_s_pad = jnp.pad(u_s, ((0, 0), (0, SUBLANES - S), (0, 0))).reshape(DB * SUBLANES, PW)
    pool_s = _pool_mix(hist_s, u_s_pad, wp_bf, pscale, nb=DB, n=SUBLANES, pos0=past, tm=SUBLANES,
                       out_dtype=f32).reshape(DB, SUBLANES, PW)[:, :S]

    def cmp_weights(w1, pe, w2):
        w1cat = w1[0].reshape(CMP_RATIO, CMP_STRIDE, HEAD_DIM, HEAD_DIM).transpose(1, 2, 0, 3)
        w1cat = w1cat.reshape(CMP_STRIDE * HEAD_DIM, CMP_RATIO * HEAD_DIM).astype(bf16)
        pe8 = jnp.broadcast_to(pe[0].reshape(1, CMP_LEN * HEAD_DIM), (SUBLANES, CMP_LEN * HEAD_DIM))
        return w1cat, pe8, w1[0].reshape(CMP_LEN * HEAD_DIM, HEAD_DIM).astype(bf16), w2[0].astype(bf16)

    cw_k = cmp_weights(w_cmp1_k, pe_cmp_k, w_cmp2_k)
    cw_v = cmp_weights(w_cmp1_v, pe_cmp_v, w_cmp2_v)

    def compress(pages, table, cw, rope, transposed=False):
        n_chunk = table.shape[1] * (page // CMP_STRIDE)
        cs_cmp = _rope_table(jnp.arange(n_chunk, dtype=i32) * CMP_STRIDE + CMP_LEN - 1)
        part = _cmp_part(pages, table, cw[0], G=G, n_pg=_pick_tile(table.shape[1], 16, 1))
        return _cmp_finish(part, cw[1], cw[2], cw[3], cs_cmp, rope=rope, transposed=transposed)

    def as_pages(a):
        return a.reshape(-1, page * G, HEAD_DIM)

    table_p = jnp.arange(MP // page, dtype=i32).reshape(B, T // page)
    kcb_p = compress(as_pages(kc_st[:MP * G]), table_p, cw_k, True)
    vcb_p_t = compress(as_pages(vc_st[:MP * G]), table_p, cw_v, False, transposed=True)
    kcb_s = compress(as_pages(cache_k_cmp.reshape(-1, HEAD_DIM)), page_table, cw_k, True)
    vcb_s = compress(as_pages(cache_v_cmp.reshape(-1, HEAD_DIM)), page_table, cw_v, False)

    ns_p = _round_up(pl.cdiv(T, SLC_LEN), LANES)
    nsa_p = _prompt_attn(q, gates, kcb_p, vcb_p_t, ks_bf, vs_bf, kw_bf, vw_bf, _block_onehot(T, ns_p),
                         B=B, T=T, G=G, R=R)

    def srow(a):
        return a[MP:].reshape(DB, S, a.shape[1])

    def to_rows(a):
        a = jnp.pad(a.transpose(0, 2, 3, 1, 4), ((0, 0), (0, 0), (0, 0), (0, SUBLANES - S), (0, 0)))
        return a.reshape(DB, G, R * SUBLANES, a.shape[-1])

    q_s = to_rows(srow(q).reshape(DB, S, G, R, HEAD_DIM))
    g_s = srow(gates).reshape(DB, S, G, LANES)[..., :R * N_BRANCH].reshape(DB, S, G, R, N_BRANCH)
    g_s = to_rows(jnp.pad(g_s, ((0, 0),) * 4 + ((0, LANES - N_BRANCH),)))

    def new_rows(a):
        return jnp.pad(srow(a), ((0, 0), (0, page - S), (0, 0)))

    n_slc_s = pl.cdiv(past + S, SLC_LEN)
    ns_s = _round_up(n_slc_s, LANES)
    part_s, qaug_s = _sample_cmp_win(q_s, g_s, kcb_s, vcb_s,
                                     state_k_win.reshape(DB, wb * G, HEAD_DIM),
                                     state_v_win.reshape(DB, wb * G, HEAD_DIM),
                                     new_rows(kw_bf), new_rows(vw_bf), past=past, n_slc=n_slc_s, ns=ns_s)
    o_s = _sample_sel(as_pages(cache_k_slc.reshape(-1, HEAD_DIM)), as_pages(cache_v_slc.reshape(-1, HEAD_DIM)),
                      page_table, _block_onehot(past + page, ns_s), qaug_s, g_s, part_s,
                      new_rows(ks_bf), new_rows(vs_bf), G=G, n_pg=_pick_tile(n_pages, 16, 1))
    nsa_s = o_s.reshape(DB, G, R, SUBLANES, HEAD_DIM)[:, :, :, :S].transpose(0, 3, 1, 2, 4)
    nsa_s = nsa_s.reshape(MS, NW).astype(bf16)

    pool_all = jnp.concatenate([pool_p, pool_s.reshape(MS, PW).astype(bf16)], axis=0)
    nsa_all = jnp.concatenate([nsa_p, nsa_s], axis=0)
    wo_bf = w_o[0].astype(bf16)
    h, h_bf = _resid_ln([(pool_all, wo_bf[:PW]), (nsa_all, wo_bf[PW:])], x_all,
                        ln1_g[0].reshape(1, D), ln1_b[0].reshape(1, D),
                        alpha=alpha, tm=tm, tn=_pick_tile(D, 512, LANES), n_kseg=1, emit_bf16=True)
    ff = _gateup(h_bf, w_gate[0], w_up[0], tm=_pick_tile(M, 1040, SUBLANES), tf=_pick_tile(F, 512, LANES))
    n_kseg = 2 if F % (2 * LANES) == 0 else 1
    y, = _resid_ln([(ff, w_down[0].astype(bf16))], h, ln2_g[0].reshape(1, D), ln2_b[0].reshape(1, D),
                   alpha=alpha, tm=tm, tn=_pick_tile(D, 256, LANES), n_kseg=n_kseg, emit_bf16=False)

    def pstate(i):
        return kv_st[i, :MP * G].reshape(1, B, T, G, HEAD_DIM)

    def sstate(i):
        return kv_st[i, MP * G:].reshape(1, DB, S, G, HEAD_DIM)

    kw_p, vw_p = pstate(4)[:, :, T - wbp:], pstate(5)[:, :, T - wbp:]
    pool_state_p = u[:MP].reshape(1, B, T, PW)[:, :, T - (POOL_MAX - 1):]
    kw_s = jnp.concatenate([state_k_win, sstate(4)], axis=2)[:, :, S:]
    vw_s = jnp.concatenate([state_v_win, sstate(5)], axis=2)[:, :, S:]
    pool_state_s = jnp.concatenate([state_pool, u_s[None]], axis=2)[:, :, S:]
    return (y[:MP].reshape(B, T, D), y[MP:].reshape(DB, S, D),
            pstate(0), pstate(1), pstate(2), pstate(3), kw_p, vw_p, pool_state_p,
            sstate(0), sstate(1), sstate(2), sstate(3), kw_s, vw_s, pool_state_s)
```

```python
import functools
import math

import jax
import jax.numpy as jnp
from jax import lax
from jax.experimental import pallas as pl
from jax.experimental.pallas import tpu as pltpu

HEAD_DIM = 128
POOL_WINDOWS = (2, 4, 8, 16)
POOL_MAX = max(POOL_WINDOWS)
N_BRANCH = 3
N_KV_SEG = 6
ROPE_KV_SEGS = (2, 4)
CMP_LEN = 32
CMP_STRIDE = 16
CMP_RATIO = CMP_LEN // CMP_STRIDE
SLC_LEN = 64
N_SEL = 16
WINDOW = 512
Q_TILE = 128
ROPE_THETA = 10000.0
LN_EPS = 1e-5
LANES = 128
SUBLANES = 8
VMEM_LIMIT = 56 * 1024 * 1024
NEG = -1e30

f32 = jnp.float32
bf16 = jnp.bfloat16
i32 = jnp.int32


def _cparams(sem):
    return pltpu.CompilerParams(dimension_semantics=sem, vmem_limit_bytes=VMEM_LIMIT)


def _pick_tile(n, target, align):
    best = None
    for t in range(align, min(n, target) + 1, align):
        if n % t == 0:
            best = t
    if best is None:
        raise ValueError(f"no tile for {n} (target {target}, align {align})")
    return best


def _round_up(n, m):
    return m * pl.cdiv(n, m)


def _dot(a, b):
    return jnp.dot(a, b, preferred_element_type=f32)


def _dot_nt(a, b):
    return lax.dot_general(a, b, (((1,), (1,)), ((), ())), preferred_element_type=f32)


def _rope_tile(a, cosf, sinf):
    return a * cosf + pltpu.roll(a, HEAD_DIM // 2, axis=1) * sinf


def _div_pow2(x, d):
    assert d & (d - 1) == 0
    return lax.shift_right_arithmetic(x, d.bit_length() - 1)


def _sigmoid(x):
    return 1.0 / (1.0 + jnp.exp(-x))


def _cast_weights_once(w_refs, wbf_refs, transpose=False):
    @pl.when(pl.program_id(1) == 0)
    def _():
        for w_ref, wbf in zip(w_refs, wbf_refs):
            w = w_ref[...]
            wbf[...] = (w.T if transpose else w).astype(bf16)


def _proj_kernel(x_ref, w_ref, cs_ref, o_ref, wbf, *, mode, scale):
    _cast_weights_once([w_ref], [wbf], transpose=True)
    acc = _dot(x_ref[...], wbf[...])
    if mode == "rope":
        cosf = cs_ref[:, :HEAD_DIM]
        sinf = cs_ref[:, HEAD_DIM:]
        for h in range(acc.shape[1] // HEAD_DIM):
            sl = slice(h * HEAD_DIM, (h + 1) * HEAD_DIM)
            o_ref[:, sl] = (_rope_tile(acc[:, sl], cosf, sinf) * scale).astype(o_ref.dtype)
    elif mode == "sigmoid":
        o_ref[...] = _sigmoid(acc).astype(o_ref.dtype)
    else:
        o_ref[...] = acc.astype(o_ref.dtype)


def _proj(x, w_t, cs, *, col0, n_col, mode, scale=1.0, out_dtype=f32, tm, tn):
    M, K = x.shape
    return pl.pallas_call(
        functools.partial(_proj_kernel, mode=mode, scale=scale),
        grid=(n_col, M // tm),
        in_specs=[pl.BlockSpec((tm, K), lambda j, i: (i, 0)),
                  pl.BlockSpec((tn, K), lambda j, i: (col0 + j, 0)),
                  pl.BlockSpec((tm, 2 * HEAD_DIM), lambda j, i: (i, 0))],
        out_specs=pl.BlockSpec((tm, tn), lambda j, i: (i, j)),
        out_shape=jax.ShapeDtypeStruct((M, n_col * tn), out_dtype),
        scratch_shapes=[pltpu.VMEM((K, tn), bf16)],
        compiler_params=_cparams(("parallel", "arbitrary")),
        name=f"proj_{mode}",
    )(x, w_t, cs)


def _proj_kv_kernel(x_ref, w_ref, cs_ref, st_ref, bf_ref, wbf, *, G):
    j = pl.program_id(0)
    _cast_weights_once([w_ref], [wbf], transpose=True)
    acc = _dot(x_ref[...], wbf[...])
    tm = acc.shape[0]

    def emit(rope):
        for g in range(G):
            y = acc[:, g * HEAD_DIM:(g + 1) * HEAD_DIM]
            if rope:
                y = _rope_tile(y, cs_ref[:, :HEAD_DIM], cs_ref[:, HEAD_DIM:])
            st_ref[pl.ds(g, tm, stride=G), :] = y
            bf_ref[:, g * HEAD_DIM:(g + 1) * HEAD_DIM] = y.astype(bf16)

    is_rope = functools.reduce(jnp.logical_or, [j == s for s in ROPE_KV_SEGS])
    pl.when(is_rope)(lambda: emit(True))
    pl.when(jnp.logical_not(is_rope))(lambda: emit(False))


def _proj_kv(x, w_t, cs, *, col0, n_seg, G, tm):
    M, K = x.shape
    N = G * HEAD_DIM
    return pl.pallas_call(
        functools.partial(_proj_kv_kernel, G=G),
        grid=(n_seg, M // tm),
        in_specs=[pl.BlockSpec((tm, K), lambda j, i: (i, 0)),
                  pl.BlockSpec((N, K), lambda j, i: (col0 + j, 0)),
                  pl.BlockSpec((tm, 2 * HEAD_DIM), lambda j, i: (i, 0))],
        out_specs=[pl.BlockSpec((None, tm * G, HEAD_DIM), lambda j, i: (j, i, 0)),
                   pl.BlockSpec((None, tm, N), lambda j, i: (j, i, 0))],
        out_shape=[jax.ShapeDtypeStruct((n_seg, M * G, HEAD_DIM), f32),
                   jax.ShapeDtypeStruct((n_seg, M, N), bf16)],
        scratch_shapes=[pltpu.VMEM((K, N), bf16)],
        compiler_params=_cparams(("parallel", "arbitrary")),
        name="proj_kv",
    )(x, w_t, cs)


def _gateup_kernel(x_ref, wg_ref, wu_ref, o_ref, wg_bf, wu_bf):
    _cast_weights_once([wg_ref, wu_ref], [wg_bf, wu_bf])
    x = x_ref[...]
    g = _dot(x, wg_bf[...])
    u = _dot(x, wu_bf[...])
    o_ref[...] = (g * _sigmoid(g) * u).astype(o_ref.dtype)


def _gateup(x, wg, wu, *, tm, tf):
    M, K = x.shape
    F = wg.shape[1]
    return pl.pallas_call(
        _gateup_kernel,
        grid=(F // tf, M // tm),
        in_specs=[pl.BlockSpec((tm, K), lambda j, i: (i, 0)),
                  pl.BlockSpec((K, tf), lambda j, i: (0, j)),
                  pl.BlockSpec((K, tf), lambda j, i: (0, j))],
        out_specs=pl.BlockSpec((tm, tf), lambda j, i: (i, j)),
        out_shape=jax.ShapeDtypeStruct((M, F), bf16),
        scratch_shapes=[pltpu.VMEM((K, tf), bf16)] * 2,
        compiler_params=_cparams(("parallel", "arbitrary")),
        name="ffn_gate_up",
    )(x, wg, wu)


def _resid_ln_kernel(*refs, n_pair, alpha, n_j, n_kseg, tn, emit_bf16, n_resid, n_out, top_rows):
    a_refs = refs[0:2 * n_pair:2]
    w_refs = refs[1:2 * n_pair:2]
    pos = 2 * n_pair
    r_refs = refs[pos:pos + n_resid]
    g_ref, b_ref = refs[pos + n_resid:pos + n_resid + 2]
    o_refs = refs[pos + n_resid + 2:pos + n_resid + 2 + n_out]
    obf_ref = refs[pos + n_resid + 2 + n_out] if emit_bf16 else None
    ybuf = refs[-1]
    i = pl.program_id(0)
    k = pl.program_id(1)
    j = pl.program_id(2)
    last_i = pl.num_programs(0) - 1

    def partial_product():
        acc = _dot(a_refs[0][...], w_refs[0][...])
        for a_ref, w_ref in zip(a_refs[1:], w_refs[1:]):
            acc = acc + _dot(a_ref[...], w_ref[...])
        return acc

    @pl.when(k == 0)
    def _():
        prod = partial_product()
        if n_resid == 1:
            ybuf[j] = alpha * r_refs[0][...] + prod
        else:
            @pl.when(i != last_i)
            def _():
                ybuf[j] = alpha * r_refs[0][...] + prod

            @pl.when(i == last_i)
            def _():
                ybuf[j, 0:top_rows, :] = alpha * r_refs[0][0:top_rows, :] + prod[0:top_rows]
                ybuf[j, top_rows:, :] = alpha * r_refs[1][...] + prod[top_rows:]

    if n_kseg > 1:
        @pl.when(k > 0)
        def _():
            ybuf[j] = ybuf[j] + partial_product()

    @pl.when((k == n_kseg - 1) & (j == n_j - 1))
    def _():
        n = n_j * tn
        tot = jnp.sum(ybuf[0], axis=-1, keepdims=True)
        for jj in range(1, n_j):
            tot = tot + jnp.sum(ybuf[jj], axis=-1, keepdims=True)
        mu = tot / n
        var = jnp.zeros_like(mu)
        for jj in range(n_j):
            d = ybuf[jj] - mu
            var = var + jnp.sum(d * d, axis=-1, keepdims=True)
        rstd = lax.rsqrt(var / n + LN_EPS)
        for jj in range(n_j):
            cols = slice(jj * tn, (jj + 1) * tn)
            out = (ybuf[jj] - mu) * rstd * g_ref[:, cols] + b_ref[:, cols]
            o_refs[0][:, cols] = out
            if n_out == 2:
                @pl.when(i == last_i)
                def _():
                    o_refs[1][:, cols] = out[top_rows:]
            if emit_bf16:
                obf_ref[:, cols] = out.astype(bf16)


def _resid_ln(pairs, resids, gamma, beta, *, alpha, tm, tn, n_kseg, emit_bf16, split_rows=None):
    M = pairs[0][0].shape[0]
    N = resids[0].shape[1]
    n_j = N // tn
    n_i = M // tm
    in_specs, args = [], []
    for a, w in pairs:
        ks = a.shape[1] // n_kseg
        assert ks * n_kseg == a.shape[1] and ks % LANES == 0
        in_specs += [pl.BlockSpec((tm, ks), lambda i, k, j: (i, k)),
                     pl.BlockSpec((ks, tn), lambda i, k, j: (k, j))]
        args += [a, w]

    def boundary(rows_first):
        top = rows_first - (n_i - 1) * tm
        assert 0 < top < tm and top % SUBLANES == 0 and rows_first + (tm - top) == M
        return top

    def rcol(k, j):
        return jnp.where(k == 0, j, n_j - 1)

    top_rows = None
    in_specs.append(pl.BlockSpec((tm, tn), lambda i, k, j: (i, rcol(k, j))))
    if len(resids) == 2:
        top_rows = boundary(resids[0].shape[0])
        in_specs.append(pl.BlockSpec((tm - top_rows, tn), lambda i, k, j: (0, rcol(k, j))))
    in_specs += [pl.BlockSpec((1, N), lambda i, k, j: (0, 0)), pl.BlockSpec((1, N), lambda i, k, j: (0, 0))]

    def full_rows(rows):
        return pl.BlockSpec((rows, N), lambda i, k, j: (i, 0), pipeline_mode=pl.Buffered(1))

    if split_rows is None:
        out_specs = [full_rows(tm)]
        out_shape = [jax.ShapeDtypeStruct((M, N), f32)]
    else:
        top_out = boundary(split_rows)
        assert top_rows in (None, top_out)
        top_rows = top_out
        out_specs = [full_rows(tm),
                     pl.BlockSpec((tm - top_rows, N), lambda i, k, j: (0, 0), pipeline_mode=pl.Buffered(1))]
        out_shape = [jax.ShapeDtypeStruct((split_rows, N), f32), jax.ShapeDtypeStruct((M - split_rows, N), f32)]
    n_out = len(out_specs)
    if emit_bf16:
        out_specs.append(full_rows(tm))
        out_shape.append(jax.ShapeDtypeStruct((M, N), bf16))
    return pl.pallas_call(
        functools.partial(_resid_ln_kernel, n_pair=len(pairs), alpha=alpha, n_j=n_j, n_kseg=n_kseg, tn=tn,
                          emit_bf16=emit_bf16, n_resid=len(resids), n_out=n_out, top_rows=top_rows),
        grid=(n_i, n_kseg, n_j),
        in_specs=in_specs,
        out_specs=out_specs,
        out_shape=out_shape,
        scratch_shapes=[pltpu.VMEM((n_j, tm, tn), f32)],
        compiler_params=_cparams(("parallel", "arbitrary", "arbitrary")),
        name="matmul_resid_layernorm",
    )(*args, *resids, gamma, beta)


def _pool_kernel(hist_ref, u_ref, wp_ref, sc_ref, o_ref, buf, *, tm, pos0):
    i = pl.program_id(1)
    H = POOL_MAX

    @pl.when(i == 0)
    def _():
        buf[0:H, :] = hist_ref[...]

    @pl.when(i > 0)
    def _():
        buf[0:H, :] = buf[tm:tm + H, :]

    buf[H:H + tm, :] = u_ref[...]
    pg = wp_ref.shape[1]
    pos = pos0 + i * tm + lax.broadcasted_iota(i32, (tm, 1), 0)
    for g, w in enumerate(POOL_WINDOWS):
        cols = slice(g * pg, (g + 1) * pg)
        cur = buf[H:H + tm, cols]
        tot = cur
        for k in range(1, w):
            tot = tot + buf[H - k:H - k + tm, cols]
        cnt = jnp.minimum(pos + 1, w).astype(f32)
        d = tot / cnt - cur
        y = _dot(d.astype(bf16), wp_ref[g]) * sc_ref[:, cols]
        o_ref[:, cols] = y.astype(o_ref.dtype)


def _pool_mix(hist, u, w_pool, scale, *, nb, n, pos0, tm, out_dtype):
    W = u.shape[1]
    nt = n // tm
    return pl.pallas_call(
        functools.partial(_pool_kernel, tm=tm, pos0=pos0),
        grid=(nb, nt),
        in_specs=[pl.BlockSpec((None, POOL_MAX, W), lambda b, i: (b, 0, 0)),
                  pl.BlockSpec((tm, W), lambda b, i: (b * nt + i, 0)),
                  pl.BlockSpec(w_pool.shape, lambda b, i: (0, 0, 0)),
                  pl.BlockSpec((1, W), lambda b, i: (0, 0))],
        out_specs=pl.BlockSpec((tm, W), lambda b, i: (b * nt + i, 0)),
        out_shape=jax.ShapeDtypeStruct((nb * n, W), out_dtype),
        scratch_shapes=[pltpu.VMEM((POOL_MAX + tm, W), f32)],
        compiler_params=_cparams(("parallel", "arbitrary")),
        name="pool_mix",
    )(hist, u, w_pool, scale)


def _cmp_part_kernel(pt_ref, *refs, n_pg, G):
    pages = refs[:n_pg]
    w_ref, o_ref, y_scr = refs[n_pg:]
    cpp = pages[0].shape[0]
    tok_per_tile = SUBLANES // G
    n_tile = n_pg * cpp
    rows = n_tile * SUBLANES
    width = w_ref.shape[2]
    row_in_tile = lax.broadcasted_iota(i32, (rows, HEAD_DIM), 0) & (SUBLANES - 1)
    acc = None
    for pp in range(CMP_STRIDE // tok_per_tile):
        x = jnp.concatenate([pages[pg][:, pp * SUBLANES:(pp + 1) * SUBLANES, :] for pg in range(n_pg)], axis=0)
        x = x.reshape(rows, HEAD_DIM)
        routed = [jnp.where((row_in_tile >= t * G) & (row_in_tile < (t + 1) * G), x, 0.0)
                  for t in range(tok_per_tile)]
        d = _dot(jnp.concatenate(routed, axis=1).astype(bf16), w_ref[pp])
        acc = d if acc is None else acc + d
    y = acc
    for t in range(1, tok_per_tile):
        y = y + pltpu.roll(acc, rows - t * G, axis=0)
    y_scr[...] = y.reshape(n_tile, SUBLANES, width)
    for g in range(G):
        o_ref[g] = y_scr[:, g, :]


def _cmp_part(pages, page_table, w1cat, *, G, n_pg):
    nb, n_pages = page_table.shape
    assert SUBLANES % G == 0
    chunk_rows = CMP_STRIDE * G
    cpp = pages.shape[1] // chunk_rows
    n_chunk = n_pages * cpp
    n_tile = n_pg * cpp
    width = w1cat.shape[1]
    tok_per_tile = SUBLANES // G
    pages = pages.reshape(pages.shape[0], cpp, chunk_rows, HEAD_DIM)
    w_tiles = w1cat.reshape(CMP_STRIDE // tok_per_tile, tok_per_tile * HEAD_DIM, width)

    def page_spec(pg):
        return pl.BlockSpec((None, cpp, chunk_rows, HEAD_DIM), lambda b, j, pt: (pt[b, j * n_pg + pg], 0, 0, 0))

    grid_spec = pltpu.PrefetchScalarGridSpec(
        num_scalar_prefetch=1,
        grid=(nb, n_pages // n_pg),
        in_specs=[page_spec(pg) for pg in range(n_pg)]
        + [pl.BlockSpec(w_tiles.shape, lambda b, j, pt: (0, 0, 0))],
        out_specs=pl.BlockSpec((None, G, n_tile, width), lambda b, j, pt: (b, 0, j, 0)),
        scratch_shapes=[pltpu.VMEM((n_tile, SUBLANES, width), f32)],
    )
    return pl.pallas_call(
        functools.partial(_cmp_part_kernel, n_pg=n_pg, G=G),
        grid_spec=grid_spec,
        out_shape=jax.ShapeDtypeStruct((nb, G, n_chunk, width), f32),
        compiler_params=_cparams(("parallel", "arbitrary")),
        name="cmp_part",
    )(page_table, *([pages] * n_pg), w_tiles)


def _gelu_tanh(x):
    return 0.5 * x * (1.0 + jnp.tanh(math.sqrt(2.0 / math.pi) * (x + 0.044715 * (x * x * x))))


def _cmp_finish_kernel(part_ref, pe_ref, w1_ref, w2_ref, cs_ref, o_ref, *, rope, transposed):
    G, n_chunk, width = part_ref.shape
    hid0 = _dot(pe_ref[...].astype(bf16), w1_ref[...])[0:1]
    part = part_ref[...].reshape(G * n_chunk, width)
    a = part[:, :HEAD_DIM]
    b = pltpu.roll(part[:, HEAD_DIM:], G * n_chunk - 1, axis=0)
    y = _dot(_gelu_tanh(hid0 + a + b).astype(bf16), w2_ref[...])
    for g in range(G):
        y_g = y[g * n_chunk:(g + 1) * n_chunk]
        if rope:
            y_g = _rope_tile(y_g, cs_ref[:, :HEAD_DIM], cs_ref[:, HEAD_DIM:])
        o_ref[g] = (y_g.T if transposed else y_g).astype(o_ref.dtype)


def _cmp_finish(part, pe, w1flat, w2, cs_cmp, *, rope, transposed):
    nb, G, n_chunk, _ = part.shape
    out_dims = (HEAD_DIM, n_chunk) if transposed else (n_chunk, HEAD_DIM)
    return pl.pallas_call(
        functools.partial(_cmp_finish_kernel, rope=rope, transposed=transposed),
        grid=(nb,),
        in_specs=[pl.BlockSpec((None, G, n_chunk, CMP_RATIO * HEAD_DIM), lambda b: (b, 0, 0, 0)),
                  pl.BlockSpec(pe.shape, lambda b: (0, 0)),
                  pl.BlockSpec(w1flat.shape, lambda b: (0, 0)),
                  pl.BlockSpec(w2.shape, lambda b: (0, 0)),
                  pl.BlockSpec(cs_cmp.shape, lambda b: (0, 0))],
        out_specs=pl.BlockSpec((None, G) + out_dims, lambda b: (b, 0, 0, 0)),
        out_shape=jax.ShapeDtypeStruct((nb, G) + out_dims, bf16),
        compiler_params=_cparams(("parallel",)),
        name="cmp_finish",
    )(part, pe, w1flat, w2, cs_cmp)


def _softmax_terms(s_biased, floor=None):
    m = jnp.max(s_biased, axis=-1, keepdims=True)
    if floor is not None:
        m = jnp.maximum(m, floor)
    e = jnp.exp(s_biased - m)
    return e, 1.0 / jnp.maximum(jnp.sum(e, axis=-1, keepdims=True), 1e-30)


def _dot_split3(p, w01):
    hi = p.astype(bf16)
    r1 = p - hi.astype(f32)
    mid = r1.astype(bf16)
    lo = (r1 - mid.astype(f32)).astype(bf16)
    return _dot(hi, w01) + _dot(mid, w01) + _dot(lo, w01)


def _overlap01(n_chunk, n_cmp, ns):
    n = lax.broadcasted_iota(i32, (n_chunk, ns), 0)
    s = lax.broadcasted_iota(i32, (n_chunk, ns), 1)
    ov = ((n * CMP_STRIDE < s * SLC_LEN + SLC_LEN) & (n * CMP_STRIDE + CMP_LEN - 1 >= s * SLC_LEN)
          & (n < n_cmp))
    return jnp.where(ov, 1.0, 0.0).astype(bf16)


def _cmp_bias(q_pos, n_chunk, n_cmp):
    n_io = lax.broadcasted_iota(i32, (q_pos.shape[0], n_chunk), 1)
    return jnp.where((n_io * CMP_STRIDE + CMP_LEN - 1 <= q_pos) & (n_io < n_cmp), 0.0, NEG)


def _selection_values(imp, s_io, cur):
    forced = (s_io == 0) | (s_io == cur) | (s_io == cur - 1)
    return jnp.where(forced, jnp.inf, jnp.where(s_io <= cur, imp, -jnp.inf))


def _ahead(other, val, other_first):
    return (other > val) | ((other == val) & other_first)


def _online_step(s, v, m_old, l_old, acc_old):
    m_new = jnp.maximum(m_old, jnp.max(s, axis=-1, keepdims=True))
    alpha = jnp.exp(m_old - m_new)
    e = jnp.exp(s - m_new)
    l_new = alpha * l_old + jnp.sum(e, axis=-1, keepdims=True)
    acc_new = alpha * acc_old + _dot(e.astype(bf16), v)
    return m_new, l_new, acc_new


def _prompt_attn_kernel(q_ref, gt_ref, kcb_ref, vcbt_ref, ks_ref, vs_ref, kw_ref, vw_ref, oh_ref, o_ref,
                        kaug, vst_ref, vwt_ref, qaug, val_scr, cnt_scr, acc_scr, *, R, n_cmp, n_slc, kc):
    c = pl.program_id(2)
    tq = Q_TILE
    nq = R * tq
    n_chunk = kcb_ref.shape[0]
    ns = oh_ref.shape[1]
    nr = val_scr.shape[0]
    win_keys = WINDOW + tq

    def transposed(tile):
        return tile.astype(f32).T.astype(bf16)

    @pl.when(c == 0)
    def _():
        kaug[:, :HEAD_DIM] = ks_ref[...]
        kaug[:, HEAD_DIM:] = oh_ref[...]
        for t in range(vst_ref.shape[0]):
            vst_ref[t] = transposed(vs_ref[t * tq:(t + 1) * tq, :])
            vwt_ref[t] = transposed(vw_ref[t * tq:(t + 1) * tq, :])

    gtt = gt_ref[...].T

    def all_heads(a):
        return jnp.concatenate([a] * R, axis=1)

    def weighted_values(v_t, e):
        return _dot(v_t, e.astype(bf16)), 1.0 / jnp.maximum(jnp.sum(e, axis=0, keepdims=True), 1e-30)

    for r in range(R):
        qaug[0:HEAD_DIM, r * tq:(r + 1) * tq] = transposed(q_ref[:, r * HEAD_DIM:(r + 1) * HEAD_DIM])
    q_t = qaug[0:HEAD_DIM, :]
    q_pos = c * tq + lax.broadcasted_iota(i32, (1, tq), 1)

    n_io = lax.broadcasted_iota(i32, (n_chunk, tq), 0)
    bias_c = jnp.where((n_io * CMP_STRIDE + CMP_LEN - 1 <= q_pos) & (n_io < n_cmp), 0.0, NEG)
    s_c = _dot(kcb_ref[...], q_t) + all_heads(bias_c)
    e_c = jnp.exp(s_c - jnp.maximum(jnp.max(s_c, axis=0, keepdims=True), 0.5 * NEG))
    o_c, inv_c = weighted_values(vcbt_ref[...], e_c)
    o_c = o_c * inv_c
    p_c = e_c * inv_c
    p_sum = p_c[:, 0:tq]
    for r in range(1, R):
        p_sum = p_sum + p_c[:, r * tq:(r + 1) * tq]

    blk = lax.broadcasted_iota(i32, (nr, n_chunk), 0)
    n_col = lax.broadcasted_iota(i32, (nr, n_chunk), 1)
    ov = jnp.where((n_col * CMP_STRIDE < blk * SLC_LEN + SLC_LEN)
                   & (n_col * CMP_STRIDE + CMP_LEN - 1 >= blk * SLC_LEN) & (n_col < n_cmp), 1.0, 0.0).astype(bf16)
    hi = p_sum.astype(bf16)
    r1 = p_sum - hi.astype(f32)
    mid = r1.astype(bf16)
    lo = (r1 - mid.astype(f32)).astype(bf16)
    imp_t = _dot(ov, hi) + _dot(ov, mid) + _dot(ov, lo)
    s_io = lax.broadcasted_iota(i32, (nr, tq), 0)
    cur = _div_pow2(c * tq + lax.broadcasted_iota(i32, (nr, tq), 1), SLC_LEN)
    val_scr[...] = _selection_values(imp_t, s_io, cur)
    cnt_scr[...] = jnp.zeros(cnt_scr.shape, f32)
    n_live = _div_pow2(c * tq + tq - 1, SLC_LEN) + 1
    for grp in range(nr // SUBLANES):
        @pl.when(grp * SUBLANES < n_live)
        def _():
            val = val_scr[...]
            cnt = cnt_scr[...]
            for sp in range(grp * SUBLANES, min((grp + 1) * SUBLANES, n_slc)):
                cnt = cnt + jnp.where(_ahead(val_scr[sp:sp + 1, :], val, s_io > sp), 1.0, 0.0)
            cnt_scr[...] = cnt

    first_blk = _div_pow2(c * tq, SLC_LEN)
    keep = (cnt_scr[...] < N_SEL) & (val_scr[...] > -jnp.inf) & (s_io < first_blk)
    mask_t = jnp.where(keep, 0.0, NEG)
    if nr < ns:
        mask_t = jnp.concatenate([mask_t, jnp.full((ns - nr, tq), NEG, f32)], axis=0)
    qaug[HEAD_DIM:, :] = all_heads(mask_t.astype(bf16))

    d0 = pl.multiple_of(c * tq, tq)
    tri = jnp.where(lax.broadcasted_iota(i32, (tq, tq), 0) <= lax.broadcasted_iota(i32, (tq, tq), 1), 0.0, NEG)
    s_d = _dot(ks_ref[pl.ds(d0, tq), :], q_t) + all_heads(tri)
    m_d = jnp.max(s_d, axis=0, keepdims=True)
    e_d = jnp.exp(s_d - m_d)
    l_d = jnp.sum(e_d, axis=0, keepdims=True)
    acc_scr[...] = _dot(vst_ref[c], e_d.astype(bf16))

    tiles_per_chunk = kc // tq

    def sweep(j, carry):
        m_old, l_old = carry
        k0 = pl.multiple_of(j * kc, kc)
        s = _dot(kaug[pl.ds(k0, kc), :], qaug[...])
        m_new = jnp.maximum(m_old, jnp.max(s, axis=0, keepdims=True))
        alpha = jnp.exp(m_old - m_new)
        e = jnp.exp(s - m_new)
        v_t = jnp.concatenate([vst_ref[j * tiles_per_chunk + i] for i in range(tiles_per_chunk)], axis=1)
        acc_scr[...] = alpha * acc_scr[...] + _dot(v_t, e.astype(bf16))
        return m_new, alpha * l_old + jnp.sum(e, axis=0, keepdims=True)

    _, l_s = lax.fori_loop(0, (c * tq + kc - 1) // kc, sweep, (m_d, l_d))
    o_s = acc_scr[...] * (1.0 / jnp.maximum(l_s, 1e-30))

    w_tile = jnp.maximum(c - WINDOW // tq, 0)
    w0 = pl.multiple_of(w_tile * tq, tq)
    key_w = w0 + lax.broadcasted_iota(i32, (win_keys, tq), 0)
    bias_w = jnp.where((key_w <= q_pos) & (q_pos - key_w < WINDOW), 0.0, NEG)
    s_w = _dot(kw_ref[pl.ds(w0, win_keys), :], q_t) + all_heads(bias_w)
    vw_t = jnp.concatenate([vwt_ref[w_tile + i] for i in range(win_keys // tq)], axis=1)
    o_w, inv_w = weighted_values(vw_t, jnp.exp(s_w - jnp.max(s_w, axis=0, keepdims=True)))
    o_w = o_w * inv_w

    for r in range(R):
        cols = slice(r * tq, (r + 1) * tq)
        g = [gtt[r * N_BRANCH + br:r * N_BRANCH + br + 1, :] for br in range(N_BRANCH)]
        o_t = g[0] * o_c[:, cols] + g[1] * o_s[:, cols] + g[2] * o_w[:, cols]
        o_ref[:, r * HEAD_DIM:(r + 1) * HEAD_DIM] = o_t.T.astype(o_ref.dtype)


def _prompt_attn(q, gates, kcb, vcb_t, ks, vs, kw, vw, onehot, *, B, T, G, R):
    n_chunk = kcb.shape[2]
    n_cmp = n_chunk - CMP_RATIO + 1
    n_slc = pl.cdiv(T, SLC_LEN)
    ns = onehot.shape[1]
    nr = _round_up(n_slc, SUBLANES)
    kc = _pick_tile(T, 1024, Q_TILE)
    nt = T // Q_TILE
    assert T % Q_TILE == 0 and T >= WINDOW + Q_TILE and Q_TILE % SLC_LEN == 0 and Q_TILE <= 2 * SLC_LEN
    seq_spec = pl.BlockSpec((T, HEAD_DIM), lambda b, g, c: (b, g))
    return pl.pallas_call(
        functools.partial(_prompt_attn_kernel, R=R, n_cmp=n_cmp, n_slc=n_slc, kc=kc),
        grid=(B, G, nt),
        in_specs=[pl.BlockSpec((Q_TILE, R * HEAD_DIM), lambda b, g, c: (b * nt + c, g)),
                  pl.BlockSpec((Q_TILE, LANES), lambda b, g, c: (b * nt + c, g)),
                  pl.BlockSpec((None, None, n_chunk, HEAD_DIM), lambda b, g, c: (b, g, 0, 0)),
                  pl.BlockSpec((None, None, HEAD_DIM, n_chunk), lambda b, g, c: (b, g, 0, 0)),
                  seq_spec, seq_spec, seq_spec, seq_spec,
                  pl.BlockSpec((T, ns), lambda b, g, c: (0, 0))],
        out_specs=pl.BlockSpec((Q_TILE, R * HEAD_DIM), lambda b, g, c: (b * nt + c, g)),
        out_shape=jax.ShapeDtypeStruct((B * T, G * R * HEAD_DIM), bf16),
        scratch_shapes=[pltpu.VMEM((T, HEAD_DIM + ns), bf16),
                        pltpu.VMEM((nt, HEAD_DIM, Q_TILE), bf16), pltpu.VMEM((nt, HEAD_DIM, Q_TILE), bf16),
                        pltpu.VMEM((HEAD_DIM + ns, R * Q_TILE), bf16),
                        pltpu.VMEM((nr, Q_TILE), f32), pltpu.VMEM((nr, Q_TILE), f32),
                        pltpu.VMEM((HEAD_DIM, R * Q_TILE), f32)],
        compiler_params=_cparams(("parallel", "parallel", "arbitrary")),
        name="nsa_prompt",
    )(q, gates, kcb, vcb_t, ks, vs, kw, vw, onehot)


def _sample_cmp_win_kernel(q_ref, gt_ref, kcb_ref, vcb_ref, kwin_ref, vwin_ref, kwn_ref, vwn_ref,
                           part_ref, qaug_ref, kbuf, vbuf, *, G, R, n_cmp, n_slc, past):
    n_chunk = kcb_ref.shape[1]
    ns = qaug_ref.shape[1] - G * HEAD_DIM
    wb = kwin_ref.shape[0] // G
    sp = SUBLANES
    rows = R * sp
    nbuf = kbuf.shape[0]
    q_pos = past + (lax.broadcasted_iota(i32, (rows, 1), 0) & (sp - 1))
    bias_c = _cmp_bias(q_pos, n_chunk, n_cmp)
    key_w = past - wb + lax.broadcasted_iota(i32, (rows, nbuf), 1)
    bias_w = jnp.where((key_w <= q_pos) & (q_pos - key_w < WINDOW) & (key_w >= 0), 0.0, NEG)
    ov = _overlap01(n_chunk, n_cmp, ns)
    s_io = lax.broadcasted_iota(i32, (sp, ns), 1)
    cur = _div_pow2(q_pos[0:sp], SLC_LEN)
    qaug_ref[:, :G * HEAD_DIM] = jnp.zeros((G * rows, G * HEAD_DIM), bf16)

    for g in range(G):
        q = q_ref[g]
        cols = slice(g * HEAD_DIM, (g + 1) * HEAD_DIM)

        e, inv = _softmax_terms(_dot_nt(q, kcb_ref[g]) + bias_c, floor=0.5 * NEG)
        p_c = e * inv
        o_c = _dot(p_c.astype(bf16), vcb_ref[g])
        p_sum = p_c[0:sp]
        for r in range(1, R):
            p_sum = p_sum + p_c[r * sp:(r + 1) * sp]
        val = _selection_values(_dot_split3(p_sum, ov), s_io, cur)
        cnt = jnp.zeros((sp, ns), f32)
        for blk in range(n_slc):
            cnt = cnt + jnp.where(_ahead(val[:, blk:blk + 1], val, s_io > blk), 1.0, 0.0)
        mask = jnp.where((cnt < N_SEL) & (val > -jnp.inf), 0.0, NEG).astype(bf16)

        qaug_ref[g * rows:(g + 1) * rows, cols] = q
        qaug_ref[g * rows:(g + 1) * rows, G * HEAD_DIM:] = jnp.concatenate([mask] * R, axis=0)

        kbuf[0:wb, :] = kwin_ref[pl.ds(g, wb, stride=G), :].astype(bf16)
        kbuf[wb:nbuf, :] = kwn_ref[:, cols]
        vbuf[0:wb, :] = vwin_ref[pl.ds(g, wb, stride=G), :].astype(bf16)
        vbuf[wb:nbuf, :] = vwn_ref[:, cols]
        e, inv = _softmax_terms(_dot_nt(q, kbuf[...]) + bias_w)
        o_w = _dot(e.astype(bf16), vbuf[...]) * inv
        part_ref[g] = gt_ref[g][:, 0:1] * o_c + gt_ref[g][:, 2:3] * o_w


def _sample_cmp_win(q, gates, kcb, vcb, kwin, vwin, kw_new, vw_new, *, past, n_slc, ns):
    DB, G, rows, _ = q.shape
    R = rows // SUBLANES
    n_chunk = kcb.shape[2]
    n_cmp = n_chunk - CMP_RATIO + 1
    wbg = kwin.shape[1]
    npad = kw_new.shape[1]
    qspec = pl.BlockSpec((None, G, rows, HEAD_DIM), lambda b: (b, 0, 0, 0))
    cspec = pl.BlockSpec((None, G, n_chunk, HEAD_DIM), lambda b: (b, 0, 0, 0))
    wspec = pl.BlockSpec((None, wbg, HEAD_DIM), lambda b: (b, 0, 0))
    nspec = pl.BlockSpec((None, npad, G * HEAD_DIM), lambda b: (b, 0, 0))
    aug_w = G * HEAD_DIM + ns
    return pl.pallas_call(
        functools.partial(_sample_cmp_win_kernel, G=G, R=R, n_cmp=n_cmp, n_slc=n_slc, past=past),
        grid=(DB,),
        in_specs=[qspec, qspec, cspec, cspec, wspec, wspec, nspec, nspec],
        out_specs=[qspec, pl.BlockSpec((None, G * rows, aug_w), lambda b: (b, 0, 0))],
        out_shape=[jax.ShapeDtypeStruct((DB, G, rows, HEAD_DIM), f32),
                   jax.ShapeDtypeStruct((DB, G * rows, aug_w), bf16)],
        scratch_shapes=[pltpu.VMEM((wbg // G + npad, HEAD_DIM), bf16)] * 2,
        compiler_params=_cparams(("parallel",)),
        name="nsa_sample_cmp_win",
    )(q, gates, kcb, vcb, kwin, vwin, kw_new, vw_new)


def _sample_sel_kernel(pt_ref, *refs, n_pg, G, R, page):
    kpages = refs[:n_pg]
    vpages = refs[n_pg:2 * n_pg]
    (oh_ref, ohn_ref, qaug_ref, gt_ref, part_ref, ksn_ref, vsn_ref, o_ref,
     kaug, vall, m_scr, l_scr, acc_scr) = refs[2 * n_pg:]
    j = pl.program_id(1)
    sp = SUBLANES
    rows = R * sp
    kvw = G * HEAD_DIM

    @pl.when(j == 0)
    def _():
        m_scr[...] = jnp.full(m_scr.shape, NEG, f32)
        l_scr[...] = jnp.zeros(l_scr.shape, f32)
        acc_scr[...] = jnp.zeros(acc_scr.shape, f32)

    def update(n_keys, bias):
        s = _dot_nt(qaug_ref[...], kaug[0:n_keys, :])
        if bias is not None:
            s = s + bias
        m, l, acc = _online_step(s, vall[0:n_keys, :], m_scr[...], l_scr[...], acc_scr[...])
        m_scr[...] = m
        l_scr[...] = l
        acc_scr[...] = acc

    for pg in range(n_pg):
        for g in range(G):
            kaug[pg * page:(pg + 1) * page, g * HEAD_DIM:(g + 1) * HEAD_DIM] = (
                kpages[pg][pl.ds(g, page, stride=G), :].astype(bf16))
            vall[pg * page:(pg + 1) * page, g * HEAD_DIM:(g + 1) * HEAD_DIM] = (
                vpages[pg][pl.ds(g, page, stride=G), :].astype(bf16))
    kaug[:, kvw:] = oh_ref[...]
    update(n_pg * page, None)

    @pl.when(j == pl.num_programs(1) - 1)
    def _():
        kaug[0:page, :kvw] = ksn_ref[...]
        kaug[0:page, kvw:] = ohn_ref[...]
        vall[0:page, :] = vsn_ref[...]
        step = lax.broadcasted_iota(i32, (G * rows, page), 0) & (sp - 1)
        causal = jnp.where(lax.broadcasted_iota(i32, (G * rows, page), 1) <= step, 0.0, NEG)
        update(page, causal)
        inv = 1.0 / jnp.maximum(l_scr[...], 1e-30)
        for g in range(G):
            rs = slice(g * rows, (g + 1) * rows)
            o_s = acc_scr[rs, g * HEAD_DIM:(g + 1) * HEAD_DIM] * inv[rs]
            o_ref[g] = part_ref[g] + gt_ref[g][:, 1:2] * o_s


def _sample_sel(kcache, vcache, page_table, onehot, qaug, gates, part, ks_new, vs_new, *, G, n_pg):
    DB, _, rows, _ = part.shape
    R = rows // SUBLANES
    n_pages = page_table.shape[1]
    page = kcache.shape[1] // G
    ns = onehot.shape[1]
    aug_w = G * HEAD_DIM + ns
    n_keys = n_pg * page

    def page_spec(pg):
        return pl.BlockSpec((None, page * G, HEAD_DIM), lambda b, j, pt: (pt[b, j * n_pg + pg], 0, 0))

    pspec = pl.BlockSpec((None, G, rows, HEAD_DIM), lambda b, j, pt: (b, 0, 0, 0))
    nspec = pl.BlockSpec((None, page, G * HEAD_DIM), lambda b, j, pt: (b, 0, 0))
    grid_spec = pltpu.PrefetchScalarGridSpec(
        num_scalar_prefetch=1,
        grid=(DB, n_pages // n_pg),
        in_specs=[page_spec(pg) for pg in range(n_pg)] * 2
        + [pl.BlockSpec((n_keys, ns), lambda b, j, pt: (j, 0)),
           pl.BlockSpec((page, ns), lambda b, j, pt: (n_pages, 0)),
           pl.BlockSpec((None, G * rows, aug_w), lambda b, j, pt: (b, 0, 0)),
           pspec, pspec, nspec, nspec],
        out_specs=pspec,
        scratch_shapes=[pltpu.VMEM((n_keys, aug_w), bf16), pltpu.VMEM((n_keys, G * HEAD_DIM), bf16),
                        pltpu.VMEM((G * rows, 1), f32), pltpu.VMEM((G * rows, 1), f32),
                        pltpu.VMEM((G * rows, G * HEAD_DIM), f32)],
    )
    return pl.pallas_call(
        functools.partial(_sample_sel_kernel, n_pg=n_pg, G=G, R=R, page=page),
        grid_spec=grid_spec,
        out_shape=jax.ShapeDtypeStruct(part.shape, f32),
        compiler_params=_cparams(("parallel", "arbitrary")),
        name="nsa_sample_sel",
    )(page_table, *([kcache] * n_pg), *([vcache] * n_pg), onehot, onehot, qaug, gates, part, ks_new, vs_new)


def _rope_table(pos):
    half = HEAD_DIM // 2
    inv = ROPE_THETA ** (-jnp.arange(half, dtype=f32) / half)
    ang = pos.astype(f32)[:, None] * inv[None, :]
    cos, sin = jnp.cos(ang), jnp.sin(ang)
    return jnp.concatenate([cos, cos, -sin, sin], axis=-1)


def _block_onehot(n_keys, ns):
    blk = jnp.arange(n_keys, dtype=i32)[:, None] // SLC_LEN
    return (blk == jnp.arange(ns, dtype=i32)[None, :]).astype(bf16)


def kernel(x_prompt, x_sample, cache_k_cmp, cache_v_cmp, cache_k_slc, cache_v_slc, state_k_win, state_v_win, state_pool, page_table, w_in, w_cmp1_k, pe_cmp_k, w_cmp2_k, w_cmp1_v, pe_cmp_v, w_cmp2_v, w_pool, pool_scale, w_o, ln1_g, ln1_b, w_gate, w_up, w_down, ln2_g, ln2_b):
    B, T, D = x_prompt.shape
    DB, S, _ = x_sample.shape
    depth, n_phys, page, G, _ = cache_k_cmp.shape
    assert depth == 1 and S <= SUBLANES
    n_pages = page_table.shape[1]
    past = n_pages * page
    PW = state_pool.shape[3]
    NW = D - PW
    NH = NW // HEAD_DIM
    R = NH // G
    KVW = G * HEAD_DIM
    F = w_gate.shape[2]
    wb = state_k_win.shape[2]
    wbp = min(WINDOW, T)
    alpha = (2 * depth) ** 0.25
    MP = B * T
    MS = DB * S
    M = MP + MS
    assert T % page == 0 and page % CMP_STRIDE == 0 and page % SLC_LEN == 0 and past % SLC_LEN == 0

    x_p, x_s = x_prompt.reshape(MP, D), x_sample.reshape(MS, D)
    x_bf = jnp.concatenate([x_p.astype(bf16), x_s.astype(bf16)], axis=0)
    w_in_t = w_in[0].T
    o_kv = PW + NW
    o_g = o_kv + N_KV_SEG * KVW
    w_g_t = w_in_t[o_g:o_g + N_BRANCH * NH].reshape(G, R * N_BRANCH, D)
    w_g_t = jnp.pad(w_g_t, ((0, 0), (0, LANES - R * N_BRANCH), (0, 0))).reshape(G * LANES, D)

    pos_all = jnp.concatenate([jnp.tile(jnp.arange(T, dtype=i32), B),
                               jnp.tile(past + jnp.arange(S, dtype=i32), DB)])
    cs_all = _rope_table(pos_all)

    tm = _pick_tile(M, 640, SUBLANES)
    tm_proj = _pick_tile(M, 832, SUBLANES)
    assert PW % KVW == 0 and NW % KVW == 0
    u = _proj(x_bf, w_in_t, cs_all, col0=0, n_col=PW // KVW, mode="none", tm=tm_proj, tn=KVW)
    q = _proj(x_bf, w_in_t, cs_all, col0=PW // KVW, n_col=NW // KVW, mode="rope", scale=HEAD_DIM ** -0.5,
              out_dtype=bf16, tm=tm_proj, tn=KVW)
    gates = _proj(x_bf, w_g_t, cs_all, col0=0, n_col=1, mode="sigmoid", tm=tm_proj, tn=G * LANES)
    kv_st, kv_bf = _proj_kv(x_bf, w_in_t, cs_all, col0=o_kv // KVW, n_seg=N_KV_SEG, G=G, tm=tm_proj)
    kc_st, vc_st = kv_st[0], kv_st[1]
    ks_bf, vs_bf, kw_bf, vw_bf = kv_bf[2], kv_bf[3], kv_bf[4], kv_bf[5]

    wp_bf = w_pool[0].astype(bf16)
    pscale = pool_scale[0].reshape(1, PW)
    pool_p = _pool_mix(jnp.zeros((B, POOL_MAX, PW), f32), u, wp_bf, pscale, nb=B, n=T, pos0=0,
                       tm=_pick_tile(T, 512, SUBLANES), out_dtype=bf16)
    u_s = u[MP:].reshape(DB, S, PW)
    hist_s = jnp.pad(state_pool[0], ((0, 0), (POOL_MAX - state_pool.shape[2], 0), (0, 0)))
    u_s_pad = jnp.pad(u_s, ((0, 0), (0, SUBLANES - S), (0, 0))).reshape(DB * SUBLANES, PW)
    pool_s = _pool_mix(hist_s, u_s_pad, wp_bf, pscale, nb=DB, n=SUBLANES, pos0=past, tm=SUBLANES,
                       out_dtype=f32).reshape(DB, SUBLANES, PW)[:, :S]

    def cmp_weights(w1, pe, w2):
        w1cat = w1[0].reshape(CMP_RATIO, CMP_STRIDE, HEAD_DIM, HEAD_DIM).transpose(1, 2, 0, 3)
        w1cat = w1cat.reshape(CMP_STRIDE * HEAD_DIM, CMP_RATIO * HEAD_DIM).astype(bf16)
        pe8 = jnp.broadcast_to(pe[0].reshape(1, CMP_LEN * HEAD_DIM), (SUBLANES, CMP_LEN * HEAD_DIM))
        return w1cat, pe8, w1[0].reshape(CMP_LEN * HEAD_DIM, HEAD_DIM).astype(bf16), w2[0].astype(bf16)

    cw_k = cmp_weights(w_cmp1_k, pe_cmp_k, w_cmp2_k)
    cw_v = cmp_weights(w_cmp1_v, pe_cmp_v, w_cmp2_v)

    def compress(pages, table, cw, rope, transposed=False):
        n_chunk = table.shape[1] * (page // CMP_STRIDE)
        cs_cmp = _rope_table(jnp.arange(n_chunk, dtype=i32) * CMP_STRIDE + CMP_LEN - 1)
        part = _cmp_part(pages, table, cw[0], G=G, n_pg=_pick_tile(table.shape[1], 16, 1))
        return _cmp_finish(part, cw[1], cw[2], cw[3], cs_cmp, rope=rope, transposed=transposed)

    def as_pages(a):
        return a.reshape(-1, page * G, HEAD_DIM)

    table_p = jnp.arange(MP // page, dtype=i32).reshape(B, T // page)
    kcb_p = compress(as_pages(kc_st[:MP * G]), table_p, cw_k, True)
    vcb_p_t = compress(as_pages(vc_st[:MP * G]), table_p, cw_v, False, transposed=True)
    kcb_s = compress(as_pages(cache_k_cmp.reshape(-1, HEAD_DIM)), page_table, cw_k, True)
    vcb_s = compress(as_pages(cache_v_cmp.reshape(-1, HEAD_DIM)), page_table, cw_v, False)

    ns_p = _round_up(pl.cdiv(T, SLC_LEN), LANES)
    nsa_p = _prompt_attn(q, gates, kcb_p, vcb_p_t, ks_bf, vs_bf, kw_bf, vw_bf, _block_onehot(T, ns_p),
                         B=B, T=T, G=G, R=R)

    def srow(a):
        return a[MP:].reshape(DB, S, a.shape[1])

    def to_rows(a):
        a = jnp.pad(a.transpose(0, 2, 3, 1, 4), ((0, 0), (0, 0), (0, 0), (0, SUBLANES - S), (0, 0)))
        return a.reshape(DB, G, R * SUBLANES, a.shape[-1])

    q_s = to_rows(srow(q).reshape(DB, S, G, R, HEAD_DIM))
    g_s = srow(gates).reshape(DB, S, G, LANES)[..., :R * N_BRANCH].reshape(DB, S, G, R, N_BRANCH)
    g_s = to_rows(jnp.pad(g_s, ((0, 0),) * 4 + ((0, LANES - N_BRANCH),)))

    def new_rows(a):
        return jnp.pad(srow(a), ((0, 0), (0, page - S), (0, 0)))

    n_slc_s = pl.cdiv(past + S, SLC_LEN)
    ns_s = _round_up(n_slc_s, LANES)
    part_s, qaug_s = _sample_cmp_win(q_s, g_s, kcb_s, vcb_s,
                                     state_k_win.reshape(DB, wb * G, HEAD_DIM),
                                     state_v_win.reshape(DB, wb * G, HEAD_DIM),
                                     new_rows(kw_bf), new_rows(vw_bf), past=past, n_slc=n_slc_s, ns=ns_s)
    o_s = _sample_sel(as_pages(cache_k_slc.reshape(-1, HEAD_DIM)), as_pages(cache_v_slc.reshape(-1, HEAD_DIM)),
                      page_table, _block_onehot(past + page, ns_s), qaug_s, g_s, part_s,
                      new_rows(ks_bf), new_rows(vs_bf), G=G, n_pg=_pick_tile(n_pages, 16, 1))
    nsa_s = o_s.reshape(DB, G, R, SUBLANES, HEAD_DIM)[:, :, :, :S].transpose(0, 3, 1, 2, 4)
    nsa_s = nsa_s.reshape(MS, NW).astype(bf16)

    pool_all = jnp.concatenate([pool_p, pool_s.reshape(MS, PW).astype(bf16)], axis=0)
    nsa_all = jnp.concatenate([nsa_p, nsa_s], axis=0)
    wo_bf = w_o[0].astype(bf16)
    h, h_bf = _resid_ln([(pool_all, wo_bf[:PW]), (nsa_all, wo_bf[PW:])], [x_p, x_s],
                        ln1_g[0].reshape(1, D), ln1_b[0].reshape(1, D),
                        alpha=alpha, tm=tm, tn=_pick_tile(D, 256, LANES), n_kseg=1, emit_bf16=True)
    ff = _gateup(h_bf, w_gate[0], w_up[0], tm=_pick_tile(M, 1040, SUBLANES), tf=_pick_tile(F, 512, LANES))
    n_kseg = 2 if F % (2 * LANES) == 0 else 1
    y_p, y_s = _resid_ln([(ff, w_down[0].astype(bf16))], [h], ln2_g[0].reshape(1, D), ln2_b[0].reshape(1, D),
                         alpha=alpha, tm=tm, tn=_pick_tile(D, 256, LANES), n_kseg=n_kseg, emit_bf16=False,
                         split_rows=MP)

    def pstate(i):
        return kv_st[i, :MP * G].reshape(1, B, T, G, HEAD_DIM)

    def sstate(i):
        return kv_st[i, MP * G:].reshape(1, DB, S, G, HEAD_DIM)

    kw_p, vw_p = pstate(4)[:, :, T - wbp:], pstate(5)[:, :, T - wbp:]
    pool_state_p = u[:MP].reshape(1, B, T, PW)[:, :, T - (POOL_MAX - 1):]
    kw_s = jnp.concatenate([state_k_win, sstate(4)], axis=2)[:, :, S:]
    vw_s = jnp.concatenate([state_v_win, sstate(5)], axis=2)[:, :, S:]
    pool_state_s = jnp.concatenate([state_pool, u_s[None]], axis=2)[:, :, S:]
    return (y_p.reshape(B, T, D), y_s.reshape(DB, S, D),
            pstate(0), pstate(1), pstate(2), pstate(3), kw_p, vw_p, pool_state_p,
            sstate(0), sstate(1), sstate(2), sstate(3), kw_s, vw_s, pool_state_s)
```

```python
import functools
import math

import jax
import jax.numpy as jnp
from jax import lax
from jax.experimental import pallas as pl
from jax.experimental.pallas import tpu as pltpu

HEAD_DIM = 128
POOL_WINDOWS = (2, 4, 8, 16)
POOL_MAX = max(POOL_WINDOWS)
N_BRANCH = 3
N_KV_SEG = 6
ROPE_KV_SEGS = (2, 4)
CMP_LEN = 32
CMP_STRIDE = 16
CMP_RATIO = CMP_LEN // CMP_STRIDE
SLC_LEN = 64
N_SEL = 16
WINDOW = 512
Q_TILE = 128
ROPE_THETA = 10000.0
LN_EPS = 1e-5
LANES = 128
SUBLANES = 8
VMEM_LIMIT = 60 * 1024 * 1024
NEG = -1e30

f32 = jnp.float32
bf16 = jnp.bfloat16
i32 = jnp.int32


def _cparams(sem):
    return pltpu.CompilerParams(dimension_semantics=sem, vmem_limit_bytes=VMEM_LIMIT)


def _pick_tile(n, target, align):
    best = None
    for t in range(align, min(n, target) + 1, align):
        if n % t == 0:
            best = t
    if best is None:
        raise ValueError(f"no tile for {n} (target {target}, align {align})")
    return best


def _round_up(n, m):
    return m * pl.cdiv(n, m)


def _dot(a, b):
    return jnp.dot(a, b, preferred_element_type=f32)


def _dot_nt(a, b):
    return lax.dot_general(a, b, (((1,), (1,)), ((), ())), preferred_element_type=f32)


def _rope_tile(a, cosf, sinf):
    return a * cosf + pltpu.roll(a, HEAD_DIM // 2, axis=1) * sinf


def _div_pow2(x, d):
    assert d & (d - 1) == 0
    return lax.shift_right_arithmetic(x, d.bit_length() - 1)


def _sigmoid(x):
    return 1.0 / (1.0 + jnp.exp(-x))


def _cast_weights_once(w_refs, wbf_refs, transpose=False):
    @pl.when(pl.program_id(1) == 0)
    def _():
        for w_ref, wbf in zip(w_refs, wbf_refs):
            w = w_ref[...]
            wbf[...] = (w.T if transpose else w).astype(bf16)


def _proj_kernel(x_ref, w_ref, cs_ref, o_ref, wbf, *, mode, scale):
    _cast_weights_once([w_ref], [wbf], transpose=True)
    acc = _dot(x_ref[...], wbf[...])
    if mode == "rope":
        cosf = cs_ref[:, :HEAD_DIM]
        sinf = cs_ref[:, HEAD_DIM:]
        for h in range(acc.shape[1] // HEAD_DIM):
            sl = slice(h * HEAD_DIM, (h + 1) * HEAD_DIM)
            o_ref[:, sl] = (_rope_tile(acc[:, sl], cosf, sinf) * scale).astype(o_ref.dtype)
    elif mode == "sigmoid":
        o_ref[...] = _sigmoid(acc).astype(o_ref.dtype)
    else:
        o_ref[...] = acc.astype(o_ref.dtype)


def _proj(x, w_t, cs, *, col0, n_col, mode, scale=1.0, out_dtype=f32, tm, tn):
    M, K = x.shape
    return pl.pallas_call(
        functools.partial(_proj_kernel, mode=mode, scale=scale),
        grid=(n_col, M // tm),
        in_specs=[pl.BlockSpec((tm, K), lambda j, i: (i, 0)),
                  pl.BlockSpec((tn, K), lambda j, i: (col0 + j, 0)),
                  pl.BlockSpec((tm, 2 * HEAD_DIM), lambda j, i: (i, 0))],
        out_specs=pl.BlockSpec((tm, tn), lambda j, i: (i, j)),
        out_shape=jax.ShapeDtypeStruct((M, n_col * tn), out_dtype),
        scratch_shapes=[pltpu.VMEM((K, tn), bf16)],
        compiler_params=_cparams(("parallel", "arbitrary")),
        name=f"proj_{mode}",
    )(x, w_t, cs)


def _proj_kv_kernel(x_ref, w_ref, cs_ref, st_ref, bf_ref, wbf, *, G):
    j = pl.program_id(0)
    _cast_weights_once([w_ref], [wbf], transpose=True)
    acc = _dot(x_ref[...], wbf[...])
    tm = acc.shape[0]

    def emit(rope):
        for g in range(G):
            y = acc[:, g * HEAD_DIM:(g + 1) * HEAD_DIM]
            if rope:
                y = _rope_tile(y, cs_ref[:, :HEAD_DIM], cs_ref[:, HEAD_DIM:])
            st_ref[pl.ds(g, tm, stride=G), :] = y
            bf_ref[:, g * HEAD_DIM:(g + 1) * HEAD_DIM] = y.astype(bf16)

    is_rope = functools.reduce(jnp.logical_or, [j == s for s in ROPE_KV_SEGS])
    pl.when(is_rope)(lambda: emit(True))
    pl.when(jnp.logical_not(is_rope))(lambda: emit(False))


def _proj_kv(x, w_t, cs, *, col0, n_seg, G, tm):
    M, K = x.shape
    N = G * HEAD_DIM
    return pl.pallas_call(
        functools.partial(_proj_kv_kernel, G=G),
        grid=(n_seg, M // tm),
        in_specs=[pl.BlockSpec((tm, K), lambda j, i: (i, 0)),
                  pl.BlockSpec((N, K), lambda j, i: (col0 + j, 0)),
                  pl.BlockSpec((tm, 2 * HEAD_DIM), lambda j, i: (i, 0))],
        out_specs=[pl.BlockSpec((None, tm * G, HEAD_DIM), lambda j, i: (j, i, 0)),
                   pl.BlockSpec((None, tm, N), lambda j, i: (j, i, 0))],
        out_shape=[jax.ShapeDtypeStruct((n_seg, M * G, HEAD_DIM), f32),
                   jax.ShapeDtypeStruct((n_seg, M, N), bf16)],
        scratch_shapes=[pltpu.VMEM((K, N), bf16)],
        compiler_params=_cparams(("parallel", "arbitrary")),
        name="proj_kv",
    )(x, w_t, cs)


def _gateup_kernel(x_ref, wg_ref, wu_ref, o_ref, wg_bf, wu_bf):
    _cast_weights_once([wg_ref, wu_ref], [wg_bf, wu_bf])
    x = x_ref[...]
    g = _dot(x, wg_bf[...])
    u = _dot(x, wu_bf[...])
    o_ref[...] = (g * _sigmoid(g) * u).astype(o_ref.dtype)


def _gateup(x, wg, wu, *, tm, tf):
    M, K = x.shape
    F = wg.shape[1]
    return pl.pallas_call(
        _gateup_kernel,
        grid=(F // tf, M // tm),
        in_specs=[pl.BlockSpec((tm, K), lambda j, i: (i, 0)),
                  pl.BlockSpec((K, tf), lambda j, i: (0, j)),
                  pl.BlockSpec((K, tf), lambda j, i: (0, j))],
        out_specs=pl.BlockSpec((tm, tf), lambda j, i: (i, j)),
        out_shape=jax.ShapeDtypeStruct((M, F), bf16),
        scratch_shapes=[pltpu.VMEM((K, tf), bf16)] * 2,
        compiler_params=_cparams(("parallel", "arbitrary")),
        name="ffn_gate_up",
    )(x, wg, wu)


def _resid_ln_kernel(*refs, n_pair, alpha, n_j, n_kseg, tn, emit_bf16, n_resid, n_out, top_rows):
    a_refs = refs[0:2 * n_pair:2]
    w_refs = refs[1:2 * n_pair:2]
    pos = 2 * n_pair
    r_refs = refs[pos:pos + n_resid]
    g_ref, b_ref = refs[pos + n_resid:pos + n_resid + 2]
    o_refs = refs[pos + n_resid + 2:pos + n_resid + 2 + n_out]
    obf_ref = refs[pos + n_resid + 2 + n_out] if emit_bf16 else None
    ybuf = refs[-1]
    i = pl.program_id(0)
    k = pl.program_id(1)
    j = pl.program_id(2)
    last_i = pl.num_programs(0) - 1

    def partial_product():
        acc = _dot(a_refs[0][...], w_refs[0][...])
        for a_ref, w_ref in zip(a_refs[1:], w_refs[1:]):
            acc = acc + _dot(a_ref[...], w_ref[...])
        return acc

    @pl.when(k == 0)
    def _():
        prod = partial_product()
        if n_resid == 1:
            ybuf[j] = alpha * r_refs[0][...] + prod
        else:
            @pl.when(i != last_i)
            def _():
                ybuf[j] = alpha * r_refs[0][...] + prod

            @pl.when(i == last_i)
            def _():
                ybuf[j, 0:top_rows, :] = alpha * r_refs[0][0:top_rows, :] + prod[0:top_rows]
                ybuf[j, top_rows:, :] = alpha * r_refs[1][...] + prod[top_rows:]

    if n_kseg > 1:
        @pl.when(k > 0)
        def _():
            ybuf[j] = ybuf[j] + partial_product()

    @pl.when((k == n_kseg - 1) & (j == n_j - 1))
    def _():
        n = n_j * tn
        tot = jnp.sum(ybuf[0], axis=-1, keepdims=True)
        for jj in range(1, n_j):
            tot = tot + jnp.sum(ybuf[jj], axis=-1, keepdims=True)
        mu = tot / n
        var = jnp.zeros_like(mu)
        for jj in range(n_j):
            d = ybuf[jj] - mu
            var = var + jnp.sum(d * d, axis=-1, keepdims=True)
        rstd = lax.rsqrt(var / n + LN_EPS)
        for jj in range(n_j):
            cols = slice(jj * tn, (jj + 1) * tn)
            out = (ybuf[jj] - mu) * rstd * g_ref[:, cols] + b_ref[:, cols]
            o_refs[0][:, cols] = out
            if n_out == 2:
                @pl.when(i == last_i)
                def _():
                    o_refs[1][:, cols] = out[top_rows:]
            if emit_bf16:
                obf_ref[:, cols] = out.astype(bf16)


def _resid_ln(pairs, resids, gamma, beta, *, alpha, tm, tn, n_kseg, emit_bf16, split_rows=None):
    M = pairs[0][0].shape[0]
    N = resids[0].shape[1]
    n_j = N // tn
    n_i = M // tm
    in_specs, args = [], []
    for a, w in pairs:
        ks = a.shape[1] // n_kseg
        assert ks * n_kseg == a.shape[1] and ks % LANES == 0
        in_specs += [pl.BlockSpec((tm, ks), lambda i, k, j: (i, k)),
                     pl.BlockSpec((ks, tn), lambda i, k, j: (k, j))]
        args += [a, w]

    def boundary(rows_first):
        top = rows_first - (n_i - 1) * tm
        assert 0 < top < tm and top % SUBLANES == 0 and rows_first + (tm - top) == M
        return top

    def rcol(k, j):
        return jnp.where(k == 0, j, n_j - 1)

    top_rows = None
    in_specs.append(pl.BlockSpec((tm, tn), lambda i, k, j: (i, rcol(k, j))))
    if len(resids) == 2:
        top_rows = boundary(resids[0].shape[0])
        in_specs.append(pl.BlockSpec((tm - top_rows, tn), lambda i, k, j: (0, rcol(k, j))))
    in_specs += [pl.BlockSpec((1, N), lambda i, k, j: (0, 0)), pl.BlockSpec((1, N), lambda i, k, j: (0, 0))]

    def full_rows(rows):
        return pl.BlockSpec((rows, N), lambda i, k, j: (i, 0), pipeline_mode=pl.Buffered(1))

    if split_rows is None:
        out_specs = [full_rows(tm)]
        out_shape = [jax.ShapeDtypeStruct((M, N), f32)]
    else:
        top_out = boundary(split_rows)
        assert top_rows in (None, top_out)
        top_rows = top_out
        out_specs = [full_rows(tm),
                     pl.BlockSpec((tm - top_rows, N), lambda i, k, j: (0, 0), pipeline_mode=pl.Buffered(1))]
        out_shape = [jax.ShapeDtypeStruct((split_rows, N), f32), jax.ShapeDtypeStruct((M - split_rows, N), f32)]
    n_out = len(out_specs)
    if emit_bf16:
        out_specs.append(full_rows(tm))
        out_shape.append(jax.ShapeDtypeStruct((M, N), bf16))
    return pl.pallas_call(
        functools.partial(_resid_ln_kernel, n_pair=len(pairs), alpha=alpha, n_j=n_j, n_kseg=n_kseg, tn=tn,
                          emit_bf16=emit_bf16, n_resid=len(resids), n_out=n_out, top_rows=top_rows),
        grid=(n_i, n_kseg, n_j),
        in_specs=in_specs,
        out_specs=out_specs,
        out_shape=out_shape,
        scratch_shapes=[pltpu.VMEM((n_j, tm, tn), f32)],
        compiler_params=_cparams(("parallel", "arbitrary", "arbitrary")),
        name="matmul_resid_layernorm",
    )(*args, *resids, gamma, beta)


def _pool_kernel(hist_ref, u_ref, wp_ref, sc_ref, o_ref, buf, *, tm, pos0):
    i = pl.program_id(1)
    H = POOL_MAX

    @pl.when(i == 0)
    def _():
        buf[0:H, :] = hist_ref[...]

    @pl.when(i > 0)
    def _():
        buf[0:H, :] = buf[tm:tm + H, :]

    buf[H:H + tm, :] = u_ref[...]
    pg = wp_ref.shape[1]
    pos = pos0 + i * tm + lax.broadcasted_iota(i32, (tm, 1), 0)
    for g, w in enumerate(POOL_WINDOWS):
        cols = slice(g * pg, (g + 1) * pg)
        cur = buf[H:H + tm, cols]
        tot = cur
        for k in range(1, w):
            tot = tot + buf[H - k:H - k + tm, cols]
        cnt = jnp.minimum(pos + 1, w).astype(f32)
        d = tot / cnt - cur
        y = _dot(d.astype(bf16), wp_ref[g]) * sc_ref[:, cols]
        o_ref[:, cols] = y.astype(o_ref.dtype)


def _pool_mix(hist, u, w_pool, scale, *, nb, n, pos0, tm, out_dtype):
    W = u.shape[1]
    nt = n // tm
    return pl.pallas_call(
        functools.partial(_pool_kernel, tm=tm, pos0=pos0),
        grid=(nb, nt),
        in_specs=[pl.BlockSpec((None, POOL_MAX, W), lambda b, i: (b, 0, 0)),
                  pl.BlockSpec((tm, W), lambda b, i: (b * nt + i, 0)),
                  pl.BlockSpec(w_pool.shape, lambda b, i: (0, 0, 0)),
                  pl.BlockSpec((1, W), lambda b, i: (0, 0))],
        out_specs=pl.BlockSpec((tm, W), lambda b, i: (b * nt + i, 0)),
        out_shape=jax.ShapeDtypeStruct((nb * n, W), out_dtype),
        scratch_shapes=[pltpu.VMEM((POOL_MAX + tm, W), f32)],
        compiler_params=_cparams(("parallel", "arbitrary")),
        name="pool_mix",
    )(hist, u, w_pool, scale)


def _cmp_part_kernel(pt_ref, *refs, n_pg, G):
    pages = refs[:n_pg]
    w_ref, o_ref, y_scr = refs[n_pg:]
    cpp = pages[0].shape[0]
    tok_per_tile = SUBLANES // G
    n_tile = n_pg * cpp
    rows = n_tile * SUBLANES
    width = w_ref.shape[2]
    row_in_tile = lax.broadcasted_iota(i32, (rows, HEAD_DIM), 0) & (SUBLANES - 1)
    acc = None
    for pp in range(CMP_STRIDE // tok_per_tile):
        x = jnp.concatenate([pages[pg][:, pp * SUBLANES:(pp + 1) * SUBLANES, :] for pg in range(n_pg)], axis=0)
        x = x.reshape(rows, HEAD_DIM)
        routed = [jnp.where((row_in_tile >= t * G) & (row_in_tile < (t + 1) * G), x, 0.0)
                  for t in range(tok_per_tile)]
        d = _dot(jnp.concatenate(routed, axis=1).astype(bf16), w_ref[pp])
        acc = d if acc is None else acc + d
    y = acc
    for t in range(1, tok_per_tile):
        y = y + pltpu.roll(acc, rows - t * G, axis=0)
    y_scr[...] = y.reshape(n_tile, SUBLANES, width)
    for g in range(G):
        o_ref[g] = y_scr[:, g, :]


def _cmp_part(pages, page_table, w1cat, *, G, n_pg):
    nb, n_pages = page_table.shape
    assert SUBLANES % G == 0
    chunk_rows = CMP_STRIDE * G
    cpp = pages.shape[1] // chunk_rows
    n_chunk = n_pages * cpp
    n_tile = n_pg * cpp
    width = w1cat.shape[1]
    tok_per_tile = SUBLANES // G
    pages = pages.reshape(pages.shape[0], cpp, chunk_rows, HEAD_DIM)
    w_tiles = w1cat.reshape(CMP_STRIDE // tok_per_tile, tok_per_tile * HEAD_DIM, width)

    def page_spec(pg):
        return pl.BlockSpec((None, cpp, chunk_rows, HEAD_DIM), lambda b, j, pt: (pt[b, j * n_pg + pg], 0, 0, 0))

    grid_spec = pltpu.PrefetchScalarGridSpec(
        num_scalar_prefetch=1,
        grid=(nb, n_pages // n_pg),
        in_specs=[page_spec(pg) for pg in range(n_pg)]
        + [pl.BlockSpec(w_tiles.shape, lambda b, j, pt: (0, 0, 0))],
        out_specs=pl.BlockSpec((None, G, n_tile, width), lambda b, j, pt: (b, 0, j, 0)),
        scratch_shapes=[pltpu.VMEM((n_tile, SUBLANES, width), f32)],
    )
    return pl.pallas_call(
        functools.partial(_cmp_part_kernel, n_pg=n_pg, G=G),
        grid_spec=grid_spec,
        out_shape=jax.ShapeDtypeStruct((nb, G, n_chunk, width), f32),
        compiler_params=_cparams(("parallel", "arbitrary")),
        name="cmp_part",
    )(page_table, *([pages] * n_pg), w_tiles)


def _gelu_tanh(x):
    return 0.5 * x * (1.0 + jnp.tanh(math.sqrt(2.0 / math.pi) * (x + 0.044715 * (x * x * x))))


def _cmp_finish_kernel(part_ref, pe_ref, w1_ref, w2_ref, cs_ref, o_ref, *, rope, transposed):
    G, n_chunk, width = part_ref.shape
    hid0 = _dot(pe_ref[...].astype(bf16), w1_ref[...])[0:1]
    part = part_ref[...].reshape(G * n_chunk, width)
    a = part[:, :HEAD_DIM]
    b = pltpu.roll(part[:, HEAD_DIM:], G * n_chunk - 1, axis=0)
    y = _dot(_gelu_tanh(hid0 + a + b).astype(bf16), w2_ref[...])
    for g in range(G):
        y_g = y[g * n_chunk:(g + 1) * n_chunk]
        if rope:
            y_g = _rope_tile(y_g, cs_ref[:, :HEAD_DIM], cs_ref[:, HEAD_DIM:])
        o_ref[g] = (y_g.T if transposed else y_g).astype(o_ref.dtype)


def _cmp_finish(part, pe, w1flat, w2, cs_cmp, *, rope, transposed):
    nb, G, n_chunk, _ = part.shape
    out_dims = (HEAD_DIM, n_chunk) if transposed else (n_chunk, HEAD_DIM)
    return pl.pallas_call(
        functools.partial(_cmp_finish_kernel, rope=rope, transposed=transposed),
        grid=(nb,),
        in_specs=[pl.BlockSpec((None, G, n_chunk, CMP_RATIO * HEAD_DIM), lambda b: (b, 0, 0, 0)),
                  pl.BlockSpec(pe.shape, lambda b: (0, 0)),
                  pl.BlockSpec(w1flat.shape, lambda b: (0, 0)),
                  pl.BlockSpec(w2.shape, lambda b: (0, 0)),
                  pl.BlockSpec(cs_cmp.shape, lambda b: (0, 0))],
        out_specs=pl.BlockSpec((None, G) + out_dims, lambda b: (b, 0, 0, 0)),
        out_shape=jax.ShapeDtypeStruct((nb, G) + out_dims, bf16),
        compiler_params=_cparams(("parallel",)),
        name="cmp_finish",
    )(part, pe, w1flat, w2, cs_cmp)


def _softmax_terms(s_biased, floor=None):
    m = jnp.max(s_biased, axis=-1, keepdims=True)
    if floor is not None:
        m = jnp.maximum(m, floor)
    e = jnp.exp(s_biased - m)
    return e, 1.0 / jnp.maximum(jnp.sum(e, axis=-1, keepdims=True), 1e-30)


def _dot_split3(p, w01):
    hi = p.astype(bf16)
    r1 = p - hi.astype(f32)
    mid = r1.astype(bf16)
    lo = (r1 - mid.astype(f32)).astype(bf16)
    return _dot(hi, w01) + _dot(mid, w01) + _dot(lo, w01)


def _overlap01(n_chunk, n_cmp, ns):
    n = lax.broadcasted_iota(i32, (n_chunk, ns), 0)
    s = lax.broadcasted_iota(i32, (n_chunk, ns), 1)
    ov = ((n * CMP_STRIDE < s * SLC_LEN + SLC_LEN) & (n * CMP_STRIDE + CMP_LEN - 1 >= s * SLC_LEN)
          & (n < n_cmp))
    return jnp.where(ov, 1.0, 0.0).astype(bf16)


def _cmp_bias(q_pos, n_chunk, n_cmp):
    n_io = lax.broadcasted_iota(i32, (q_pos.shape[0], n_chunk), 1)
    return jnp.where((n_io * CMP_STRIDE + CMP_LEN - 1 <= q_pos) & (n_io < n_cmp), 0.0, NEG)


def _selection_values(imp, s_io, cur):
    forced = (s_io == 0) | (s_io == cur) | (s_io == cur - 1)
    return jnp.where(forced, jnp.inf, jnp.where(s_io <= cur, imp, -jnp.inf))


def _ahead(other, val, other_first):
    return (other > val) | ((other == val) & other_first)


def _online_step(s, v, m_old, l_old, acc_old):
    m_new = jnp.maximum(m_old, jnp.max(s, axis=-1, keepdims=True))
    alpha = jnp.exp(m_old - m_new)
    e = jnp.exp(s - m_new)
    l_new = alpha * l_old + jnp.sum(e, axis=-1, keepdims=True)
    acc_new = alpha * acc_old + _dot(e.astype(bf16), v)
    return m_new, l_new, acc_new


def _prompt_attn_kernel(q_ref, gt_ref, kcb_ref, vcbt_ref, ks_ref, vs_ref, kw_ref, vw_ref, oh_ref, o_ref,
                        kaug, vst_ref, vwt_ref, qaug, val_scr, cnt_scr, acc_scr, *, R, n_cmp, n_slc, kc):
    c = pl.program_id(2)
    tq = Q_TILE
    nq = R * tq
    n_chunk = kcb_ref.shape[0]
    ns = oh_ref.shape[1]
    nr = val_scr.shape[0]
    win_keys = WINDOW + tq

    def transposed(tile):
        return tile.astype(f32).T.astype(bf16)

    @pl.when(c == 0)
    def _():
        kaug[:, :HEAD_DIM] = ks_ref[...]
        kaug[:, HEAD_DIM:] = oh_ref[...]
        for t in range(vst_ref.shape[0]):
            vst_ref[t] = transposed(vs_ref[t * tq:(t + 1) * tq, :])
            vwt_ref[t] = transposed(vw_ref[t * tq:(t + 1) * tq, :])

    gtt = gt_ref[...].T

    def all_heads(a):
        return jnp.concatenate([a] * R, axis=1)

    def weighted_values(v_t, e):
        return _dot(v_t, e.astype(bf16)), 1.0 / jnp.maximum(jnp.sum(e, axis=0, keepdims=True), 1e-30)

    for r in range(R):
        qaug[0:HEAD_DIM, r * tq:(r + 1) * tq] = transposed(q_ref[:, r * HEAD_DIM:(r + 1) * HEAD_DIM])
    q_t = qaug[0:HEAD_DIM, :]
    q_pos = c * tq + lax.broadcasted_iota(i32, (1, tq), 1)

    n_io = lax.broadcasted_iota(i32, (n_chunk, tq), 0)
    bias_c = jnp.where((n_io * CMP_STRIDE + CMP_LEN - 1 <= q_pos) & (n_io < n_cmp), 0.0, NEG)
    s_c = _dot(kcb_ref[...], q_t) + all_heads(bias_c)
    e_c = jnp.exp(s_c - jnp.maximum(jnp.max(s_c, axis=0, keepdims=True), 0.5 * NEG))
    o_c, inv_c = weighted_values(vcbt_ref[...], e_c)
    o_c = o_c * inv_c
    p_c = e_c * inv_c
    p_sum = p_c[:, 0:tq]
    for r in range(1, R):
        p_sum = p_sum + p_c[:, r * tq:(r + 1) * tq]

    blk = lax.broadcasted_iota(i32, (nr, n_chunk), 0)
    n_col = lax.broadcasted_iota(i32, (nr, n_chunk), 1)
    ov = jnp.where((n_col * CMP_STRIDE < blk * SLC_LEN + SLC_LEN)
                   & (n_col * CMP_STRIDE + CMP_LEN - 1 >= blk * SLC_LEN) & (n_col < n_cmp), 1.0, 0.0).astype(bf16)
    hi = p_sum.astype(bf16)
    r1 = p_sum - hi.astype(f32)
    mid = r1.astype(bf16)
    lo = (r1 - mid.astype(f32)).astype(bf16)
    imp_t = _dot(ov, hi) + _dot(ov, mid) + _dot(ov, lo)
    s_io = lax.broadcasted_iota(i32, (nr, tq), 0)
    cur = _div_pow2(c * tq + lax.broadcasted_iota(i32, (nr, tq), 1), SLC_LEN)
    val_scr[...] = _selection_values(imp_t, s_io, cur)
    cnt_scr[...] = jnp.zeros(cnt_scr.shape, f32)
    n_live = _div_pow2(c * tq + tq - 1, SLC_LEN) + 1
    for grp in range(nr // SUBLANES):
        @pl.when(grp * SUBLANES < n_live)
        def _():
            val = val_scr[...]
            cnt = cnt_scr[...]
            for sp in range(grp * SUBLANES, min((grp + 1) * SUBLANES, n_slc)):
                cnt = cnt + jnp.where(_ahead(val_scr[sp:sp + 1, :], val, s_io > sp), 1.0, 0.0)
            cnt_scr[...] = cnt

    first_blk = _div_pow2(c * tq, SLC_LEN)
    keep = (cnt_scr[...] < N_SEL) & (val_scr[...] > -jnp.inf) & (s_io < first_blk)
    mask_t = jnp.where(keep, 0.0, NEG)
    if nr < ns:
        mask_t = jnp.concatenate([mask_t, jnp.full((ns - nr, tq), NEG, f32)], axis=0)
    qaug[HEAD_DIM:, :] = all_heads(mask_t.astype(bf16))

    d0 = pl.multiple_of(c * tq, tq)
    tri = jnp.where(lax.broadcasted_iota(i32, (tq, tq), 0) <= lax.broadcasted_iota(i32, (tq, tq), 1), 0.0, NEG)
    s_d = _dot(ks_ref[pl.ds(d0, tq), :], q_t) + all_heads(tri)
    m_d = jnp.max(s_d, axis=0, keepdims=True)
    e_d = jnp.exp(s_d - m_d)
    l_d = jnp.sum(e_d, axis=0, keepdims=True)
    acc_scr[...] = _dot(vst_ref[c], e_d.astype(bf16))

    tiles_per_chunk = kc // tq

    def sweep(j, carry):
        m_old, l_old = carry
        k0 = pl.multiple_of(j * kc, kc)
        s = _dot(kaug[pl.ds(k0, kc), :], qaug[...])
        m_new = jnp.maximum(m_old, jnp.max(s, axis=0, keepdims=True))
        alpha = jnp.exp(m_old - m_new)
        e = jnp.exp(s - m_new)
        v_t = jnp.concatenate([vst_ref[j * tiles_per_chunk + i] for i in range(tiles_per_chunk)], axis=1)
        acc_scr[...] = alpha * acc_scr[...] + _dot(v_t, e.astype(bf16))
        return m_new, alpha * l_old + jnp.sum(e, axis=0, keepdims=True)

    _, l_s = lax.fori_loop(0, (c * tq + kc - 1) // kc, sweep, (m_d, l_d))
    o_s = acc_scr[...] * (1.0 / jnp.maximum(l_s, 1e-30))

    w_tile = jnp.maximum(c - WINDOW // tq, 0)
    w0 = pl.multiple_of(w_tile * tq, tq)
    key_w = w0 + lax.broadcasted_iota(i32, (win_keys, tq), 0)
    bias_w = jnp.where((key_w <= q_pos) & (q_pos - key_w < WINDOW), 0.0, NEG)
    s_w = _dot(kw_ref[pl.ds(w0, win_keys), :], q_t) + all_heads(bias_w)
    vw_t = jnp.concatenate([vwt_ref[w_tile + i] for i in range(win_keys // tq)], axis=1)
    o_w, inv_w = weighted_values(vw_t, jnp.exp(s_w - jnp.max(s_w, axis=0, keepdims=True)))
    o_w = o_w * inv_w

    for r in range(R):
        cols = slice(r * tq, (r + 1) * tq)
        g = [gtt[r * N_BRANCH + br:r * N_BRANCH + br + 1, :] for br in range(N_BRANCH)]
        o_t = g[0] * o_c[:, cols] + g[1] * o_s[:, cols] + g[2] * o_w[:, cols]
        o_ref[:, r * HEAD_DIM:(r + 1) * HEAD_DIM] = o_t.T.astype(o_ref.dtype)


def _prompt_attn(q, gates, kcb, vcb_t, ks, vs, kw, vw, onehot, *, B, T, G, R):
    n_chunk = kcb.shape[2]
    n_cmp = n_chunk - CMP_RATIO + 1
    n_slc = pl.cdiv(T, SLC_LEN)
    ns = onehot.shape[1]
    nr = _round_up(n_slc, SUBLANES)
    kc = _pick_tile(T, 1024, Q_TILE)
    nt = T // Q_TILE
    assert T % Q_TILE == 0 and T >= WINDOW + Q_TILE and Q_TILE % SLC_LEN == 0 and Q_TILE <= 2 * SLC_LEN
    seq_spec = pl.BlockSpec((T, HEAD_DIM), lambda b, g, c: (b, g))
    return pl.pallas_call(
        functools.partial(_prompt_attn_kernel, R=R, n_cmp=n_cmp, n_slc=n_slc, kc=kc),
        grid=(B, G, nt),
        in_specs=[pl.BlockSpec((Q_TILE, R * HEAD_DIM), lambda b, g, c: (b * nt + c, g)),
                  pl.BlockSpec((Q_TILE, LANES), lambda b, g, c: (b * nt + c, g)),
                  pl.BlockSpec((None, None, n_chunk, HEAD_DIM), lambda b, g, c: (b, g, 0, 0)),
                  pl.BlockSpec((None, None, HEAD_DIM, n_chunk), lambda b, g, c: (b, g, 0, 0)),
                  seq_spec, seq_spec, seq_spec, seq_spec,
                  pl.BlockSpec((T, ns), lambda b, g, c: (0, 0))],
        out_specs=pl.BlockSpec((Q_TILE, R * HEAD_DIM), lambda b, g, c: (b * nt + c, g)),
        out_shape=jax.ShapeDtypeStruct((B * T, G * R * HEAD_DIM), bf16),
        scratch_shapes=[pltpu.VMEM((T, HEAD_DIM + ns), bf16),
                        pltpu.VMEM((nt, HEAD_DIM, Q_TILE), bf16), pltpu.VMEM((nt, HEAD_DIM, Q_TILE), bf16),
                        pltpu.VMEM((HEAD_DIM + ns, R * Q_TILE), bf16),
                        pltpu.VMEM((nr, Q_TILE), f32), pltpu.VMEM((nr, Q_TILE), f32),
                        pltpu.VMEM((HEAD_DIM, R * Q_TILE), f32)],
        compiler_params=_cparams(("parallel", "parallel", "arbitrary")),
        name="nsa_prompt",
    )(q, gates, kcb, vcb_t, ks, vs, kw, vw, onehot)


def _sample_cmp_win_kernel(q_ref, gt_ref, kcb_ref, vcb_ref, kwin_ref, vwin_ref, kwn_ref, vwn_ref,
                           part_ref, qaug_ref, kbuf, vbuf, *, G, R, n_cmp, n_slc, past):
    n_chunk = kcb_ref.shape[1]
    ns = qaug_ref.shape[1] - G * HEAD_DIM
    wb = kwin_ref.shape[0] // G
    sp = SUBLANES
    rows = R * sp
    nbuf = kbuf.shape[0]
    q_pos = past + (lax.broadcasted_iota(i32, (rows, 1), 0) & (sp - 1))
    bias_c = _cmp_bias(q_pos, n_chunk, n_cmp)
    key_w = past - wb + lax.broadcasted_iota(i32, (rows, nbuf), 1)
    bias_w = jnp.where((key_w <= q_pos) & (q_pos - key_w < WINDOW) & (key_w >= 0), 0.0, NEG)
    ov = _overlap01(n_chunk, n_cmp, ns)
    s_io = lax.broadcasted_iota(i32, (sp, ns), 1)
    cur = _div_pow2(q_pos[0:sp], SLC_LEN)
    qaug_ref[:, :G * HEAD_DIM] = jnp.zeros((G * rows, G * HEAD_DIM), bf16)

    for g in range(G):
        q = q_ref[g]
        cols = slice(g * HEAD_DIM, (g + 1) * HEAD_DIM)

        e, inv = _softmax_terms(_dot_nt(q, kcb_ref[g]) + bias_c, floor=0.5 * NEG)
        p_c = e * inv
        o_c = _dot(p_c.astype(bf16), vcb_ref[g])
        p_sum = p_c[0:sp]
        for r in range(1, R):
            p_sum = p_sum + p_c[r * sp:(r + 1) * sp]
        val = _selection_values(_dot_split3(p_sum, ov), s_io, cur)
        cnt = jnp.zeros((sp, ns), f32)
        for blk in range(n_slc):
            cnt = cnt + jnp.where(_ahead(val[:, blk:blk + 1], val, s_io > blk), 1.0, 0.0)
        mask = jnp.where((cnt < N_SEL) & (val > -jnp.inf), 0.0, NEG).astype(bf16)

        qaug_ref[g * rows:(g + 1) * rows, cols] = q
        qaug_ref[g * rows:(g + 1) * rows, G * HEAD_DIM:] = jnp.concatenate([mask] * R, axis=0)

        kbuf[0:wb, :] = kwin_ref[pl.ds(g, wb, stride=G), :].astype(bf16)
        kbuf[wb:nbuf, :] = kwn_ref[:, cols]
        vbuf[0:wb, :] = vwin_ref[pl.ds(g, wb, stride=G), :].astype(bf16)
        vbuf[wb:nbuf, :] = vwn_ref[:, cols]
        e, inv = _softmax_terms(_dot_nt(q, kbuf[...]) + bias_w)
        o_w = _dot(e.astype(bf16), vbuf[...]) * inv
        part_ref[g] = gt_ref[g][:, 0:1] * o_c + gt_ref[g][:, 2:3] * o_w


def _sample_cmp_win(q, gates, kcb, vcb, kwin, vwin, kw_new, vw_new, *, past, n_slc, ns):
    DB, G, rows, _ = q.shape
    R = rows // SUBLANES
    n_chunk = kcb.shape[2]
    n_cmp = n_chunk - CMP_RATIO + 1
    wbg = kwin.shape[1]
    npad = kw_new.shape[1]
    qspec = pl.BlockSpec((None, G, rows, HEAD_DIM), lambda b: (b, 0, 0, 0))
    cspec = pl.BlockSpec((None, G, n_chunk, HEAD_DIM), lambda b: (b, 0, 0, 0))
    wspec = pl.BlockSpec((None, wbg, HEAD_DIM), lambda b: (b, 0, 0))
    nspec = pl.BlockSpec((None, npad, G * HEAD_DIM), lambda b: (b, 0, 0))
    aug_w = G * HEAD_DIM + ns
    return pl.pallas_call(
        functools.partial(_sample_cmp_win_kernel, G=G, R=R, n_cmp=n_cmp, n_slc=n_slc, past=past),
        grid=(DB,),
        in_specs=[qspec, qspec, cspec, cspec, wspec, wspec, nspec, nspec],
        out_specs=[qspec, pl.BlockSpec((None, G * rows, aug_w), lambda b: (b, 0, 0))],
        out_shape=[jax.ShapeDtypeStruct((DB, G, rows, HEAD_DIM), f32),
                   jax.ShapeDtypeStruct((DB, G * rows, aug_w), bf16)],
        scratch_shapes=[pltpu.VMEM((wbg // G + npad, HEAD_DIM), bf16)] * 2,
        compiler_params=_cparams(("parallel",)),
        name="nsa_sample_cmp_win",
    )(q, gates, kcb, vcb, kwin, vwin, kw_new, vw_new)


def _sample_sel_kernel(pt_ref, *refs, n_pg, G, R, page):
    kpages = refs[:n_pg]
    vpages = refs[n_pg:2 * n_pg]
    (oh_ref, ohn_ref, qaug_ref, gt_ref, part_ref, ksn_ref, vsn_ref, o_ref,
     kaug, vall, m_scr, l_scr, acc_scr) = refs[2 * n_pg:]
    j = pl.program_id(1)
    sp = SUBLANES
    rows = R * sp
    kvw = G * HEAD_DIM

    @pl.when(j == 0)
    def _():
        m_scr[...] = jnp.full(m_scr.shape, NEG, f32)
        l_scr[...] = jnp.zeros(l_scr.shape, f32)
        acc_scr[...] = jnp.zeros(acc_scr.shape, f32)

    def update(n_keys, bias):
        s = _dot_nt(qaug_ref[...], kaug[0:n_keys, :])
        if bias is not None:
            s = s + bias
        m, l, acc = _online_step(s, vall[0:n_keys, :], m_scr[...], l_scr[...], acc_scr[...])
        m_scr[...] = m
        l_scr[...] = l
        acc_scr[...] = acc

    for pg in range(n_pg):
        for g in range(G):
            kaug[pg * page:(pg + 1) * page, g * HEAD_DIM:(g + 1) * HEAD_DIM] = (
                kpages[pg][pl.ds(g, page, stride=G), :].astype(bf16))
            vall[pg * page:(pg + 1) * page, g * HEAD_DIM:(g + 1) * HEAD_DIM] = (
                vpages[pg][pl.ds(g, page, stride=G), :].astype(bf16))
    kaug[:, kvw:] = oh_ref[...]
    update(n_pg * page, None)

    @pl.when(j == pl.num_programs(1) - 1)
    def _():
        kaug[0:page, :kvw] = ksn_ref[...]
        kaug[0:page, kvw:] = ohn_ref[...]
        vall[0:page, :] = vsn_ref[...]
        step = lax.broadcasted_iota(i32, (G * rows, page), 0) & (sp - 1)
        causal = jnp.where(lax.broadcasted_iota(i32, (G * rows, page), 1) <= step, 0.0, NEG)
        update(page, causal)
        inv = 1.0 / jnp.maximum(l_scr[...], 1e-30)
        for g in range(G):
            rs = slice(g * rows, (g + 1) * rows)
            o_s = acc_scr[rs, g * HEAD_DIM:(g + 1) * HEAD_DIM] * inv[rs]
            o_ref[g] = part_ref[g] + gt_ref[g][:, 1:2] * o_s


def _sample_sel(kcache, vcache, page_table, onehot, qaug, gates, part, ks_new, vs_new, *, G, n_pg):
    DB, _, rows, _ = part.shape
    R = rows // SUBLANES
    n_pages = page_table.shape[1]
    page = kcache.shape[1] // G
    ns = onehot.shape[1]
    aug_w = G * HEAD_DIM + ns
    n_keys = n_pg * page

    def page_spec(pg):
        return pl.BlockSpec((None, page * G, HEAD_DIM), lambda b, j, pt: (pt[b, j * n_pg + pg], 0, 0))

    pspec = pl.BlockSpec((None, G, rows, HEAD_DIM), lambda b, j, pt: (b, 0, 0, 0))
    nspec = pl.BlockSpec((None, page, G * HEAD_DIM), lambda b, j, pt: (b, 0, 0))
    grid_spec = pltpu.PrefetchScalarGridSpec(
        num_scalar_prefetch=1,
        grid=(DB, n_pages // n_pg),
        in_specs=[page_spec(pg) for pg in range(n_pg)] * 2
        + [pl.BlockSpec((n_keys, ns), lambda b, j, pt: (j, 0)),
           pl.BlockSpec((page, ns), lambda b, j, pt: (n_pages, 0)),
           pl.BlockSpec((None, G * rows, aug_w), lambda b, j, pt: (b, 0, 0)),
           pspec, pspec, nspec, nspec],
        out_specs=pspec,
        scratch_shapes=[pltpu.VMEM((n_keys, aug_w), bf16), pltpu.VMEM((n_keys, G * HEAD_DIM), bf16),
                        pltpu.VMEM((G * rows, 1), f32), pltpu.VMEM((G * rows, 1), f32),
                        pltpu.VMEM((G * rows, G * HEAD_DIM), f32)],
    )
    return pl.pallas_call(
        functools.partial(_sample_sel_kernel, n_pg=n_pg, G=G, R=R, page=page),
        grid_spec=grid_spec,
        out_shape=jax.ShapeDtypeStruct(part.shape, f32),
        compiler_params=_cparams(("parallel", "arbitrary")),
        name="nsa_sample_sel",
    )(page_table, *([kcache] * n_pg), *([vcache] * n_pg), onehot, onehot, qaug, gates, part, ks_new, vs_new)


def _rope_table(pos):
    half = HEAD_DIM // 2
    inv = ROPE_THETA ** (-jnp.arange(half, dtype=f32) / half)
    ang = pos.astype(f32)[:, None] * inv[None, :]
    cos, sin = jnp.cos(ang), jnp.sin(ang)
    return jnp.concatenate([cos, cos, -sin, sin], axis=-1)


def _block_onehot(n_keys, ns):
    blk = jnp.arange(n_keys, dtype=i32)[:, None] // SLC_LEN
    return (blk == jnp.arange(ns, dtype=i32)[None, :]).astype(bf16)


def kernel(x_prompt, x_sample, cache_k_cmp, cache_v_cmp, cache_k_slc, cache_v_slc, state_k_win, state_v_win, state_pool, page_table, w_in, w_cmp1_k, pe_cmp_k, w_cmp2_k, w_cmp1_v, pe_cmp_v, w_cmp2_v, w_pool, pool_scale, w_o, ln1_g, ln1_b, w_gate, w_up, w_down, ln2_g, ln2_b):
    B, T, D = x_prompt.shape
    DB, S, _ = x_sample.shape
    depth, n_phys, page, G, _ = cache_k_cmp.shape
    assert depth == 1 and S <= SUBLANES
    n_pages = page_table.shape[1]
    past = n_pages * page
    PW = state_pool.shape[3]
    NW = D - PW
    NH = NW // HEAD_DIM
    R = NH // G
    KVW = G * HEAD_DIM
    F = w_gate.shape[2]
    wb = state_k_win.shape[2]
    wbp = min(WINDOW, T)
    alpha = (2 * depth) ** 0.25
    MP = B * T
    MS = DB * S
    M = MP + MS
    assert T % page == 0 and page % CMP_STRIDE == 0 and page % SLC_LEN == 0 and past % SLC_LEN == 0

    x_p, x_s = x_prompt.reshape(MP, D), x_sample.reshape(MS, D)
    x_bf = jnp.concatenate([x_p.astype(bf16), x_s.astype(bf16)], axis=0)
    w_in_t = w_in[0].T
    o_kv = PW + NW
    o_g = o_kv + N_KV_SEG * KVW
    w_g_t = w_in_t[o_g:o_g + N_BRANCH * NH].reshape(G, R * N_BRANCH, D)
    w_g_t = jnp.pad(w_g_t, ((0, 0), (0, LANES - R * N_BRANCH), (0, 0))).reshape(G * LANES, D)

    pos_all = jnp.concatenate([jnp.tile(jnp.arange(T, dtype=i32), B),
                               jnp.tile(past + jnp.arange(S, dtype=i32), DB)])
    cs_all = _rope_table(pos_all)

    tm = _pick_tile(M, 640, SUBLANES)
    tm_proj = _pick_tile(M, 832, SUBLANES)
    assert PW % KVW == 0 and NW % KVW == 0
    u = _proj(x_bf, w_in_t, cs_all, col0=0, n_col=PW // KVW, mode="none", tm=tm_proj, tn=KVW)
    q = _proj(x_bf, w_in_t, cs_all, col0=PW // KVW, n_col=NW // KVW, mode="rope", scale=HEAD_DIM ** -0.5,
              out_dtype=bf16, tm=tm_proj, tn=KVW)
    gates = _proj(x_bf, w_g_t, cs_all, col0=0, n_col=1, mode="sigmoid", tm=tm_proj, tn=G * LANES)
    kv_st, kv_bf = _proj_kv(x_bf, w_in_t, cs_all, col0=o_kv // KVW, n_seg=N_KV_SEG, G=G, tm=tm_proj)
    kc_st, vc_st = kv_st[0], kv_st[1]
    ks_bf, vs_bf, kw_bf, vw_bf = kv_bf[2], kv_bf[3], kv_bf[4], kv_bf[5]

    wp_bf = w_pool[0].astype(bf16)
    pscale = pool_scale[0].reshape(1, PW)
    pool_p = _pool_mix(jnp.zeros((B, POOL_MAX, PW), f32), u, wp_bf, pscale, nb=B, n=T, pos0=0,
                       tm=_pick_tile(T, 512, SUBLANES), out_dtype=bf16)
    u_s = u[MP:].reshape(DB, S, PW)
    hist_s = jnp.pad(state_pool[0], ((0, 0), (POOL_MAX - state_pool.shape[2], 0), (0, 0)))
    u_s_pad = jnp.pad(u_s, ((0, 0), (0, SUBLANES - S), (0, 0))).reshape(DB * SUBLANES, PW)
    pool_s = _pool_mix(hist_s, u_s_pad, wp_bf, pscale, nb=DB, n=SUBLANES, pos0=past, tm=SUBLANES,
                       out_dtype=f32).reshape(DB, SUBLANES, PW)[:, :S]

    def cmp_weights(w1, pe, w2):
        w1cat = w1[0].reshape(CMP_RATIO, CMP_STRIDE, HEAD_DIM, HEAD_DIM).transpose(1, 2, 0, 3)
        w1cat = w1cat.reshape(CMP_STRIDE * HEAD_DIM, CMP_RATIO * HEAD_DIM).astype(bf16)
        pe8 = jnp.broadcast_to(pe[0].reshape(1, CMP_LEN * HEAD_DIM), (SUBLANES, CMP_LEN * HEAD_DIM))
        return w1cat, pe8, w1[0].reshape(CMP_LEN * HEAD_DIM, HEAD_DIM).astype(bf16), w2[0].astype(bf16)

    cw_k = cmp_weights(w_cmp1_k, pe_cmp_k, w_cmp2_k)
    cw_v = cmp_weights(w_cmp1_v, pe_cmp_v, w_cmp2_v)

    def compress(pages, table, cw, rope, transposed=False):
        n_chunk = table.shape[1] * (page // CMP_STRIDE)
        cs_cmp = _rope_table(jnp.arange(n_chunk, dtype=i32) * CMP_STRIDE + CMP_LEN - 1)
        part = _cmp_part(pages, table, cw[0], G=G, n_pg=_pick_tile(table.shape[1], 16, 1))
        return _cmp_finish(part, cw[1], cw[2], cw[3], cs_cmp, rope=rope, transposed=transposed)

    def as_pages(a):
        return a.reshape(-1, page * G, HEAD_DIM)

    table_p = jnp.arange(MP // page, dtype=i32).reshape(B, T // page)
    kcb_p = compress(as_pages(kc_st[:MP * G]), table_p, cw_k, True)
    vcb_p_t = compress(as_pages(vc_st[:MP * G]), table_p, cw_v, False, transposed=True)
    kcb_s = compress(as_pages(cache_k_cmp.reshape(-1, HEAD_DIM)), page_table, cw_k, True)
    vcb_s = compress(as_pages(cache_v_cmp.reshape(-1, HEAD_DIM)), page_table, cw_v, False)

    ns_p = _round_up(pl.cdiv(T, SLC_LEN), LANES)
    nsa_p = _prompt_attn(q, gates, kcb_p, vcb_p_t, ks_bf, vs_bf, kw_bf, vw_bf, _block_onehot(T, ns_p),
                         B=B, T=T, G=G, R=R)

    def srow(a):
        return a[MP:].reshape(DB, S, a.shape[1])

    def to_rows(a):
        a = jnp.pad(a.transpose(0, 2, 3, 1, 4), ((0, 0), (0, 0), (0, 0), (0, SUBLANES - S), (0, 0)))
        return a.reshape(DB, G, R * SUBLANES, a.shape[-1])

    q_s = to_rows(srow(q).reshape(DB, S, G, R, HEAD_DIM))
    g_s = srow(gates).reshape(DB, S, G, LANES)[..., :R * N_BRANCH].reshape(DB, S, G, R, N_BRANCH)
    g_s = to_rows(jnp.pad(g_s, ((0, 0),) * 4 + ((0, LANES - N_BRANCH),)))

    def new_rows(a):
        return jnp.pad(srow(a), ((0, 0), (0, page - S), (0, 0)))

    n_slc_s = pl.cdiv(past + S, SLC_LEN)
    ns_s = _round_up(n_slc_s, LANES)
    part_s, qaug_s = _sample_cmp_win(q_s, g_s, kcb_s, vcb_s,
                                     state_k_win.reshape(DB, wb * G, HEAD_DIM),
                                     state_v_win.reshape(DB, wb * G, HEAD_DIM),
                                     new_rows(kw_bf), new_rows(vw_bf), past=past, n_slc=n_slc_s, ns=ns_s)
    o_s = _sample_sel(as_pages(cache_k_slc.reshape(-1, HEAD_DIM)), as_pages(cache_v_slc.reshape(-1, HEAD_DIM)),
                      page_table, _block_onehot(past + page, ns_s), qaug_s, g_s, part_s,
                      new_rows(ks_bf), new_rows(vs_bf), G=G, n_pg=_pick_tile(n_pages, 16, 1))
    nsa_s = o_s.reshape(DB, G, R, SUBLANES, HEAD_DIM)[:, :, :, :S].transpose(0, 3, 1, 2, 4)
    nsa_s = nsa_s.reshape(MS, NW).astype(bf16)

    pool_all = jnp.concatenate([pool_p, pool_s.reshape(MS, PW).astype(bf16)], axis=0)
    nsa_all = jnp.concatenate([nsa_p, nsa_s], axis=0)
    wo_bf = w_o[0].astype(bf16)
    h, h_bf = _resid_ln([(pool_all, wo_bf[:PW]), (nsa_all, wo_bf[PW:])], [x_p, x_s],
                        ln1_g[0].reshape(1, D), ln1_b[0].reshape(1, D),
                        alpha=alpha, tm=tm, tn=_pick_tile(D, 512, LANES), n_kseg=1, emit_bf16=True)
    ff = _gateup(h_bf, w_gate[0], w_up[0], tm=_pick_tile(M, 1040, SUBLANES), tf=_pick_tile(F, 512, LANES))
    n_kseg = 2 if F % (2 * LANES) == 0 else 1
    y_p, y_s = _resid_ln([(ff, w_down[0].astype(bf16))], [h], ln2_g[0].reshape(1, D), ln2_b[0].reshape(1, D),
                         alpha=alpha, tm=tm, tn=_pick_tile(D, 256, LANES), n_kseg=n_kseg, emit_bf16=False,
                         split_rows=MP)

    def pstate(i):
        return kv_st[i, :MP * G].reshape(1, B, T, G, HEAD_DIM)

    def sstate(i):
        return kv_st[i, MP * G:].reshape(1, DB, S, G, HEAD_DIM)

    kw_p, vw_p = pstate(4)[:, :, T - wbp:], pstate(5)[:, :, T - wbp:]
    pool_state_p = u[:MP].reshape(1, B, T, PW)[:, :, T - (POOL_MAX - 1):]
    kw_s = jnp.concatenate([state_k_win, sstate(4)], axis=2)[:, :, S:]
    vw_s = jnp.concatenate([state_v_win, sstate(5)], axis=2)[:, :, S:]
    pool_state_s = jnp.concatenate([state_pool, u_s[None]], axis=2)[:, :, S:]
    return (y_p.reshape(B, T, D), y_s.reshape(DB, S, D),
            pstate(0), pstate(1), pstate(2), pstate(3), kw_p, vw_p, pool_state_p,
            sstate(0), sstate(1), sstate(2), sstate(3), kw_s, vw_s, pool_state_s)
```
